```python
import math, functools
import jax, jax.numpy as jnp
from jax import lax
import numpy as np


D_MODEL = 2048
BATCH = 2
SEQ = 4096
DEPTH = 1
DEC_BATCH = 128
DEC_SEQ = 8
PAST_LEN = 8192
PAGE_SIZE = 128

N_HEADS = 32
KV_HEADS = 4
HEAD_DIM = 64
GROUP = N_HEADS // KV_HEADS
WINDOW = 128
BLOCK = 128
N_BUCKETS = 32
MAX_DISTANCE = 128
D_CONV = D_MODEL
CONV_WIDTH = 31
D_FF = 4 * D_MODEL
FFN_CONV_WIDTH = 3
EPS = 1e-6
D_Q = N_HEADS * HEAD_DIM
D_KV = KV_HEADS * HEAD_DIM
D_IN = 2 * D_CONV + D_Q + 2 * D_KV + 2 * D_MODEL

kernel_name = "hybrid_conformer_swa_sink_convffn_step"


def rms_norm(x, g):
    xf = x.astype(jnp.float32)
    y = xf * lax.rsqrt(jnp.mean(xf * xf, axis=-1, keepdims=True) + EPS)
    return (y * g.astype(jnp.float32)).astype(x.dtype)


def layer_norm(x, g, b):
    xf = x.astype(jnp.float32)
    mu = jnp.mean(xf, axis=-1, keepdims=True)
    var = jnp.mean(jnp.square(xf - mu), axis=-1, keepdims=True)
    y = (xf - mu) * lax.rsqrt(var + EPS)
    return (y * g.astype(jnp.float32) + b.astype(jnp.float32)).astype(x.dtype)


def causal_depthwise_conv(x_ext, k, b):
    c = x_ext.shape[-1]
    y = lax.conv_general_dilated(x_ext, k[:, None, :].astype(x_ext.dtype), window_strides=(1,),
                                 padding='VALID', dimension_numbers=('NWC', 'WIO', 'NWC'),
                                 feature_group_count=c)
    return y + b


def rel_bucket(dist):
    max_exact = N_BUCKETS // 2
    d = jnp.maximum(dist, 1).astype(jnp.float32)
    large = max_exact + (jnp.log(d / max_exact) / math.log(MAX_DISTANCE / max_exact)
                         * (N_BUCKETS - max_exact)).astype(jnp.int32)
    large = jnp.minimum(large, N_BUCKETS - 1)
    return jnp.where(dist < max_exact, dist, large)


def window_softmax_attention(q, k, v, dist, valid, sinks, rel_bias):
    n, nq = q.shape[:2]
    nk = k.shape[1]
    qg = q.reshape(n, nq, KV_HEADS, GROUP, HEAD_DIM)
    s = jnp.einsum('nqhgd,nshd->nhgqs', qg, k).astype(jnp.float32) * (HEAD_DIM ** -0.5)
    bias = rel_bias[rel_bucket(jnp.maximum(dist, 0))].astype(jnp.float32)
    bias = jnp.transpose(bias, (2, 0, 1)).reshape(KV_HEADS, GROUP, nq, nk)
    s = jnp.where(valid[:, None, None], s + bias, -jnp.inf)
    sink = sinks.astype(jnp.float32).reshape(KV_HEADS, GROUP, 1, 1)
    m = jnp.maximum(jnp.max(s, axis=-1, keepdims=True), sink)
    p = jnp.exp(s - m)
    p = p / (jnp.sum(p, axis=-1, keepdims=True) + jnp.exp(sink - m))
    o = jnp.einsum('nhgqs,nshd->nqhgd', p.astype(v.dtype), v)
    return o.reshape(n, nq, D_Q)


def prompt_attention(q, k, v, keep, sinks, rel_bias):
    b, t = q.shape[:2]
    nb = t // BLOCK
    qb = q.reshape(b * nb, BLOCK, N_HEADS, HEAD_DIM)
    kb = k.reshape(b, nb, BLOCK, KV_HEADS, HEAD_DIM)
    vb = v.reshape(b, nb, BLOCK, KV_HEADS, HEAD_DIM)
    pad = ((0, 0), (1, 0), (0, 0), (0, 0), (0, 0))
    k_band = jnp.concatenate([jnp.pad(kb, pad)[:, :-1], kb], axis=2).reshape(b * nb, 2 * BLOCK, KV_HEADS, HEAD_DIM)
    v_band = jnp.concatenate([jnp.pad(vb, pad)[:, :-1], vb], axis=2).reshape(b * nb, 2 * BLOCK, KV_HEADS, HEAD_DIM)
    qi = jnp.arange(BLOCK)[:, None]
    kr = jnp.arange(2 * BLOCK)[None, :]
    dist = qi + BLOCK - kr
    k_pos = jnp.arange(nb)[:, None, None] * BLOCK - BLOCK + kr
    valid = (dist >= 0) & (dist < WINDOW) & (k_pos >= 0)
    valid = jnp.tile(valid, (b, 1, 1))
    o = window_softmax_attention(qb, k_band, v_band, dist, valid, sinks, rel_bias)
    return o.reshape(b, t, D_Q), k[:, t - keep:], v[:, t - keep:]


def sample_attention(q, k, v, cache_k, cache_v, sinks, rel_bias):
    t = q.shape[1]
    keep = cache_k.shape[1]
    k_all = jnp.concatenate([cache_k, k], axis=1)
    v_all = jnp.concatenate([cache_v, v], axis=1)
    dist = jnp.arange(t)[:, None] + keep - jnp.arange(keep + t)[None, :]
    valid = ((dist >= 0) & (dist < WINDOW))[None]
    o = window_softmax_attention(q, k_all, v_all, dist, valid, sinks, rel_bias)
    return o, k_all[:, -keep:], v_all[:, -keep:]


def trunk_layer(x, conv_prev, ffn_prev, attention, params):
    (g_pre, w_in, b_in, conv_k, conv_b, ln_g, ln_b, w_conv_proj, w_attn_proj, w_out, g_post,
     g_ffn_pre, w_up, ffn_k, ffn_b, w_down, g_ffn_post) = params
    n, t, _ = x.shape
    xn = rms_norm(x, g_pre)
    u = xn @ w_in + b_in
    idx = np.cumsum([D_CONV, D_CONV, D_Q, D_KV, D_KV, D_MODEL]).tolist()
    glu_a, glu_b, q, k, v, gate_c, gate_a = jnp.split(u, idx, axis=-1)
    glu = glu_a * jax.nn.sigmoid(glu_b)
    glu_ext = jnp.concatenate([conv_prev, glu], axis=1)
    c = causal_depthwise_conv(glu_ext, conv_k, conv_b)
    conv_out = jax.nn.silu(layer_norm(c, ln_g, ln_b)) @ w_conv_proj
    new_conv = glu_ext[:, -(CONV_WIDTH - 1):]
    attn, new_k, new_v = attention(q.reshape(n, t, N_HEADS, HEAD_DIM),
                                   k.reshape(n, t, KV_HEADS, HEAD_DIM),
                                   v.reshape(n, t, KV_HEADS, HEAD_DIM))
    attn_out = attn @ w_attn_proj
    mixed = jax.nn.sigmoid(gate_c) * conv_out + jax.nn.sigmoid(gate_a) * attn_out
    h = x + rms_norm(mixed @ w_out, g_post)
    up = rms_norm(h, g_ffn_pre) @ w_up
    up_ext = jnp.concatenate([ffn_prev, up], axis=1)
    gate, val = jnp.split(causal_depthwise_conv(up_ext, ffn_k, ffn_b), 2, axis=-1)
    y = h + rms_norm((jax.nn.gelu(gate, approximate=True) * val) @ w_down, g_ffn_post)
    return y, new_k, new_v, new_conv, up_ext[:, -(FFN_CONV_WIDTH - 1):]


def setup_inputs(seed: int = 0) -> dict:
    key = jax.random.key(seed)
    ks = jax.random.split(key, 32)
    f32 = jnp.float32
    keep = min(WINDOW, PAST_LEN)

    def nrm(k, shape, scale):
        return jax.random.normal(k, shape, f32) * scale

    def gain(k, shape):
        return 1.0 + 0.01 * jax.random.normal(k, shape, f32)

    return {
        "x_prompt": nrm(ks[0], (BATCH, SEQ, D_MODEL), 1.0),
        "x_sample": nrm(ks[1], (DEC_BATCH, DEC_SEQ, D_MODEL), 1.0),
        "cache_k": nrm(ks[2], (DEPTH, DEC_BATCH, keep, KV_HEADS, HEAD_DIM), 1.0),
        "cache_v": nrm(ks[3], (DEPTH, DEC_BATCH, keep, KV_HEADS, HEAD_DIM), 1.0),
        "state_conv": nrm(ks[4], (DEPTH, DEC_BATCH, CONV_WIDTH - 1, D_CONV), 0.5),
        "state_ffn_conv": nrm(ks[5], (DEPTH, DEC_BATCH, FFN_CONV_WIDTH - 1, 2 * D_FF), 1.0),
        "norm_mix_pre": gain(ks[6], (DEPTH, D_MODEL)),
        "w_in": nrm(ks[7], (DEPTH, D_MODEL, D_IN), D_MODEL ** -0.5),
        "b_in": nrm(ks[8], (DEPTH, D_IN), 0.01),
        "conv_dw_k": nrm(ks[9], (DEPTH, CONV_WIDTH, D_CONV), CONV_WIDTH ** -0.5),
        "conv_dw_b": nrm(ks[10], (DEPTH, D_CONV), 0.01),
        "conv_ln_g": gain(ks[11], (DEPTH, D_CONV)),
        "conv_ln_b": nrm(ks[12], (DEPTH, D_CONV), 0.01),
        "w_conv_proj": nrm(ks[13], (DEPTH, D_CONV, D_MODEL), D_CONV ** -0.5),
        "attn_sinks": nrm(ks[14], (DEPTH, N_HEADS), 0.5),
        "rel_bias": nrm(ks[15], (N_BUCKETS, N_HEADS), 0.1),
        "w_attn_proj": nrm(ks[16], (DEPTH, D_Q, D_MODEL), D_Q ** -0.5),
        "w_out": nrm(ks[17], (DEPTH, D_MODEL, D_MODEL), D_MODEL ** -0.5),
        "norm_mix_post": gain(ks[18], (DEPTH, D_MODEL)),
        "norm_ffn_pre": gain(ks[19], (DEPTH, D_MODEL)),
        "w_up": nrm(ks[20], (DEPTH, D_MODEL, 2 * D_FF), D_MODEL ** -0.5),
        "ffn_dw_k": nrm(ks[21], (DEPTH, FFN_CONV_WIDTH, 2 * D_FF), FFN_CONV_WIDTH ** -0.5),
        "ffn_dw_b": nrm(ks[22], (DEPTH, 2 * D_FF), 0.01),
        "w_down": nrm(ks[23], (DEPTH, D_FF, D_MODEL), D_FF ** -0.5),
        "norm_ffn_post": gain(ks[24], (DEPTH, D_MODEL)),
    }


def reference(x_prompt, x_sample, cache_k, cache_v, state_conv, state_ffn_conv,
              norm_mix_pre, w_in, b_in, conv_dw_k, conv_dw_b, conv_ln_g, conv_ln_b, w_conv_proj,
              attn_sinks, rel_bias, w_attn_proj, w_out, norm_mix_post,
              norm_ffn_pre, w_up, ffn_dw_k, ffn_dw_b, w_down, norm_ffn_post):
    keep = cache_k.shape[2]
    b = x_prompt.shape[0]
    y_p, y_s = x_prompt, x_sample
    kp, vp, cp, fp, ksl, vsl, csl, fsl = ([] for _ in range(8))
    for l in range(DEPTH):
        params = (norm_mix_pre[l], w_in[l], b_in[l], conv_dw_k[l], conv_dw_b[l], conv_ln_g[l], conv_ln_b[l],
                  w_conv_proj[l], w_attn_proj[l], w_out[l], norm_mix_post[l],
                  norm_ffn_pre[l], w_up[l], ffn_dw_k[l], ffn_dw_b[l], w_down[l], norm_ffn_post[l])
        attn_p = functools.partial(prompt_attention, keep=keep, sinks=attn_sinks[l], rel_bias=rel_bias)
        conv0 = jnp.zeros((b, CONV_WIDTH - 1, D_CONV), x_prompt.dtype)
        ffn0 = jnp.zeros((b, FFN_CONV_WIDTH - 1, 2 * D_FF), x_prompt.dtype)
        y_p, k1, v1, c1, f1 = trunk_layer(y_p, conv0, ffn0, attn_p, params)
        attn_s = functools.partial(sample_attention, cache_k=cache_k[l], cache_v=cache_v[l],
                                   sinks=attn_sinks[l], rel_bias=rel_bias)
        y_s, k2, v2, c2, f2 = trunk_layer(y_s, state_conv[l], state_ffn_conv[l], attn_s, params)
        kp.append(k1); vp.append(v1); cp.append(c1); fp.append(f1)
        ksl.append(k2); vsl.append(v2); csl.append(c2); fsl.append(f2)
    k_prompt = jnp.stack(kp)
    v_prompt = jnp.stack(vp)
    conv_prompt = jnp.stack(cp)
    ffn_conv_prompt = jnp.stack(fp)
    k_sample = jnp.stack(ksl)
    v_sample = jnp.stack(vsl)
    conv_sample = jnp.stack(csl)
    ffn_conv_sample = jnp.stack(fsl)
    return (y_p, y_s, k_prompt, v_prompt, conv_prompt, ffn_conv_prompt, k_sample, v_sample, conv_sample, ffn_conv_sample)
```

```python
import functools
import math

import jax
import jax.numpy as jnp
from jax import lax
from jax.experimental import pallas as pl
from jax.experimental.pallas import tpu as pltpu

EPS = 1e-6
WINDOW = 128
N_BUCKETS = 32
MAX_DISTANCE = 128
MASKED = -1e30
VMEM_LIMIT_BYTES = 56 * 1024 * 1024
HALO = 32

_bf16 = jnp.bfloat16
_f32 = jnp.float32


def _params(*sem):
    return pltpu.CompilerParams(dimension_semantics=sem, vmem_limit_bytes=VMEM_LIMIT_BYTES)


def _dot(a, b):
    return jnp.dot(a, b, preferred_element_type=_f32)


def _sigmoid(x):
    return 1.0 / (1.0 + jnp.exp(-x))


def _rms(x, g):
    return x * lax.rsqrt(jnp.mean(x * x, axis=-1, keepdims=True) + EPS) * g


def _proj_kernel(x_ref, g_ref, w_ref, b_ref, o_ref, xn_ref, *, act):
    @pl.when(pl.program_id(1) == 0)
    def _():
        xn_ref[...] = _rms(x_ref[...], g_ref[...]).astype(_bf16)

    u = _dot(xn_ref[...], w_ref[...]) + b_ref[...]
    o_ref[...] = _sigmoid(u) if act == "sigmoid" else u


def _proj_glu_kernel(x_ref, g_ref, wa_ref, wb_ref, ba_ref, bb_ref, o_ref, xn_ref):
    @pl.when(pl.program_id(1) == 0)
    def _():
        xn_ref[...] = _rms(x_ref[...], g_ref[...]).astype(_bf16)

    xn = xn_ref[...]
    a = _dot(xn, wa_ref[...]) + ba_ref[...]
    b = _dot(xn, wb_ref[...]) + bb_ref[...]
    o_ref[...] = a * _sigmoid(b)


def _proj(x, g, w, b, col0, ncols, act, tm=512, tn=512):
    m, d = x.shape
    off = col0 // tn
    return pl.pallas_call(
        functools.partial(_proj_kernel, act=act),
        grid=(m // tm, ncols // tn),
        in_specs=[
            pl.BlockSpec((tm, d), lambda i, j: (i, 0)),
            pl.BlockSpec((1, d), lambda i, j: (0, 0)),
            pl.BlockSpec((d, tn), lambda i, j: (0, j + off)),
            pl.BlockSpec((1, tn), lambda i, j: (0, j + off)),
        ],
        out_specs=pl.BlockSpec((tm, tn), lambda i, j: (i, j)),
        out_shape=jax.ShapeDtypeStruct((m, ncols), _f32),
        scratch_shapes=[pltpu.VMEM((tm, d), _bf16)],
        compiler_params=_params("arbitrary", "arbitrary"),
        name="proj_" + act,
    )(x, g, w, b)


def _proj_glu(x, g, w, b, ncols, tm=512, tn=512):
    m, d = x.shape
    off = ncols // tn
    return pl.pallas_call(
        _proj_glu_kernel,
        grid=(m // tm, ncols // tn),
        in_specs=[
            pl.BlockSpec((tm, d), lambda i, j: (i, 0)),
            pl.BlockSpec((1, d), lambda i, j: (0, 0)),
            pl.BlockSpec((d, tn), lambda i, j: (0, j)),
            pl.BlockSpec((d, tn), lambda i, j: (0, j + off)),
            pl.BlockSpec((1, tn), lambda i, j: (0, j)),
            pl.BlockSpec((1, tn), lambda i, j: (0, j + off)),
        ],
        out_specs=pl.BlockSpec((tm, tn), lambda i, j: (i, j)),
        out_shape=jax.ShapeDtypeStruct((m, ncols), _f32),
        scratch_shapes=[pltpu.VMEM((tm, d), _bf16)],
        compiler_params=_params("arbitrary", "arbitrary"),
        name="proj_glu",
    )(x, g, w, w, b, b)


def _ln_silu_proj(conv_ref, act_ref, lg_ref, lb_ref, w_ref, o_ref, rows):
    rt = 16

    def body(r, carry):
        r0 = pl.multiple_of(r * rt, rt)
        c = conv_ref[pl.ds(r0, rt), :]
        mu = jnp.mean(c, axis=-1, keepdims=True)
        cc = c - mu
        var = jnp.mean(cc * cc, axis=-1, keepdims=True)
        y = cc * lax.rsqrt(var + EPS) * lg_ref[...] + lb_ref[...]
        act_ref[pl.ds(r0, rt), :] = (y * _sigmoid(y)).astype(_bf16)
        return carry

    lax.fori_loop(0, rows // rt, body, 0)
    o_ref[...] = _dot(act_ref[...], w_ref[...])


def _conv_prompt_kernel(glu_ref, k_ref, cb_ref, lg_ref, lb_ref, w_ref, o_ref, ext_ref, conv_ref, act_ref, *, tm, width):
    d = glu_ref.shape[1]
    i = pl.program_id(1)

    @pl.when(i == 0)
    def _():
        ext_ref[0:HALO, :] = jnp.zeros((HALO, d), _f32)

    @pl.when(i > 0)
    def _():
        ext_ref[0:HALO, :] = ext_ref[tm:tm + HALO, :]

    ext_ref[HALO:HALO + tm, :] = glu_ref[...]

    rt, ct = 32, 256
    first = HALO - (width - 1)

    def body(r, carry):
        r0 = pl.multiple_of(r * rt, rt)
        for c in range(d // ct):
            cs = slice(c * ct, (c + 1) * ct)
            blk = ext_ref[pl.ds(r0, rt + HALO), cs]
            acc = jnp.broadcast_to(cb_ref[:, cs], (rt, ct))
            for w in range(width):
                acc = acc + k_ref[w:w + 1, cs] * blk[first + w:first + w + rt, :]
            conv_ref[pl.ds(r0, rt), cs] = acc
        return carry

    lax.fori_loop(0, tm // rt, body, 0)
    _ln_silu_proj(conv_ref, act_ref, lg_ref, lb_ref, w_ref, o_ref, tm)


def _conv_prompt(glu, conv_k, conv_b, ln_g, ln_b, w, nseq, tm=256):
    m, d = glu.shape
    nt = m // nseq // tm
    width = conv_k.shape[0]
    const = lambda b, i: (0, 0)
    return pl.pallas_call(
        functools.partial(_conv_prompt_kernel, tm=tm, width=width),
        grid=(nseq, nt),
        in_specs=[
            pl.BlockSpec((tm, d), lambda b, i: (b * nt + i, 0)),
            pl.BlockSpec((width, d), const),
            pl.BlockSpec((1, d), const),
            pl.BlockSpec((1, d), const),
            pl.BlockSpec((1, d), const),
            pl.BlockSpec((d, d), const),
        ],
        out_specs=pl.BlockSpec((tm, d), lambda b, i: (b * nt + i, 0)),
        out_shape=jax.ShapeDtypeStruct((m, d), _f32),
        scratch_shapes=[pltpu.VMEM((tm + HALO, d), _f32), pltpu.VMEM((tm, d), _f32), pltpu.VMEM((tm, d), _bf16)],
        compiler_params=_params("arbitrary", "arbitrary"),
        name="conv_prompt",
    )(glu, conv_k, conv_b, ln_g, ln_b, w)


def _conv_sample_kernel(ext_ref, k_ref, cb_ref, lg_ref, lb_ref, w_ref, o_ref, conv_ref, act_ref, *, nb, t, width):
    d = ext_ref.shape[2]
    ct = 512

    def body(n, carry):
        r0 = pl.multiple_of(n * t, t)
        for c in range(d // ct):
            cs = slice(c * ct, (c + 1) * ct)
            blk = ext_ref[n, :, cs]
            acc = jnp.broadcast_to(cb_ref[:, cs], (t, ct))
            for w in range(width):
                acc = acc + k_ref[w:w + 1, cs] * blk[w:w + t, :]
            conv_ref[pl.ds(r0, t), cs] = acc
        return carry

    lax.fori_loop(0, nb, body, 0)
    _ln_silu_proj(conv_ref, act_ref, lg_ref, lb_ref, w_ref, o_ref, nb * t)


def _conv_sample(glu_ext, conv_k, conv_b, ln_g, ln_b, w, nb=32):
    n, rows, d = glu_ext.shape
    width = conv_k.shape[0]
    t = rows - (width - 1)
    const = lambda i: (0, 0)
    return pl.pallas_call(
        functools.partial(_conv_sample_kernel, nb=nb, t=t, width=width),
        grid=(n // nb,),
        in_specs=[
            pl.BlockSpec((nb, rows, d), lambda i: (i, 0, 0)),
            pl.BlockSpec((width, d), const),
            pl.BlockSpec((1, d), const),
            pl.BlockSpec((1, d), const),
            pl.BlockSpec((1, d), const),
            pl.BlockSpec((d, d), const),
        ],
        out_specs=pl.BlockSpec((nb * t, d), lambda i: (i, 0)),
        out_shape=jax.ShapeDtypeStruct((n * t, d), _f32),
        scratch_shapes=[pltpu.VMEM((nb * t, d), _f32), pltpu.VMEM((nb * t, d), _bf16)],
        compiler_params=_params("arbitrary"),
        name="conv_sample",
    )(glu_ext, conv_k, conv_b, ln_g, ln_b, w)


def _softmax_pv(s, sink, v):
    m = jnp.maximum(jnp.max(s, axis=-1, keepdims=True), sink)
    p = jnp.exp(s - m)
    denom = jnp.sum(p, axis=-1, keepdims=True) + jnp.exp(sink - m)
    return _dot(p.astype(_bf16), v) / denom


def _attn_prompt_kernel(sink_ref, q_ref, kc_ref, kp_ref, vc_ref, vp_ref, bias_ref, o_ref, *, n_heads, kv_heads, hd):
    group = n_heads // kv_heads
    scale = hd ** -0.5
    nt = (((1,), (1,)), ((), ()))
    for h in range(kv_heads):
        ls = slice(h * hd, (h + 1) * hd)
        k = jnp.concatenate([kp_ref[:, ls], kc_ref[:, ls]], axis=0).astype(_bf16)
        v = jnp.concatenate([vp_ref[:, ls], vc_ref[:, ls]], axis=0).astype(_bf16)
        for g in range(group):
            head = h * group + g
            hs = slice(head * hd, (head + 1) * hd)
            q = q_ref[:, hs].astype(_bf16)
            s = lax.dot_general(q, k, nt, preferred_element_type=_f32) * scale + bias_ref[0, head]
            o_ref[:, hs] = _softmax_pv(s, sink_ref[head], v)


def _attn_prompt(qkv, bias, sinks, nseq, n_heads, kv_heads, hd):
    m = qkv.shape[0]
    dq, dkv = n_heads * hd, kv_heads * hd
    nb = m // nseq // WINDOW
    kcol, vcol = dq // dkv, dq // dkv + 1
    cur = lambda b, i: b * nb + i
    prev = lambda b, i: b * nb + jnp.maximum(i - 1, 0)
    return pl.pallas_call(
        functools.partial(_attn_prompt_kernel, n_heads=n_heads, kv_heads=kv_heads, hd=hd),
        grid=(nseq, nb),
        in_specs=[
            pl.BlockSpec(memory_space=pltpu.SMEM),
            pl.BlockSpec((WINDOW, dq), lambda b, i: (cur(b, i), 0)),
            pl.BlockSpec((WINDOW, dkv), lambda b, i: (cur(b, i), kcol)),
            pl.BlockSpec((WINDOW, dkv), lambda b, i: (prev(b, i), kcol)),
            pl.BlockSpec((WINDOW, dkv), lambda b, i: (cur(b, i), vcol)),
            pl.BlockSpec((WINDOW, dkv), lambda b, i: (prev(b, i), vcol)),
            pl.BlockSpec((1, n_heads, WINDOW, 2 * WINDOW), lambda b, i: (jnp.minimum(i, 1), 0, 0, 0)),
        ],
        out_specs=pl.BlockSpec((WINDOW, dq), lambda b, i: (cur(b, i), 0)),
        out_shape=jax.ShapeDtypeStruct((m, dq), _f32),
        compiler_params=_params("arbitrary", "arbitrary"),
        name="attn_prompt",
    )(sinks, qkv, qkv, qkv, qkv, qkv, bias)


def _attn_sample_kernel(q_ref, k_ref, v_ref, bias_ref, sink_ref, o_ref, *, nb, kv_heads, hd):
    scale = hd ** -0.5
    nt = (((1,), (1,)), ((), ()))

    def body(n, carry):
        for h in range(kv_heads):
            ls = slice(h * hd, (h + 1) * hd)
            q = q_ref[n, h].astype(_bf16)
            k = k_ref[n, :, ls].astype(_bf16)
            v = v_ref[n, :, ls].astype(_bf16)
            s = lax.dot_general(q, k, nt, preferred_element_type=_f32) * scale + bias_ref[h]
            o_ref[n, h] = _softmax_pv(s, sink_ref[h], v)
        return carry

    lax.fori_loop(0, nb, body, 0)


def _attn_sample(q, k_all, v_all, bias, sink_rows, nb=8):
    n, kv_heads, rows, hd = q.shape
    keys, dkv = k_all.shape[1:]
    return pl.pallas_call(
        functools.partial(_attn_sample_kernel, nb=nb, kv_heads=kv_heads, hd=hd),
        grid=(n // nb,),
        in_specs=[
            pl.BlockSpec((nb, kv_heads, rows, hd), lambda i: (i, 0, 0, 0)),
            pl.BlockSpec((nb, keys, dkv), lambda i: (i, 0, 0)),
            pl.BlockSpec((nb, keys, dkv), lambda i: (i, 0, 0)),
            pl.BlockSpec((kv_heads, rows, keys), lambda i: (0, 0, 0)),
            pl.BlockSpec((kv_heads, rows, 1), lambda i: (0, 0, 0)),
        ],
        out_specs=pl.BlockSpec((nb, kv_heads, rows, hd), lambda i: (i, 0, 0, 0)),
        out_shape=jax.ShapeDtypeStruct(q.shape, _f32),
        compiler_params=_params("arbitrary"),
        name="attn_sample",
    )(q, k_all, v_all, bias, sink_rows)


def _mix_kernel(attn_ref, conv_ref, gc_ref, ga_ref, x_ref, wa_ref, wo_ref, g_ref, h_ref):
    ao = _dot(attn_ref[...].astype(_bf16), wa_ref[...])
    mixed = gc_ref[...] * conv_ref[...] + ga_ref[...] * ao
    o = _dot(mixed.astype(_bf16), wo_ref[...])
    h_ref[...] = x_ref[...] + _rms(o, g_ref[...])


def _mix(attn, conv_out, gates, x, w_attn, w_out, g_post, tm=256):
    m, d = x.shape
    row = lambda i: (i, 0)
    const = lambda i: (0, 0)
    once = pl.Buffered(1)
    return pl.pallas_call(
        _mix_kernel,
        grid=(m // tm,),
        in_specs=[
            pl.BlockSpec((tm, d), row),
            pl.BlockSpec((tm, d), row),
            pl.BlockSpec((tm, d), lambda i: (i, 0)),
            pl.BlockSpec((tm, d), lambda i: (i, 1)),
            pl.BlockSpec((tm, d), row),
            pl.BlockSpec((d, d), const, pipeline_mode=once),
            pl.BlockSpec((d, d), const, pipeline_mode=once),
            pl.BlockSpec((1, d), const),
        ],
        out_specs=pl.BlockSpec((tm, d), row),
        out_shape=jax.ShapeDtypeStruct((m, d), _f32),
        compiler_params=_params("arbitrary"),
        name="mix",
    )(attn, conv_out, gates, gates, x, w_attn, w_out, g_post)


def _gelu_tanh(x):
    return 0.5 * x * (1.0 + jnp.tanh(math.sqrt(2.0 / math.pi) * (x + 0.044715 * (x * x * x))))


def _ffn_finish(c, nc, acc_ref, h_ref, gp_ref, y_ref):
    @pl.when(c == nc - 1)
    def _():
        y_ref[...] = h_ref[...] + _rms(acc_ref[...], gp_ref[...])


def _ffn_prompt_kernel(h_ref, g_ref, wg_ref, wv_ref, kg_ref, kv_ref, bg_ref, bv_ref, wd_ref, gp_ref,
                       y_ref, tg_ref, tv_ref, hn_ref, acc_ref, ug_ref, uv_ref, cg_ref, cv_ref, *, tm, nc):
    i, c = pl.program_id(1), pl.program_id(2)
    tc = wg_ref.shape[1]

    @pl.when(c == 0)
    def _():
        hn_ref[...] = _rms(h_ref[...], g_ref[...]).astype(_bf16)
        acc_ref[...] = jnp.zeros_like(acc_ref)

    def conv(u_ref, carry_ref, w_ref, k_ref, b_ref, tail_ref):
        @pl.when(i == 0)
        def _():
            u_ref[0:8, :] = jnp.zeros((8, tc), _f32)

        @pl.when(i > 0)
        def _():
            u_ref[0:8, :] = carry_ref[c]

        u_ref[8:8 + tm, :] = _dot(hn_ref[...], w_ref[...])
        carry_ref[c] = u_ref[tm:tm + 8, :]
        tail_ref[0, 0] = u_ref[tm + 6:tm + 8, :]
        return (k_ref[2:3, :] * u_ref[8:8 + tm, :] + k_ref[1:2, :] * u_ref[7:7 + tm, :]
                + k_ref[0:1, :] * u_ref[6:6 + tm, :] + b_ref[...])

    gate = conv(ug_ref, cg_ref, wg_ref, kg_ref, bg_ref, tg_ref)
    val = conv(uv_ref, cv_ref, wv_ref, kv_ref, bv_ref, tv_ref)
    acc_ref[...] += _dot((_gelu_tanh(gate) * val).astype(_bf16), wd_ref[...])
    _ffn_finish(c, nc, acc_ref, h_ref, gp_ref, y_ref)


def _ffn_prompt(h, g_pre, w_up, ffn_k, ffn_b, w_down, g_post, nseq, tm=512, tc=512):
    m, d = h.shape
    dff = w_down.shape[0]
    nt, nc = m // nseq // tm, dff // tc
    width = ffn_k.shape[0]
    assert width == 3
    row = lambda b, i, c: (b * nt + i, 0)
    const = lambda b, i, c: (0, 0)
    lo = lambda b, i, c: (0, c)
    hi = lambda b, i, c: (0, c + nc)
    y, tg, tv = pl.pallas_call(
        functools.partial(_ffn_prompt_kernel, tm=tm, nc=nc),
        grid=(nseq, nt, nc),
        in_specs=[
            pl.BlockSpec((tm, d), row),
            pl.BlockSpec((1, d), const),
            pl.BlockSpec((d, tc), lo),
            pl.BlockSpec((d, tc), hi),
            pl.BlockSpec((width, tc), lo),
            pl.BlockSpec((width, tc), hi),
            pl.BlockSpec((1, tc), lo),
            pl.BlockSpec((1, tc), hi),
            pl.BlockSpec((tc, d), lambda b, i, c: (c, 0)),
            pl.BlockSpec((1, d), const),
        ],
        out_specs=[
            pl.BlockSpec((tm, d), row),
            pl.BlockSpec((1, 1, width - 1, tc), lambda b, i, c: (b, i, 0, c)),
            pl.BlockSpec((1, 1, width - 1, tc), lambda b, i, c: (b, i, 0, c)),
        ],
        out_shape=[
            jax.ShapeDtypeStruct((m, d), _f32),
            jax.ShapeDtypeStruct((nseq, nt, width - 1, dff), _f32),
            jax.ShapeDtypeStruct((nseq, nt, width - 1, dff), _f32),
        ],
        scratch_shapes=[
            pltpu.VMEM((tm, d), _bf16),
            pltpu.VMEM((tm, d), _f32),
            pltpu.VMEM((tm + 8, tc), _f32),
            pltpu.VMEM((tm + 8, tc), _f32),
            pltpu.VMEM((nc, 8, tc), _f32),
            pltpu.VMEM((nc, 8, tc), _f32),
        ],
        compiler_params=_params("arbitrary", "arbitrary", "arbitrary"),
        name="ffn_prompt",
    )(h, g_pre, w_up, w_up, ffn_k, ffn_k, ffn_b, ffn_b, w_down, g_post)
    return y, jnp.concatenate([tg[:, -1], tv[:, -1]], axis=-1)


def _ffn_sample_kernel(h_ref, g_ref, wg_ref, wv_ref, kg_ref, kv_ref, bg_ref, bv_ref, sg_ref, sv_ref, wd_ref, gp_ref,
                       y_ref, tg_ref, tv_ref, hn_ref, acc_ref, ug_ref, uv_ref, *, nb, t, nc):
    c = pl.program_id(1)
    tc = wg_ref.shape[1]

    @pl.when(c == 0)
    def _():
        hn_ref[...] = _rms(h_ref[...], g_ref[...]).astype(_bf16)
        acc_ref[...] = jnp.zeros_like(acc_ref)

    def conv(u_ref, s_ref, w_ref, k_ref, b_ref, tail_ref):
        u_ref[:, 6:8, :] = s_ref[...]
        u_ref[:, 8:8 + t, :] = _dot(hn_ref[...], w_ref[...]).reshape(nb, t, tc)
        tail_ref[...] = u_ref[:, 6 + t:8 + t, :]
        cv = (k_ref[2:3, :] * u_ref[:, 8:8 + t, :] + k_ref[1:2, :] * u_ref[:, 7:7 + t, :]
              + k_ref[0:1, :] * u_ref[:, 6:6 + t, :] + b_ref[...])
        return cv.reshape(nb * t, tc)

    gate = conv(ug_ref, sg_ref, wg_ref, kg_ref, bg_ref, tg_ref)
    val = conv(uv_ref, sv_ref, wv_ref, kv_ref, bv_ref, tv_ref)
    acc_ref[...] += _dot((_gelu_tanh(gate) * val).astype(_bf16), wd_ref[...])
    _ffn_finish(c, nc, acc_ref, h_ref, gp_ref, y_ref)


def _ffn_sample(h, state, g_pre, w_up, ffn_k, ffn_b, w_down, g_post, t, nb=64, tc=512):
    m, d = h.shape
    n = m // t
    dff = w_down.shape[0]
    nc = dff // tc
    width = ffn_k.shape[0]
    assert width == 3 and t >= width - 1
    tm = nb * t
    row = lambda i, c: (i, 0)
    const = lambda i, c: (0, 0)
    lo = lambda i, c: (0, c)
    hi = lambda i, c: (0, c + nc)
    y, tg, tv = pl.pallas_call(
        functools.partial(_ffn_sample_kernel, nb=nb, t=t, nc=nc),
        grid=(n // nb, nc),
        in_specs=[
            pl.BlockSpec((tm, d), row),
            pl.BlockSpec((1, d), const),
            pl.BlockSpec((d, tc), lo),
            pl.BlockSpec((d, tc), hi),
            pl.BlockSpec((width, tc), lo),
            pl.BlockSpec((width, tc), hi),
            pl.BlockSpec((1, tc), lo),
            pl.BlockSpec((1, tc), hi),
            pl.BlockSpec((nb, width - 1, tc), lambda i, c: (i, 0, c)),
            pl.BlockSpec((nb, width - 1, tc), lambda i, c: (i, 0, c + nc)),
            pl.BlockSpec((tc, d), lambda i, c: (c, 0)),
            pl.BlockSpec((1, d), const),
        ],
        out_specs=[
            pl.BlockSpec((tm, d), row),
            pl.BlockSpec((nb, width - 1, tc), lambda i, c: (i, 0, c)),
            pl.BlockSpec((nb, width - 1, tc), lambda i, c: (i, 0, c)),
        ],
        out_shape=[
            jax.ShapeDtypeStruct((m, d), _f32),
            jax.ShapeDtypeStruct((n, width - 1, dff), _f32),
            jax.ShapeDtypeStruct((n, width - 1, dff), _f32),
        ],
        scratch_shapes=[
            pltpu.VMEM((tm, d), _bf16),
            pltpu.VMEM((tm, d), _f32),
            pltpu.VMEM((nb, 8 + t, tc), _f32),
            pltpu.VMEM((nb, 8 + t, tc), _f32),
        ],
        compiler_params=_params("arbitrary", "arbitrary"),
        name="ffn_sample",
    )(h, g_pre, w_up, w_up, ffn_k, ffn_k, ffn_b, ffn_b, state, state, w_down, g_post)
    return y, jnp.concatenate([tg, tv], axis=-1)


def _rel_bucket(dist):
    max_exact = N_BUCKETS // 2
    dd = jnp.maximum(dist, 1).astype(_f32)
    large = max_exact + (jnp.log(dd / max_exact) / math.log(MAX_DISTANCE / max_exact)
                         * (N_BUCKETS - max_exact)).astype(jnp.int32)
    large = jnp.minimum(large, N_BUCKETS - 1)
    return jnp.where(dist < max_exact, dist, large)


def _bias_table(rel_bias, dist, valid):
    b = rel_bias[_rel_bucket(jnp.maximum(dist, 0))].astype(_f32)
    return jnp.where(valid[None], jnp.transpose(b, (2, 0, 1)), MASKED)


def _row(v):
    return v.reshape(1, -1)


def _layer(xp, xs, cache_k, cache_v, st_conv, st_ffn, p):
    (g_pre, w_in, b_in, conv_k, conv_b, ln_g, ln_b, w_conv, sinks, rel_bias, w_attn, w_out, g_post,
     g_ffn_pre, w_up, ffn_k, ffn_b, w_down, g_ffn_post) = p
    nseq, seq, d = xp.shape
    ns, t, _ = xs.shape
    n_heads = sinks.shape[0]
    keep, kv_heads, hd = cache_k.shape[1:]
    group = n_heads // kv_heads
    dq, dkv = n_heads * hd, kv_heads * hd
    d_conv = conv_k.shape[1]
    assert d_conv == d and dq == d and keep == WINDOW and seq % WINDOW == 0

    w_in_b, w_conv_b, w_attn_b, w_out_b = (w.astype(_bf16) for w in (w_in, w_conv, w_attn, w_out))
    w_up_b, w_down_b = w_up.astype(_bf16), w_down.astype(_bf16)
    g_pre, b_in, conv_b, ln_g, ln_b, g_post, g_ffn_pre, ffn_b, g_ffn_post = map(
        _row, (g_pre, b_in, conv_b, ln_g, ln_b, g_post, g_ffn_pre, ffn_b, g_ffn_post))
    qkv0, gate0 = 2 * d_conv, 2 * d_conv + dq + 2 * dkv

    qi = jnp.arange(WINDOW)[:, None]
    kr = jnp.arange(2 * WINDOW)[None, :]
    dist_p = qi + WINDOW - kr
    vis = (dist_p >= 0) & (dist_p < WINDOW)
    bias_p = jnp.stack([_bias_table(rel_bias, dist_p, vis & (kr >= WINDOW)), _bias_table(rel_bias, dist_p, vis)])
    keys_s = 2 * WINDOW
    js = jnp.arange(keys_s)[None, :]
    dist_s = jnp.arange(t)[:, None] + keep - js
    bias_s = _bias_table(rel_bias, dist_s, (dist_s >= 0) & (dist_s < WINDOW) & (js < keep + t))
    bias_s = bias_s.reshape(kv_heads, group * t, keys_s)
    sink_rows = jnp.repeat(sinks.astype(_f32), t).reshape(kv_heads, group * t, 1)

    outs = []
    for x3, is_prompt in ((xp, True), (xs, False)):
        n = x3.shape[0]
        x = x3.reshape(-1, d)
        glu = _proj_glu(x, g_pre, w_in_b, b_in, d_conv)
        qkv = _proj(x, g_pre, w_in_b, b_in, qkv0, dq + 2 * dkv, "linear", tn=256)
        gates = _proj(x, g_pre, w_in_b, b_in, gate0, 2 * d, "sigmoid")
        k_new = qkv[:, dq:dq + dkv].reshape(n, -1, kv_heads, hd)
        v_new = qkv[:, dq + dkv:].reshape(n, -1, kv_heads, hd)
        if is_prompt:
            conv_out = _conv_prompt(glu, conv_k, conv_b, ln_g, ln_b, w_conv_b, nseq)
            attn = _attn_prompt(qkv, bias_p, sinks.astype(_f32), nseq, n_heads, kv_heads, hd)
            new_conv = glu.reshape(n, seq, d)[:, seq - (conv_k.shape[0] - 1):]
            new_k, new_v = k_new[:, seq - keep:], v_new[:, seq - keep:]
        else:
            glu_ext = jnp.concatenate([st_conv, glu.reshape(n, t, d)], axis=1)
            conv_out = _conv_sample(glu_ext, conv_k, conv_b, ln_g, ln_b, w_conv_b)
            new_conv = glu_ext[:, t:]
            k_all = jnp.concatenate([cache_k, k_new], axis=1)
            v_all = jnp.concatenate([cache_v, v_new], axis=1)
            new_k, new_v = k_all[:, t:], v_all[:, t:]
            pad = ((0, 0), (0, keys_s - keep - t), (0, 0))
            q = qkv[:, :dq].reshape(n, t, kv_heads, group, hd).transpose(0, 2, 3, 1, 4).reshape(n, kv_heads, group * t, hd)
            o = _attn_sample(q, jnp.pad(k_all.reshape(n, keep + t, dkv), pad), jnp.pad(v_all.reshape(n, keep + t, dkv), pad),
                             bias_s, sink_rows)
            attn = o.reshape(n, kv_heads, group, t, hd).transpose(0, 3, 1, 2, 4).reshape(n * t, dq)
        h = _mix(attn, conv_out, gates, x, w_attn_b, w_out_b, g_post)
        if is_prompt:
            y, new_ffn = _ffn_prompt(h, g_ffn_pre, w_up_b, ffn_k, ffn_b, w_down_b, g_ffn_post, nseq)
        else:
            y, new_ffn = _ffn_sample(h, st_ffn, g_ffn_pre, w_up_b, ffn_k, ffn_b, w_down_b, g_ffn_post, t)
        outs.append((y.reshape(x3.shape), new_k, new_v, new_conv, new_ffn))
    return outs


def kernel(x_prompt, x_sample, cache_k, cache_v, state_conv, state_ffn_conv, norm_mix_pre, w_in, b_in, conv_dw_k, conv_dw_b, conv_ln_g, conv_ln_b, w_conv_proj, attn_sinks, rel_bias, w_attn_proj, w_out, norm_mix_post, norm_ffn_pre, w_up, ffn_dw_k, ffn_dw_b, w_down, norm_ffn_post):
    y_p, y_s = x_prompt, x_sample
    per_layer = []
    for l in range(w_in.shape[0]):
        p = (norm_mix_pre[l], w_in[l], b_in[l], conv_dw_k[l], conv_dw_b[l], conv_ln_g[l], conv_ln_b[l], w_conv_proj[l],
             attn_sinks[l], rel_bias, w_attn_proj[l], w_out[l], norm_mix_post[l],
             norm_ffn_pre[l], w_up[l], ffn_dw_k[l], ffn_dw_b[l], w_down[l], norm_ffn_post[l])
        (y_p, *rest_p), (y_s, *rest_s) = _layer(y_p, y_s, cache_k[l], cache_v[l], state_conv[l], state_ffn_conv[l], p)
        per_layer.append(rest_p + rest_s)
    stacked = [jnp.stack(leaves) for leaves in zip(*per_layer)]
    return (y_p, y_s, *stacked)
```

```python
import functools
import math

import jax
import jax.numpy as jnp
from jax import lax
from jax.experimental import pallas as pl
from jax.experimental.pallas import tpu as pltpu

EPS = 1e-6
WINDOW = 128
N_BUCKETS = 32
MAX_DISTANCE = 128
MASKED = -1e30
VMEM_LIMIT_BYTES = 56 * 1024 * 1024
HALO = 32

_bf16 = jnp.bfloat16
_f32 = jnp.float32


def _params(*sem):
    return pltpu.CompilerParams(dimension_semantics=sem, vmem_limit_bytes=VMEM_LIMIT_BYTES)


def _dot(a, b):
    return jnp.dot(a, b, preferred_element_type=_f32)


def _sigmoid(x):
    return 1.0 / (1.0 + jnp.exp(-x))


def _rms(x, g):
    return x * lax.rsqrt(jnp.mean(x * x, axis=-1, keepdims=True) + EPS) * g


def _inproj_kernel(x_ref, g_ref, wa_ref, wb_ref, ba_ref, bb_ref, glu_ref, qkv_ref, gate_ref, xn_ref, *, n_glu, n_qkv):
    j = pl.program_id(1)

    @pl.when(j == 0)
    def _():
        xn_ref[...] = _rms(x_ref[...], g_ref[...]).astype(_bf16)

    half = wb_ref.shape[1] // 2
    halves = [slice(0, half), slice(half, 2 * half)]

    def u(w_ref, b_ref, cs):
        return _dot(xn_ref[...], w_ref[:, cs]) + b_ref[:, cs]

    @pl.when(j < n_glu)
    def _():
        for cs in halves:
            glu_ref[:, cs] = u(wa_ref, ba_ref, cs) * _sigmoid(u(wb_ref, bb_ref, cs))

    @pl.when((j >= n_glu) & (j < n_glu + n_qkv))
    def _():
        for cs in halves:
            qkv_ref[:, cs] = u(wb_ref, bb_ref, cs)

    @pl.when(j >= n_glu + n_qkv)
    def _():
        for cs in halves:
            gate_ref[:, cs] = _sigmoid(u(wb_ref, bb_ref, cs))


def _inproj(x, g, w, b, d_glu, d_qkv, d_gate, tm=512, tn=512):
    m, d = x.shape
    n_glu, n_qkv, n_gate = d_glu // tn, d_qkv // tn, d_gate // tn
    assert w.shape[1] == 2 * d_glu + d_qkv + d_gate
    first = lambda i, j: (0, jnp.minimum(j, n_glu - 1))
    rest = lambda i, j: (0, j + n_glu)
    return pl.pallas_call(
        functools.partial(_inproj_kernel, n_glu=n_glu, n_qkv=n_qkv),
        grid=(m // tm, n_glu + n_qkv + n_gate),
        in_specs=[
            pl.BlockSpec((tm, d), lambda i, j: (i, 0)),
            pl.BlockSpec((1, d), lambda i, j: (0, 0)),
            pl.BlockSpec((d, tn), first),
            pl.BlockSpec((d, tn), rest),
            pl.BlockSpec((1, tn), first),
            pl.BlockSpec((1, tn), rest),
        ],
        out_specs=[
            pl.BlockSpec((tm, tn), lambda i, j: (i, jnp.minimum(j, n_glu - 1))),
            pl.BlockSpec((tm, tn), lambda i, j: (i, jnp.clip(j - n_glu, 0, n_qkv - 1))),
            pl.BlockSpec((tm, tn), lambda i, j: (i, jnp.maximum(j - n_glu - n_qkv, 0))),
        ],
        out_shape=[
            jax.ShapeDtypeStruct((m, d_glu), _f32),
            jax.ShapeDtypeStruct((m, d_qkv), _f32),
            jax.ShapeDtypeStruct((m, d_gate), _f32),
        ],
        scratch_shapes=[pltpu.VMEM((tm, d), _bf16)],
        compiler_params=_params("arbitrary", "arbitrary"),
        name="inproj",
    )(x, g, w, w, b, b)


def _ln_silu_proj(conv_ref, act_ref, lg_ref, lb_ref, w_ref, o_ref, rows):
    rt = 16

    def body(r, carry):
        r0 = pl.multiple_of(r * rt, rt)
        c = conv_ref[pl.ds(r0, rt), :]
        mu = jnp.mean(c, axis=-1, keepdims=True)
        cc = c - mu
        var = jnp.mean(cc * cc, axis=-1, keepdims=True)
        y = cc * lax.rsqrt(var + EPS) * lg_ref[...] + lb_ref[...]
        act_ref[pl.ds(r0, rt), :] = (y * _sigmoid(y)).astype(_bf16)
        return carry

    lax.fori_loop(0, rows // rt, body, 0, unroll=4)
    o_ref[...] = _dot(act_ref[...], w_ref[...])


def _conv_prompt_kernel(glu_ref, k_ref, cb_ref, lg_ref, lb_ref, w_ref, o_ref, ext_ref, conv_ref, act_ref, *, tm, width):
    d = glu_ref.shape[1]
    i = pl.program_id(1)

    @pl.when(i == 0)
    def _():
        ext_ref[0:HALO, :] = jnp.zeros((HALO, d), _f32)

    @pl.when(i > 0)
    def _():
        ext_ref[0:HALO, :] = ext_ref[tm:tm + HALO, :]

    ext_ref[HALO:HALO + tm, :] = glu_ref[...]

    rt, ct, sub = 64, 128, 8
    first = HALO - (width - 1)

    def body(r, carry):
        r0 = pl.multiple_of(r * rt, rt)
        for c in range(d // ct):
            cs = slice(c * ct, (c + 1) * ct)
            blk = ext_ref[pl.ds(r0, rt + HALO), cs]
            acc = jnp.broadcast_to(cb_ref[:, cs], (rt, ct))
            for phase in range(sub):
                taps = [w for w in range(width) if (first + w) % sub == phase]
                rows = rt if phase == 0 else rt + sub
                part = None
                for w in taps:
                    lo = first + w - phase
                    term = k_ref[w:w + 1, cs] * blk[lo:lo + rows, :]
                    part = term if part is None else part + term
                acc = acc + part[phase:phase + rt, :]
            conv_ref[pl.ds(r0, rt), cs] = acc
        return carry

    lax.fori_loop(0, tm // rt, body, 0)
    _ln_silu_proj(conv_ref, act_ref, lg_ref, lb_ref, w_ref, o_ref, tm)


def _conv_prompt(glu, conv_k, conv_b, ln_g, ln_b, w, nseq, tm=256):
    m, d = glu.shape
    nt = m // nseq // tm
    width = conv_k.shape[0]
    const = lambda b, i: (0, 0)
    return pl.pallas_call(
        functools.partial(_conv_prompt_kernel, tm=tm, width=width),
        grid=(nseq, nt),
        in_specs=[
            pl.BlockSpec((tm, d), lambda b, i: (b * nt + i, 0)),
            pl.BlockSpec((width, d), const),
            pl.BlockSpec((1, d), const),
            pl.BlockSpec((1, d), const),
            pl.BlockSpec((1, d), const),
            pl.BlockSpec((d, d), const),
        ],
        out_specs=pl.BlockSpec((tm, d), lambda b, i: (b * nt + i, 0)),
        out_shape=jax.ShapeDtypeStruct((m, d), _f32),
        scratch_shapes=[pltpu.VMEM((tm + HALO, d), _f32), pltpu.VMEM((tm, d), _f32), pltpu.VMEM((tm, d), _bf16)],
        compiler_params=_params("arbitrary", "arbitrary"),
        name="conv_prompt",
    )(glu, conv_k, conv_b, ln_g, ln_b, w)


def _conv_sample_kernel(ext_ref, k_ref, cb_ref, lg_ref, lb_ref, w_ref, o_ref, conv_ref, act_ref, *, nb, t, width):
    d = ext_ref.shape[2]
    ct = 512

    def body(n, carry):
        r0 = pl.multiple_of(n * t, t)
        for c in range(d // ct):
            cs = slice(c * ct, (c + 1) * ct)
            blk = ext_ref[n, :, cs]
            acc = jnp.broadcast_to(cb_ref[:, cs], (t, ct))
            for w in range(width):
                acc = acc + k_ref[w:w + 1, cs] * blk[w:w + t, :]
            conv_ref[pl.ds(r0, t), cs] = acc
        return carry

    lax.fori_loop(0, nb, body, 0)
    _ln_silu_proj(conv_ref, act_ref, lg_ref, lb_ref, w_ref, o_ref, nb * t)


def _conv_sample(glu_ext, conv_k, conv_b, ln_g, ln_b, w, nb=32):
    n, rows, d = glu_ext.shape
    width = conv_k.shape[0]
    t = rows - (width - 1)
    const = lambda i: (0, 0)
    return pl.pallas_call(
        functools.partial(_conv_sample_kernel, nb=nb, t=t, width=width),
        grid=(n // nb,),
        in_specs=[
            pl.BlockSpec((nb, rows, d), lambda i: (i, 0, 0)),
            pl.BlockSpec((width, d), const),
            pl.BlockSpec((1, d), const),
            pl.BlockSpec((1, d), const),
            pl.BlockSpec((1, d), const),
            pl.BlockSpec((d, d), const),
        ],
        out_specs=pl.BlockSpec((nb * t, d), lambda i: (i, 0)),
        out_shape=jax.ShapeDtypeStruct((n * t, d), _f32),
        scratch_shapes=[pltpu.VMEM((nb * t, d), _f32), pltpu.VMEM((nb * t, d), _bf16)],
        compiler_params=_params("arbitrary"),
        name="conv_sample",
    )(glu_ext, conv_k, conv_b, ln_g, ln_b, w)


def _softmax_pv(s, sink, v):
    m = jnp.maximum(jnp.max(s, axis=-1, keepdims=True), sink)
    p = jnp.exp(s - m)
    denom = jnp.sum(p, axis=-1, keepdims=True) + jnp.exp(sink - m)
    return _dot(p.astype(_bf16), v) / denom


def _attn_prompt_kernel(sink_ref, q_ref, kc_ref, kp_ref, vc_ref, vp_ref, bias_ref, o_ref, *, n_heads, kv_heads, hd):
    group = n_heads // kv_heads
    scale = hd ** -0.5
    nt = (((1,), (1,)), ((), ()))
    for h in range(kv_heads):
        ls = slice(h * hd, (h + 1) * hd)
        k = jnp.concatenate([kp_ref[:, ls], kc_ref[:, ls]], axis=0).astype(_bf16)
        v = jnp.concatenate([vp_ref[:, ls], vc_ref[:, ls]], axis=0).astype(_bf16)
        for g in range(group):
            head = h * group + g
            hs = slice(head * hd, (head + 1) * hd)
            q = q_ref[:, hs].astype(_bf16)
            s = lax.dot_general(q, k, nt, preferred_element_type=_f32) * scale + bias_ref[0, head]
            o_ref[:, hs] = _softmax_pv(s, sink_ref[head], v)


def _attn_prompt(qkv, bias, sinks, nseq, n_heads, kv_heads, hd):
    m = qkv.shape[0]
    dq, dkv = n_heads * hd, kv_heads * hd
    nb = m // nseq // WINDOW
    kcol, vcol = dq // dkv, dq // dkv + 1
    cur = lambda b, i: b * nb + i
    prev = lambda b, i: b * nb + jnp.maximum(i - 1, 0)
    return pl.pallas_call(
        functools.partial(_attn_prompt_kernel, n_heads=n_heads, kv_heads=kv_heads, hd=hd),
        grid=(nseq, nb),
        in_specs=[
            pl.BlockSpec(memory_space=pltpu.SMEM),
            pl.BlockSpec((WINDOW, dq), lambda b, i: (cur(b, i), 0)),
            pl.BlockSpec((WINDOW, dkv), lambda b, i: (cur(b, i), kcol)),
            pl.BlockSpec((WINDOW, dkv), lambda b, i: (prev(b, i), kcol)),
            pl.BlockSpec((WINDOW, dkv), lambda b, i: (cur(b, i), vcol)),
            pl.BlockSpec((WINDOW, dkv), lambda b, i: (prev(b, i), vcol)),
            pl.BlockSpec((1, n_heads, WINDOW, 2 * WINDOW), lambda b, i: (jnp.minimum(i, 1), 0, 0, 0)),
        ],
        out_specs=pl.BlockSpec((WINDOW, dq), lambda b, i: (cur(b, i), 0)),
        out_shape=jax.ShapeDtypeStruct((m, dq), _f32),
        compiler_params=_params("arbitrary", "arbitrary"),
        name="attn_prompt",
    )(sinks, qkv, qkv, qkv, qkv, qkv, bias)


def _attn_sample_kernel(q_ref, k_ref, v_ref, bias_ref, sink_ref, o_ref, *, nb, kv_heads, hd):
    scale = hd ** -0.5
    nt = (((1,), (1,)), ((), ()))

    def body(n, carry):
        for h in range(kv_heads):
            ls = slice(h * hd, (h + 1) * hd)
            q = q_ref[n, h].astype(_bf16)
            k = k_ref[n, :, ls].astype(_bf16)
            v = v_ref[n, :, ls].astype(_bf16)
            s = lax.dot_general(q, k, nt, preferred_element_type=_f32) * scale + bias_ref[h]
            o_ref[n, h] = _softmax_pv(s, sink_ref[h], v)
        return carry

    lax.fori_loop(0, nb, body, 0)


def _attn_sample(q, k_all, v_all, bias, sink_rows, nb=8):
    n, kv_heads, rows, hd = q.shape
    keys, dkv = k_all.shape[1:]
    return pl.pallas_call(
        functools.partial(_attn_sample_kernel, nb=nb, kv_heads=kv_heads, hd=hd),
        grid=(n // nb,),
        in_specs=[
            pl.BlockSpec((nb, kv_heads, rows, hd), lambda i: (i, 0, 0, 0)),
            pl.BlockSpec((nb, keys, dkv), lambda i: (i, 0, 0)),
            pl.BlockSpec((nb, keys, dkv), lambda i: (i, 0, 0)),
            pl.BlockSpec((kv_heads, rows, keys), lambda i: (0, 0, 0)),
            pl.BlockSpec((kv_heads, rows, 1), lambda i: (0, 0, 0)),
        ],
        out_specs=pl.BlockSpec((nb, kv_heads, rows, hd), lambda i: (i, 0, 0, 0)),
        out_shape=jax.ShapeDtypeStruct(q.shape, _f32),
        compiler_params=_params("arbitrary"),
        name="attn_sample",
    )(q, k_all, v_all, bias, sink_rows)


def _mix_kernel(attn_ref, conv_ref, gc_ref, ga_ref, x_ref, wa_ref, wo_ref, g_ref, h_ref):
    ao = _dot(attn_ref[...].astype(_bf16), wa_ref[...])
    mixed = gc_ref[...] * conv_ref[...] + ga_ref[...] * ao
    o = _dot(mixed.astype(_bf16), wo_ref[...])
    h_ref[...] = x_ref[...] + _rms(o, g_ref[...])


def _mix(attn, conv_out, gates, x, w_attn, w_out, g_post, tm=256):
    m, d = x.shape
    row = lambda i: (i, 0)
    const = lambda i: (0, 0)
    once = pl.Buffered(1)
    return pl.pallas_call(
        _mix_kernel,
        grid=(m // tm,),
        in_specs=[
            pl.BlockSpec((tm, d), row),
            pl.BlockSpec((tm, d), row),
            pl.BlockSpec((tm, d), lambda i: (i, 0)),
            pl.BlockSpec((tm, d), lambda i: (i, 1)),
            pl.BlockSpec((tm, d), row),
            pl.BlockSpec((d, d), const, pipeline_mode=once),
            pl.BlockSpec((d, d), const, pipeline_mode=once),
            pl.BlockSpec((1, d), const),
        ],
        out_specs=pl.BlockSpec((tm, d), row),
        out_shape=jax.ShapeDtypeStruct((m, d), _f32),
        compiler_params=_params("arbitrary"),
        name="mix",
    )(attn, conv_out, gates, gates, x, w_attn, w_out, g_post)


def _gelu_tanh(x):
    return 0.5 * x * (1.0 + jnp.tanh(math.sqrt(2.0 / math.pi) * (x + 0.044715 * (x * x * x))))


def _ffn_finish(c, nc, acc_ref, h_ref, gp_ref, y_ref):
    @pl.when(c == nc - 1)
    def _():
        y_ref[...] = h_ref[...] + _rms(acc_ref[...], gp_ref[...])


def _ffn_prompt_kernel(h_ref, g_ref, wg_ref, wv_ref, kg_ref, kv_ref, bg_ref, bv_ref, wd_ref, gp_ref,
                       y_ref, tg_ref, tv_ref, hn_ref, acc_ref, ug_ref, uv_ref, cg_ref, cv_ref, *, tm, nc):
    i, c = pl.program_id(1), pl.program_id(2)
    tc = wg_ref.shape[1]

    @pl.when(c == 0)
    def _():
        hn_ref[...] = _rms(h_ref[...], g_ref[...]).astype(_bf16)
        acc_ref[...] = jnp.zeros_like(acc_ref)

    def conv(u_ref, carry_ref, w_ref, k_ref, b_ref, tail_ref):
        @pl.when(i == 0)
        def _():
            u_ref[0:8, :] = jnp.zeros((8, tc), _f32)

        @pl.when(i > 0)
        def _():
            u_ref[0:8, :] = carry_ref[c]

        u_ref[8:8 + tm, :] = _dot(hn_ref[...], w_ref[...])
        carry_ref[c] = u_ref[tm:tm + 8, :]
        tail_ref[0, 0] = u_ref[tm + 6:tm + 8, :]
        return (k_ref[2:3, :] * u_ref[8:8 + tm, :] + k_ref[1:2, :] * u_ref[7:7 + tm, :]
                + k_ref[0:1, :] * u_ref[6:6 + tm, :] + b_ref[...])

    gate = conv(ug_ref, cg_ref, wg_ref, kg_ref, bg_ref, tg_ref)
    val = conv(uv_ref, cv_ref, wv_ref, kv_ref, bv_ref, tv_ref)
    acc_ref[...] += _dot((_gelu_tanh(gate) * val).astype(_bf16), wd_ref[...])
    _ffn_finish(c, nc, acc_ref, h_ref, gp_ref, y_ref)


def _ffn_prompt(h, g_pre, w_up, ffn_k, ffn_b, w_down, g_post, nseq, tm=512, tc=512):
    m, d = h.shape
    dff = w_down.shape[0]
    nt, nc = m // nseq // tm, dff // tc
    width = ffn_k.shape[0]
    assert width == 3
    row = lambda b, i, c: (b * nt + i, 0)
    const = lambda b, i, c: (0, 0)
    lo = lambda b, i, c: (0, c)
    hi = lambda b, i, c: (0, c + nc)
    y, tg, tv = pl.pallas_call(
        functools.partial(_ffn_prompt_kernel, tm=tm, nc=nc),
        grid=(nseq, nt, nc),
        in_specs=[
            pl.BlockSpec((tm, d), row),
            pl.BlockSpec((1, d), const),
            pl.BlockSpec((d, tc), lo),
            pl.BlockSpec((d, tc), hi),
            pl.BlockSpec((width, tc), lo),
            pl.BlockSpec((width, tc), hi),
            pl.BlockSpec((1, tc), lo),
            pl.BlockSpec((1, tc), hi),
            pl.BlockSpec((tc, d), lambda b, i, c: (c, 0)),
            pl.BlockSpec((1, d), const),
        ],
        out_specs=[
            pl.BlockSpec((tm, d), row),
            pl.BlockSpec((1, 1, width - 1, tc), lambda b, i, c: (b, i, 0, c)),
            pl.BlockSpec((1, 1, width - 1, tc), lambda b, i, c: (b, i, 0, c)),
        ],
        out_shape=[
            jax.ShapeDtypeStruct((m, d), _f32),
            jax.ShapeDtypeStruct((nseq, nt, width - 1, dff), _f32),
            jax.ShapeDtypeStruct((nseq, nt, width - 1, dff), _f32),
        ],
        scratch_shapes=[
            pltpu.VMEM((tm, d), _bf16),
            pltpu.VMEM((tm, d), _f32),
            pltpu.VMEM((tm + 8, tc), _f32),
            pltpu.VMEM((tm + 8, tc), _f32),
            pltpu.VMEM((nc, 8, tc), _f32),
            pltpu.VMEM((nc, 8, tc), _f32),
        ],
        compiler_params=_params("arbitrary", "arbitrary", "arbitrary"),
        name="ffn_prompt",
    )(h, g_pre, w_up, w_up, ffn_k, ffn_k, ffn_b, ffn_b, w_down, g_post)
    return y, jnp.concatenate([tg[:, -1], tv[:, -1]], axis=-1)


def _ffn_sample_kernel(h_ref, g_ref, wg_ref, wv_ref, kg_ref, kv_ref, bg_ref, bv_ref, sg_ref, sv_ref, wd_ref, gp_ref,
                       y_ref, tg_ref, tv_ref, hn_ref, acc_ref, ug_ref, uv_ref, *, nb, t, nc):
    c = pl.program_id(1)
    tc = wg_ref.shape[1]

    @pl.when(c == 0)
    def _():
        hn_ref[...] = _rms(h_ref[...], g_ref[...]).astype(_bf16)
        acc_ref[...] = jnp.zeros_like(acc_ref)

    def conv(u_ref, s_ref, w_ref, k_ref, b_ref, tail_ref):
        u_ref[:, 6:8, :] = s_ref[...]
        u_ref[:, 8:8 + t, :] = _dot(hn_ref[...], w_ref[...]).reshape(nb, t, tc)
        tail_ref[...] = u_ref[:, 6 + t:8 + t, :]
        cv = (k_ref[2:3, :] * u_ref[:, 8:8 + t, :] + k_ref[1:2, :] * u_ref[:, 7:7 + t, :]
              + k_ref[0:1, :] * u_ref[:, 6:6 + t, :] + b_ref[...])
        return cv.reshape(nb * t, tc)

    gate = conv(ug_ref, sg_ref, wg_ref, kg_ref, bg_ref, tg_ref)
    val = conv(uv_ref, sv_ref, wv_ref, kv_ref, bv_ref, tv_ref)
    acc_ref[...] += _dot((_gelu_tanh(gate) * val).astype(_bf16), wd_ref[...])
    _ffn_finish(c, nc, acc_ref, h_ref, gp_ref, y_ref)


def _ffn_sample(h, state, g_pre, w_up, ffn_k, ffn_b, w_down, g_post, t, nb=64, tc=512):
    m, d = h.shape
    n = m // t
    dff = w_down.shape[0]
    nc = dff // tc
    width = ffn_k.shape[0]
    assert width == 3 and t >= width - 1
    tm = nb * t
    row = lambda i, c: (i, 0)
    const = lambda i, c: (0, 0)
    lo = lambda i, c: (0, c)
    hi = lambda i, c: (0, c + nc)
    y, tg, tv = pl.pallas_call(
        functools.partial(_ffn_sample_kernel, nb=nb, t=t, nc=nc),
        grid=(n // nb, nc),
        in_specs=[
            pl.BlockSpec((tm, d), row),
            pl.BlockSpec((1, d), const),
            pl.BlockSpec((d, tc), lo),
            pl.BlockSpec((d, tc), hi),
            pl.BlockSpec((width, tc), lo),
            pl.BlockSpec((width, tc), hi),
            pl.BlockSpec((1, tc), lo),
            pl.BlockSpec((1, tc), hi),
            pl.BlockSpec((nb, width - 1, tc), lambda i, c: (i, 0, c)),
            pl.BlockSpec((nb, width - 1, tc), lambda i, c: (i, 0, c + nc)),
            pl.BlockSpec((tc, d), lambda i, c: (c, 0)),
            pl.BlockSpec((1, d), const),
        ],
        out_specs=[
            pl.BlockSpec((tm, d), row),
            pl.BlockSpec((nb, width - 1, tc), lambda i, c: (i, 0, c)),
            pl.BlockSpec((nb, width - 1, tc), lambda i, c: (i, 0, c)),
        ],
        out_shape=[
            jax.ShapeDtypeStruct((m, d), _f32),
            jax.ShapeDtypeStruct((n, width - 1, dff), _f32),
            jax.ShapeDtypeStruct((n, width - 1, dff), _f32),
        ],
        scratch_shapes=[
            pltpu.VMEM((tm, d), _bf16),
            pltpu.VMEM((tm, d), _f32),
            pltpu.VMEM((nb, 8 + t, tc), _f32),
            pltpu.VMEM((nb, 8 + t, tc), _f32),
        ],
        compiler_params=_params("arbitrary", "arbitrary"),
        name="ffn_sample",
    )(h, g_pre, w_up, w_up, ffn_k, ffn_k, ffn_b, ffn_b, state, state, w_down, g_post)
    return y, jnp.concatenate([tg, tv], axis=-1)


def _rel_bucket(dist):
    max_exact = N_BUCKETS // 2
    dd = jnp.maximum(dist, 1).astype(_f32)
    large = max_exact + (jnp.log(dd / max_exact) / math.log(MAX_DISTANCE / max_exact)
                         * (N_BUCKETS - max_exact)).astype(jnp.int32)
    large = jnp.minimum(large, N_BUCKETS - 1)
    return jnp.where(dist < max_exact, dist, large)


def _bias_table(rel_bias):
    n = WINDOW
    by_dist = rel_bias[_rel_bucket(jnp.arange(n))].astype(_f32).T
    g = jnp.concatenate([jnp.full_like(by_dist, MASKED), by_dist[:, ::-1],
                         jnp.full_like(by_dist, MASKED)], axis=1)
    heads, length = g.shape
    skew = jnp.broadcast_to(g[:, None, :], (heads, n, length)).reshape(heads, n * length)
    skew = skew[:, :n * (length - 1)].reshape(heads, n, length - 1)
    return skew[:, :, n - 1:3 * n - 1]


def _row(v):
    return v.reshape(1, -1)


def _layer(xp, xs, cache_k, cache_v, st_conv, st_ffn, p):
    (g_pre, w_in, b_in, conv_k, conv_b, ln_g, ln_b, w_conv, sinks, rel_bias, w_attn, w_out, g_post,
     g_ffn_pre, w_up, ffn_k, ffn_b, w_down, g_ffn_post) = p
    nseq, seq, d = xp.shape
    ns, t, _ = xs.shape
    n_heads = sinks.shape[0]
    keep, kv_heads, hd = cache_k.shape[1:]
    group = n_heads // kv_heads
    dq, dkv = n_heads * hd, kv_heads * hd
    d_conv = conv_k.shape[1]
    assert d_conv == d and dq == d and keep == WINDOW and seq % WINDOW == 0

    w_in_b, w_conv_b, w_attn_b, w_out_b = (w.astype(_bf16) for w in (w_in, w_conv, w_attn, w_out))
    w_up_b, w_down_b = w_up.astype(_bf16), w_down.astype(_bf16)
    g_pre, b_in, conv_b, ln_g, ln_b, g_post, g_ffn_pre, ffn_b, g_ffn_post = map(
        _row, (g_pre, b_in, conv_b, ln_g, ln_b, g_post, g_ffn_pre, ffn_b, g_ffn_post))

    bias = _bias_table(rel_bias)
    bias_p = jnp.stack([jnp.where(jnp.arange(2 * WINDOW) >= WINDOW, bias, MASKED), bias])
    keys_s = 2 * WINDOW
    assert t <= WINDOW
    bias_s = bias[:, :t].reshape(kv_heads, group * t, keys_s)
    sink_rows = jnp.repeat(sinks.astype(_f32), t).reshape(kv_heads, group * t, 1)

    outs = []
    for x3, is_prompt in ((xp, True), (xs, False)):
        n = x3.shape[0]
        x = x3.reshape(-1, d)
        glu, qkv, gates = _inproj(x, g_pre, w_in_b, b_in, d_conv, dq + 2 * dkv, 2 * d)
        k_new = qkv[:, dq:dq + dkv].reshape(n, -1, kv_heads, hd)
        v_new = qkv[:, dq + dkv:].reshape(n, -1, kv_heads, hd)
        if is_prompt:
            conv_out = _conv_prompt(glu, conv_k, conv_b, ln_g, ln_b, w_conv_b, nseq)
            attn = _attn_prompt(qkv, bias_p, sinks.astype(_f32), nseq, n_heads, kv_heads, hd)
            new_conv = glu.reshape(n, seq, d)[:, seq - (conv_k.shape[0] - 1):]
            new_k, new_v = k_new[:, seq - keep:], v_new[:, seq - keep:]
        else:
            glu_ext = jnp.concatenate([st_conv, glu.reshape(n, t, d)], axis=1)
            conv_out = _conv_sample(glu_ext, conv_k, conv_b, ln_g, ln_b, w_conv_b)
            new_conv = glu_ext[:, t:]
            k_all = jnp.concatenate([cache_k, k_new], axis=1)
            v_all = jnp.concatenate([cache_v, v_new], axis=1)
            new_k, new_v = k_all[:, t:], v_all[:, t:]
            pad = ((0, 0), (0, keys_s - keep - t), (0, 0))
            q = qkv[:, :dq].reshape(n, t, kv_heads, group, hd).transpose(0, 2, 3, 1, 4).reshape(n, kv_heads, group * t, hd)
            o = _attn_sample(q, jnp.pad(k_all.reshape(n, keep + t, dkv), pad), jnp.pad(v_all.reshape(n, keep + t, dkv), pad),
                             bias_s, sink_rows)
            attn = o.reshape(n, kv_heads, group, t, hd).transpose(0, 3, 1, 2, 4).reshape(n * t, dq)
        h = _mix(attn, conv_out, gates, x, w_attn_b, w_out_b, g_post)
        if is_prompt:
            y, new_ffn = _ffn_prompt(h, g_ffn_pre, w_up_b, ffn_k, ffn_b, w_down_b, g_ffn_post, nseq)
        else:
            y, new_ffn = _ffn_sample(h, st_ffn, g_ffn_pre, w_up_b, ffn_k, ffn_b, w_down_b, g_ffn_post, t)
        outs.append((y.reshape(x3.shape), new_k, new_v, new_conv, new_ffn))
    return outs


def kernel(x_prompt, x_sample, cache_k, cache_v, state_conv, state_ffn_conv, norm_mix_pre, w_in, b_in, conv_dw_k, conv_dw_b, conv_ln_g, conv_ln_b, w_conv_proj, attn_sinks, rel_bias, w_attn_proj, w_out, norm_mix_post, norm_ffn_pre, w_up, ffn_dw_k, ffn_dw_b, w_down, norm_ffn_post):
    y_p, y_s = x_prompt, x_sample
    per_layer = []
    for l in range(w_in.shape[0]):
        p = (norm_mix_pre[l], w_in[l], b_in[l], conv_dw_k[l], conv_dw_b[l], conv_ln_g[l], conv_ln_b[l], w_conv_proj[l],
             attn_sinks[l], rel_bias, w_attn_proj[l], w_out[l], norm_mix_post[l],
             norm_ffn_pre[l], w_up[l], ffn_dw_k[l], ffn_dw_b[l], w_down[l], norm_ffn_post[l])
        (y_p, *rest_p), (y_s, *rest_s) = _layer(y_p, y_s, cache_k[l], cache_v[l], state_conv[l], state_ffn_conv[l], p)
        per_layer.append(rest_p + rest_s)
    stacked = [jnp.stack(leaves) for leaves in zip(*per_layer)]
    return (y_p, y_s, *stacked)
```

```python
import functools
import math

import jax
import jax.numpy as jnp
from jax import lax
from jax.experimental import pallas as pl
from jax.experimental.pallas import tpu as pltpu

EPS = 1e-6
WINDOW = 128
N_BUCKETS = 32
MAX_DISTANCE = 128
MASKED = -1e30
VMEM_LIMIT_BYTES = 56 * 1024 * 1024
HALO = 32

_bf16 = jnp.bfloat16
_f32 = jnp.float32


def _params(*sem):
    return pltpu.CompilerParams(dimension_semantics=sem, vmem_limit_bytes=VMEM_LIMIT_BYTES)


def _dot(a, b):
    return jnp.dot(a, b, preferred_element_type=_f32)


def _sigmoid(x):
    return 1.0 / (1.0 + jnp.exp(-x))


def _rms(x, g):
    return x * lax.rsqrt(jnp.mean(x * x, axis=-1, keepdims=True) + EPS) * g


def _inproj_kernel(x_ref, g_ref, wa_ref, wb_ref, ba_ref, bb_ref, glu_ref, qkv_ref, gate_ref, xn_ref, *, n_glu, n_qkv):
    j = pl.program_id(1)

    @pl.when(j == 0)
    def _():
        xn_ref[...] = _rms(x_ref[...], g_ref[...]).astype(_bf16)

    half = wb_ref.shape[2] // 2
    halves = [slice(0, half), slice(half, 2 * half)]

    def u(w_ref, b_ref, cs):
        return _dot(xn_ref[...], w_ref[0, :, cs]) + b_ref[:, cs]

    @pl.when(j < n_glu)
    def _():
        for cs in halves:
            glu_ref[:, cs] = u(wa_ref, ba_ref, cs) * _sigmoid(u(wb_ref, bb_ref, cs))

    @pl.when((j >= n_glu) & (j < n_glu + n_qkv))
    def _():
        for cs in halves:
            qkv_ref[:, cs] = u(wb_ref, bb_ref, cs)

    @pl.when(j >= n_glu + n_qkv)
    def _():
        for cs in halves:
            gate_ref[:, cs] = _sigmoid(u(wb_ref, bb_ref, cs))


def _column_tiles(w, tn):
    d, n = w.shape
    return w.reshape(d, n // tn, tn).transpose(1, 0, 2)


def _inproj(x, g, w, b, d_glu, d_qkv, d_gate, tm=1024):
    m, d = x.shape
    tm = min(tm, m)
    tn = w.shape[2]
    n_glu, n_qkv, n_gate = d_glu // tn, d_qkv // tn, d_gate // tn
    assert w.shape[0] == 2 * n_glu + n_qkv + n_gate
    first_w = lambda i, j: (jnp.minimum(j, n_glu - 1), 0, 0)
    rest_w = lambda i, j: (j + n_glu, 0, 0)
    first = lambda i, j: (0, jnp.minimum(j, n_glu - 1))
    rest = lambda i, j: (0, j + n_glu)
    return pl.pallas_call(
        functools.partial(_inproj_kernel, n_glu=n_glu, n_qkv=n_qkv),
        grid=(m // tm, n_glu + n_qkv + n_gate),
        in_specs=[
            pl.BlockSpec((tm, d), lambda i, j: (i, 0)),
            pl.BlockSpec((1, d), lambda i, j: (0, 0)),
            pl.BlockSpec((1, d, tn), first_w),
            pl.BlockSpec((1, d, tn), rest_w),
            pl.BlockSpec((1, tn), first),
            pl.BlockSpec((1, tn), rest),
        ],
        out_specs=[
            pl.BlockSpec((tm, tn), lambda i, j: (i, jnp.minimum(j, n_glu - 1))),
            pl.BlockSpec((tm, tn), lambda i, j: (i, jnp.clip(j - n_glu, 0, n_qkv - 1))),
            pl.BlockSpec((tm, tn), lambda i, j: (i, jnp.maximum(j - n_glu - n_qkv, 0))),
        ],
        out_shape=[
            jax.ShapeDtypeStruct((m, d_glu), _f32),
            jax.ShapeDtypeStruct((m, d_qkv), _f32),
            jax.ShapeDtypeStruct((m, d_gate), _f32),
        ],
        scratch_shapes=[pltpu.VMEM((tm, d), _bf16)],
        compiler_params=_params("arbitrary", "arbitrary"),
        name="inproj",
    )(x, g, w, w, b, b)


def _ln_silu_proj(conv_ref, act_ref, lg_ref, lb_ref, w_ref, o_ref, rows):
    rt = 16

    def body(r, carry):
        r0 = pl.multiple_of(r * rt, rt)
        c = conv_ref[pl.ds(r0, rt), :]
        mu = jnp.mean(c, axis=-1, keepdims=True)
        cc = c - mu
        var = jnp.mean(cc * cc, axis=-1, keepdims=True)
        y = cc * lax.rsqrt(var + EPS) * lg_ref[...] + lb_ref[...]
        act_ref[pl.ds(r0, rt), :] = (y * _sigmoid(y)).astype(_bf16)
        return carry

    lax.fori_loop(0, rows // rt, body, 0, unroll=4)
    o_ref[...] = _dot(act_ref[...], w_ref[...])


def _conv_prompt_kernel(glu_ref, k_ref, cb_ref, lg_ref, lb_ref, w_ref, o_ref, ext_ref, conv_ref, act_ref, *, tm, width):
    d = glu_ref.shape[1]
    i = pl.program_id(1)

    @pl.when(i == 0)
    def _():
        ext_ref[0:HALO, :] = jnp.zeros((HALO, d), _f32)

    @pl.when(i > 0)
    def _():
        ext_ref[0:HALO, :] = ext_ref[tm:tm + HALO, :]

    ext_ref[HALO:HALO + tm, :] = glu_ref[...]

    rt, ct, sub = 64, 128, 8
    first = HALO - (width - 1)

    def body(r, carry):
        r0 = pl.multiple_of(r * rt, rt)
        for c in range(d // ct):
            cs = slice(c * ct, (c + 1) * ct)
            blk = ext_ref[pl.ds(r0, rt + HALO), cs]
            acc = jnp.broadcast_to(cb_ref[:, cs], (rt, ct))
            for phase in range(sub):
                taps = [w for w in range(width) if (first + w) % sub == phase]
                rows = rt if phase == 0 else rt + sub
                part = None
                for w in taps:
                    lo = first + w - phase
                    term = k_ref[w:w + 1, cs] * blk[lo:lo + rows, :]
                    part = term if part is None else part + term
                acc = acc + part[phase:phase + rt, :]
            conv_ref[pl.ds(r0, rt), cs] = acc
        return carry

    lax.fori_loop(0, tm // rt, body, 0)
    _ln_silu_proj(conv_ref, act_ref, lg_ref, lb_ref, w_ref, o_ref, tm)


def _conv_prompt(glu, conv_k, conv_b, ln_g, ln_b, w, nseq, tm=256):
    m, d = glu.shape
    nt = m // nseq // tm
    width = conv_k.shape[0]
    const = lambda b, i: (0, 0)
    return pl.pallas_call(
        functools.partial(_conv_prompt_kernel, tm=tm, width=width),
        grid=(nseq, nt),
        in_specs=[
            pl.BlockSpec((tm, d), lambda b, i: (b * nt + i, 0)),
            pl.BlockSpec((width, d), const),
            pl.BlockSpec((1, d), const),
            pl.BlockSpec((1, d), const),
            pl.BlockSpec((1, d), const),
            pl.BlockSpec((d, d), const),
        ],
        out_specs=pl.BlockSpec((tm, d), lambda b, i: (b * nt + i, 0)),
        out_shape=jax.ShapeDtypeStruct((m, d), _f32),
        scratch_shapes=[pltpu.VMEM((tm + HALO, d), _f32), pltpu.VMEM((tm, d), _f32), pltpu.VMEM((tm, d), _bf16)],
        compiler_params=_params("arbitrary", "arbitrary"),
        name="conv_prompt",
    )(glu, conv_k, conv_b, ln_g, ln_b, w)


def _conv_sample_kernel(ext_ref, k_ref, cb_ref, lg_ref, lb_ref, w_ref, o_ref, conv_ref, act_ref, *, nb, t, width):
    d = ext_ref.shape[2]
    ct = 512

    def body(n, carry):
        r0 = pl.multiple_of(n * t, t)
        for c in range(d // ct):
            cs = slice(c * ct, (c + 1) * ct)
            blk = ext_ref[n, :, cs]
            acc = jnp.broadcast_to(cb_ref[:, cs], (t, ct))
            for w in range(width):
                acc = acc + k_ref[w:w + 1, cs] * blk[w:w + t, :]
            conv_ref[pl.ds(r0, t), cs] = acc
        return carry

    lax.fori_loop(0, nb, body, 0)
    _ln_silu_proj(conv_ref, act_ref, lg_ref, lb_ref, w_ref, o_ref, nb * t)


def _conv_sample(glu_ext, conv_k, conv_b, ln_g, ln_b, w, nb=32):
    n, rows, d = glu_ext.shape
    width = conv_k.shape[0]
    t = rows - (width - 1)
    const = lambda i: (0, 0)
    return pl.pallas_call(
        functools.partial(_conv_sample_kernel, nb=nb, t=t, width=width),
        grid=(n // nb,),
        in_specs=[
            pl.BlockSpec((nb, rows, d), lambda i: (i, 0, 0)),
            pl.BlockSpec((width, d), const),
            pl.BlockSpec((1, d), const),
            pl.BlockSpec((1, d), const),
            pl.BlockSpec((1, d), const),
            pl.BlockSpec((d, d), const),
        ],
        out_specs=pl.BlockSpec((nb * t, d), lambda i: (i, 0)),
        out_shape=jax.ShapeDtypeStruct((n * t, d), _f32),
        scratch_shapes=[pltpu.VMEM((nb * t, d), _f32), pltpu.VMEM((nb * t, d), _bf16)],
        compiler_params=_params("arbitrary"),
        name="conv_sample",
    )(glu_ext, conv_k, conv_b, ln_g, ln_b, w)


def _softmax_pv(s, sink, v):
    m = jnp.maximum(jnp.max(s, axis=-1, keepdims=True), sink)
    p = jnp.exp(s - m)
    denom = jnp.sum(p, axis=-1, keepdims=True) + jnp.exp(sink - m)
    return _dot(p.astype(_bf16), v) / denom


def _attn_prompt_kernel(q_ref, kc_ref, kp_ref, vc_ref, vp_ref, bias_ref, sink_ref, o_ref, *, kv_heads, hd):
    blk = q_ref.shape[0]
    grp = 2 * hd
    pairs = q_ref.shape[1] // (kv_heads * grp)
    rows = pairs * blk
    scale = hd ** -0.5
    nt = (((1,), (1,)), ((), ()))
    low = lax.broadcasted_iota(jnp.int32, (2 * blk, grp), 1) < hd
    qi = lax.broadcasted_iota(jnp.int32, (rows, blk), 0) & (blk - 1)
    cur_visible = lax.broadcasted_iota(jnp.int32, (rows, blk), 1) <= qi

    def halves(p_ref, c_ref, h):
        g0 = (h // 2) * grp
        x = jnp.concatenate([p_ref[:, g0:g0 + grp], c_ref[:, g0:g0 + grp]], axis=0)
        swapped = pltpu.roll(x, hd, axis=1)
        in_low, in_high = (x, swapped) if h % 2 == 0 else (swapped, x)
        return jnp.where(low, in_low, 0.0).astype(_bf16), jnp.where(low, 0.0, in_high).astype(_bf16)

    for h in range(kv_heads):
        ks = halves(kp_ref, kc_ref, h)
        vs = halves(vp_ref, vc_ref, h)
        lanes = [slice((h * pairs + p) * grp, (h * pairs + p + 1) * grp) for p in range(pairs)]
        qs = (jnp.concatenate([q_ref[:, ls] for ls in lanes], axis=0) * scale).astype(_bf16)
        out = None
        for parity in range(2):
            s2 = lax.dot_general(qs, ks[parity], nt, preferred_element_type=_f32)
            s = jnp.where(cur_visible, s2[:, blk:], s2[:, :blk]) + bias_ref[0, h, parity]
            sink = sink_ref[h, parity]
            m = jnp.maximum(jnp.max(s, axis=-1, keepdims=True), sink)
            p = jnp.exp(s - m)
            den = jnp.sum(p, axis=-1, keepdims=True) + jnp.exp(sink - m)
            p2 = jnp.concatenate([jnp.where(cur_visible, 0.0, p), jnp.where(cur_visible, p, 0.0)], axis=1)
            o = _dot(p2.astype(_bf16), vs[parity]) * (1.0 / den)
            out = o if out is None else out + o
        for p, ls in enumerate(lanes):
            o_ref[:, ls] = out[p * blk:(p + 1) * blk, :]


def _attn_prompt(qkv, bias, sinks, nseq, n_heads, kv_heads, hd):
    m = qkv.shape[0]
    dq, dkv = n_heads * hd, kv_heads * hd
    nb = m // nseq // WINDOW
    pairs = n_heads // kv_heads // 2
    rows = pairs * WINDOW
    assert WINDOW & (WINDOW - 1) == 0 and 2 * hd == 128 and kv_heads % 2 == 0
    qi = jnp.arange(WINDOW)[:, None]
    cur_visible = jnp.arange(WINDOW)[None, :] <= qi
    folded = jnp.stack([jnp.where(cur_visible, bias[:, :, WINDOW:], MASKED),
                        jnp.where(cur_visible, bias[:, :, WINDOW:], bias[:, :, :WINDOW])])
    folded = folded.reshape(2, kv_heads, pairs, 2, WINDOW, WINDOW).transpose(0, 1, 3, 2, 4, 5)
    folded = folded.reshape(2, kv_heads, 2, rows, WINDOW)
    sink_rows = jnp.repeat(sinks.reshape(kv_heads, pairs, 2).transpose(0, 2, 1), WINDOW, axis=-1)
    sink_rows = sink_rows.reshape(kv_heads, 2, rows, 1)
    kcol, vcol = dq // dkv, dq // dkv + 1
    cur = lambda b, i: b * nb + i
    prev = lambda b, i: b * nb + jnp.maximum(i - 1, 0)
    return pl.pallas_call(
        functools.partial(_attn_prompt_kernel, kv_heads=kv_heads, hd=hd),
        grid=(nseq, nb),
        in_specs=[
            pl.BlockSpec((WINDOW, dq), lambda b, i: (cur(b, i), 0)),
            pl.BlockSpec((WINDOW, dkv), lambda b, i: (cur(b, i), kcol)),
            pl.BlockSpec((WINDOW, dkv), lambda b, i: (prev(b, i), kcol)),
            pl.BlockSpec((WINDOW, dkv), lambda b, i: (cur(b, i), vcol)),
            pl.BlockSpec((WINDOW, dkv), lambda b, i: (prev(b, i), vcol)),
            pl.BlockSpec((1, kv_heads, 2, rows, WINDOW), lambda b, i: (jnp.minimum(i, 1), 0, 0, 0, 0)),
            pl.BlockSpec((kv_heads, 2, rows, 1), lambda b, i: (0, 0, 0, 0)),
        ],
        out_specs=pl.BlockSpec((WINDOW, dq), lambda b, i: (cur(b, i), 0)),
        out_shape=jax.ShapeDtypeStruct((m, dq), _f32),
        compiler_params=_params("arbitrary", "arbitrary"),
        name="attn_prompt",
    )(qkv, qkv, qkv, qkv, qkv, folded, sink_rows)


def _attn_sample_kernel(q_ref, k_ref, v_ref, bias_ref, sink_ref, o_ref, *, nb, kv_heads, hd):
    scale = hd ** -0.5
    nt = (((1,), (1,)), ((), ()))

    def body(n, carry):
        for h in range(kv_heads):
            ls = slice(h * hd, (h + 1) * hd)
            q = q_ref[n, h].astype(_bf16)
            k = k_ref[n, :, ls].astype(_bf16)
            v = v_ref[n, :, ls].astype(_bf16)
            s = lax.dot_general(q, k, nt, preferred_element_type=_f32) * scale + bias_ref[h]
            o_ref[n, h] = _softmax_pv(s, sink_ref[h], v)
        return carry

    lax.fori_loop(0, nb, body, 0)


def _attn_sample(q, k_all, v_all, bias, sink_rows, nb=8):
    n, kv_heads, rows, hd = q.shape
    keys, dkv = k_all.shape[1:]
    return pl.pallas_call(
        functools.partial(_attn_sample_kernel, nb=nb, kv_heads=kv_heads, hd=hd),
        grid=(n // nb,),
        in_specs=[
            pl.BlockSpec((nb, kv_heads, rows, hd), lambda i: (i, 0, 0, 0)),
            pl.BlockSpec((nb, keys, dkv), lambda i: (i, 0, 0)),
            pl.BlockSpec((nb, keys, dkv), lambda i: (i, 0, 0)),
            pl.BlockSpec((kv_heads, rows, keys), lambda i: (0, 0, 0)),
            pl.BlockSpec((kv_heads, rows, 1), lambda i: (0, 0, 0)),
        ],
        out_specs=pl.BlockSpec((nb, kv_heads, rows, hd), lambda i: (i, 0, 0, 0)),
        out_shape=jax.ShapeDtypeStruct(q.shape, _f32),
        compiler_params=_params("arbitrary"),
        name="attn_sample",
    )(q, k_all, v_all, bias, sink_rows)


def _mix_kernel(attn_ref, conv_ref, gc_ref, ga_ref, x_ref, wa_ref, wo_ref, g_ref, h_ref):
    ao = _dot(attn_ref[...].astype(_bf16), wa_ref[...])
    mixed = gc_ref[...] * conv_ref[...] + ga_ref[...] * ao
    o = _dot(mixed.astype(_bf16), wo_ref[...])
    h_ref[...] = x_ref[...] + _rms(o, g_ref[...])


def _mix(attn, conv_out, gates, x, w_attn, w_out, g_post, tm=256):
    m, d = x.shape
    row = lambda i: (i, 0)
    const = lambda i: (0, 0)
    once = pl.Buffered(1)
    return pl.pallas_call(
        _mix_kernel,
        grid=(m // tm,),
        in_specs=[
            pl.BlockSpec((tm, d), row),
            pl.BlockSpec((tm, d), row),
            pl.BlockSpec((tm, d), lambda i: (i, 0)),
            pl.BlockSpec((tm, d), lambda i: (i, 1)),
            pl.BlockSpec((tm, d), row),
            pl.BlockSpec((d, d), const, pipeline_mode=once),
            pl.BlockSpec((d, d), const, pipeline_mode=once),
            pl.BlockSpec((1, d), const),
        ],
        out_specs=pl.BlockSpec((tm, d), row),
        out_shape=jax.ShapeDtypeStruct((m, d), _f32),
        compiler_params=_params("arbitrary"),
        name="mix",
    )(attn, conv_out, gates, gates, x, w_attn, w_out, g_post)


def _gelu_tanh(x):
    return 0.5 * x * (1.0 + jnp.tanh(math.sqrt(2.0 / math.pi) * (x + 0.044715 * (x * x * x))))


def _ffn_finish(c, nc, acc_ref, h_ref, gp_ref, y_ref):
    @pl.when(c == nc - 1)
    def _():
        y_ref[...] = h_ref[...] + _rms(acc_ref[...], gp_ref[...])


def _ffn_prompt_kernel(h_ref, g_ref, wg_ref, wv_ref, kg_ref, kv_ref, bg_ref, bv_ref, wd_ref, gp_ref,
                       y_ref, tg_ref, tv_ref, hn_ref, acc_ref, act_ref, ug_ref, uv_ref, cg_ref, cv_ref, *, tm, nc):
    i, c = pl.program_id(1), pl.program_id(2)
    tc = wg_ref.shape[1]

    @pl.when(c == 0)
    def _():
        hn_ref[...] = _rms(h_ref[...], g_ref[...]).astype(_bf16)
        acc_ref[...] = jnp.zeros_like(acc_ref)

    @pl.when(i == 0)
    def _():
        ug_ref[0:8, :] = jnp.zeros((8, tc), _f32)
        uv_ref[0:8, :] = jnp.zeros((8, tc), _f32)

    @pl.when(i > 0)
    def _():
        ug_ref[0:8, :] = cg_ref[c]
        uv_ref[0:8, :] = cv_ref[c]

    def conv(u_ref, w_ref, k_ref, b_ref, cs):
        u_ref[8:8 + tm, cs] = _dot(hn_ref[...], w_ref[:, cs])
        return (k_ref[2:3, cs] * u_ref[8:8 + tm, cs] + k_ref[1:2, cs] * u_ref[7:7 + tm, cs]
                + k_ref[0:1, cs] * u_ref[6:6 + tm, cs] + b_ref[:, cs])

    half = tc // 2
    for cs in (slice(0, half), slice(half, tc)):
        gate = conv(ug_ref, wg_ref, kg_ref, bg_ref, cs)
        val = conv(uv_ref, wv_ref, kv_ref, bv_ref, cs)
        act_ref[:, cs] = (_gelu_tanh(gate) * val).astype(_bf16)
    acc_ref[...] += _dot(act_ref[...], wd_ref[...])
    cg_ref[c] = ug_ref[tm:tm + 8, :]
    cv_ref[c] = uv_ref[tm:tm + 8, :]
    tg_ref[0, 0] = ug_ref[tm + 6:tm + 8, :]
    tv_ref[0, 0] = uv_ref[tm + 6:tm + 8, :]
    _ffn_finish(c, nc, acc_ref, h_ref, gp_ref, y_ref)


def _ffn_prompt(h, g_pre, w_up, ffn_k, ffn_b, w_down, g_post, nseq, tm=512, tc=512):
    m, d = h.shape
    dff = w_down.shape[0]
    nt, nc = m // nseq // tm, dff // tc
    width = ffn_k.shape[0]
    assert width == 3
    row = lambda b, i, c: (b * nt + i, 0)
    const = lambda b, i, c: (0, 0)
    lo = lambda b, i, c: (0, c)
    hi = lambda b, i, c: (0, c + nc)
    y, tg, tv = pl.pallas_call(
        functools.partial(_ffn_prompt_kernel, tm=tm, nc=nc),
        grid=(nseq, nt, nc),
        in_specs=[
            pl.BlockSpec((tm, d), row),
            pl.BlockSpec((1, d), const),
            pl.BlockSpec((d, tc), lo),
            pl.BlockSpec((d, tc), hi),
            pl.BlockSpec((width, tc), lo),
            pl.BlockSpec((width, tc), hi),
            pl.BlockSpec((1, tc), lo),
            pl.BlockSpec((1, tc), hi),
            pl.BlockSpec((tc, d), lambda b, i, c: (c, 0)),
            pl.BlockSpec((1, d), const),
        ],
        out_specs=[
            pl.BlockSpec((tm, d), row),
            pl.BlockSpec((1, 1, width - 1, tc), lambda b, i, c: (b, i, 0, c)),
            pl.BlockSpec((1, 1, width - 1, tc), lambda b, i, c: (b, i, 0, c)),
        ],
        out_shape=[
            jax.ShapeDtypeStruct((m, d), _f32),
            jax.ShapeDtypeStruct((nseq, nt, width - 1, dff), _f32),
            jax.ShapeDtypeStruct((nseq, nt, width - 1, dff), _f32),
        ],
        scratch_shapes=[
            pltpu.VMEM((tm, d), _bf16),
            pltpu.VMEM((tm, d), _f32),
            pltpu.VMEM((tm, tc), _bf16),
            pltpu.VMEM((tm + 8, tc), _f32),
            pltpu.VMEM((tm + 8, tc), _f32),
            pltpu.VMEM((nc, 8, tc), _f32),
            pltpu.VMEM((nc, 8, tc), _f32),
        ],
        compiler_params=_params("arbitrary", "arbitrary", "arbitrary"),
        name="ffn_prompt",
    )(h, g_pre, w_up, w_up, ffn_k, ffn_k, ffn_b, ffn_b, w_down, g_post)
    return y, jnp.concatenate([tg[:, -1], tv[:, -1]], axis=-1)


def _ffn_sample_kernel(h_ref, g_ref, wg_ref, wv_ref, kg_ref, kv_ref, bg_ref, bv_ref, sg_ref, sv_ref, wd_ref, gp_ref,
                       y_ref, tg_ref, tv_ref, hn_ref, acc_ref, act_ref, ug_ref, uv_ref, *, nb, t, nc):
    c = pl.program_id(1)
    tc = wg_ref.shape[1]

    @pl.when(c == 0)
    def _():
        hn_ref[...] = _rms(h_ref[...], g_ref[...]).astype(_bf16)
        acc_ref[...] = jnp.zeros_like(acc_ref)

    ug_ref[:, 6:8, :] = sg_ref[...]
    uv_ref[:, 6:8, :] = sv_ref[...]

    def conv(u_ref, w_ref, k_ref, b_ref, cs):
        half = cs.stop - cs.start
        u_ref[:, 8:8 + t, cs] = _dot(hn_ref[...], w_ref[:, cs]).reshape(nb, t, half)
        cv = (k_ref[2:3, cs] * u_ref[:, 8:8 + t, cs] + k_ref[1:2, cs] * u_ref[:, 7:7 + t, cs]
              + k_ref[0:1, cs] * u_ref[:, 6:6 + t, cs] + b_ref[:, cs])
        return cv.reshape(nb * t, half)

    half = tc // 2
    for cs in (slice(0, half), slice(half, tc)):
        gate = conv(ug_ref, wg_ref, kg_ref, bg_ref, cs)
        val = conv(uv_ref, wv_ref, kv_ref, bv_ref, cs)
        act_ref[:, cs] = (_gelu_tanh(gate) * val).astype(_bf16)
    acc_ref[...] += _dot(act_ref[...], wd_ref[...])
    tg_ref[...] = ug_ref[:, 6 + t:8 + t, :]
    tv_ref[...] = uv_ref[:, 6 + t:8 + t, :]
    _ffn_finish(c, nc, acc_ref, h_ref, gp_ref, y_ref)


def _ffn_sample(h, state, g_pre, w_up, ffn_k, ffn_b, w_down, g_post, t, nb=64, tc=512):
    m, d = h.shape
    n = m // t
    dff = w_down.shape[0]
    nc = dff // tc
    width = ffn_k.shape[0]
    assert width == 3 and t >= width - 1
    tm = nb * t
    row = lambda i, c: (i, 0)
    const = lambda i, c: (0, 0)
    lo = lambda i, c: (0, c)
    hi = lambda i, c: (0, c + nc)
    y, tg, tv = pl.pallas_call(
        functools.partial(_ffn_sample_kernel, nb=nb, t=t, nc=nc),
        grid=(n // nb, nc),
        in_specs=[
            pl.BlockSpec((tm, d), row),
            pl.BlockSpec((1, d), const),
            pl.BlockSpec((d, tc), lo),
            pl.BlockSpec((d, tc), hi),
            pl.BlockSpec((width, tc), lo),
            pl.BlockSpec((width, tc), hi),
            pl.BlockSpec((1, tc), lo),
            pl.BlockSpec((1, tc), hi),
            pl.BlockSpec((nb, width - 1, tc), lambda i, c: (i, 0, c)),
            pl.BlockSpec((nb, width - 1, tc), lambda i, c: (i, 0, c + nc)),
            pl.BlockSpec((tc, d), lambda i, c: (c, 0)),
            pl.BlockSpec((1, d), const),
        ],
        out_specs=[
            pl.BlockSpec((tm, d), row),
            pl.BlockSpec((nb, width - 1, tc), lambda i, c: (i, 0, c)),
            pl.BlockSpec((nb, width - 1, tc), lambda i, c: (i, 0, c)),
        ],
        out_shape=[
            jax.ShapeDtypeStruct((m, d), _f32),
            jax.ShapeDtypeStruct((n, width - 1, dff), _f32),
            jax.ShapeDtypeStruct((n, width - 1, dff), _f32),
        ],
        scratch_shapes=[
            pltpu.VMEM((tm, d), _bf16),
            pltpu.VMEM((tm, d), _f32),
            pltpu.VMEM((tm, tc), _bf16),
            pltpu.VMEM((nb, 8 + t, tc), _f32),
            pltpu.VMEM((nb, 8 + t, tc), _f32),
        ],
        compiler_params=_params("arbitrary", "arbitrary"),
        name="ffn_sample",
    )(h, g_pre, w_up, w_up, ffn_k, ffn_k, ffn_b, ffn_b, state, state, w_down, g_post)
    return y, jnp.concatenate([tg, tv], axis=-1)


def _rel_bucket(dist):
    max_exact = N_BUCKETS // 2
    dd = jnp.maximum(dist, 1).astype(_f32)
    large = max_exact + (jnp.log(dd / max_exact) / math.log(MAX_DISTANCE / max_exact)
                         * (N_BUCKETS - max_exact)).astype(jnp.int32)
    large = jnp.minimum(large, N_BUCKETS - 1)
    return jnp.where(dist < max_exact, dist, large)


def _bias_table(rel_bias):
    n = WINDOW
    by_dist = rel_bias[_rel_bucket(jnp.arange(n))].astype(_f32).T
    g = jnp.concatenate([jnp.full_like(by_dist, MASKED), by_dist[:, ::-1],
                         jnp.full_like(by_dist, MASKED)], axis=1)
    heads, length = g.shape
    skew = jnp.broadcast_to(g[:, None, :], (heads, n, length)).reshape(heads, n * length)
    skew = skew[:, :n * (length - 1)].reshape(heads, n, length - 1)
    return skew[:, :, n - 1:3 * n - 1]


def _row(v):
    return v.reshape(1, -1)


def _layer(xp, xs, cache_k, cache_v, st_conv, st_ffn, p):
    (g_pre, w_in, b_in, conv_k, conv_b, ln_g, ln_b, w_conv, sinks, rel_bias, w_attn, w_out, g_post,
     g_ffn_pre, w_up, ffn_k, ffn_b, w_down, g_ffn_post) = p
    nseq, seq, d = xp.shape
    ns, t, _ = xs.shape
    n_heads = sinks.shape[0]
    keep, kv_heads, hd = cache_k.shape[1:]
    group = n_heads // kv_heads
    dq, dkv = n_heads * hd, kv_heads * hd
    d_conv = conv_k.shape[1]
    assert d_conv == d and dq == d and keep == WINDOW and seq % WINDOW == 0

    w_in_b, w_conv_b, w_attn_b, w_out_b = (w.astype(_bf16) for w in (w_in, w_conv, w_attn, w_out))
    w_in_b = _column_tiles(w_in_b, 512)
    w_up_b, w_down_b = w_up.astype(_bf16), w_down.astype(_bf16)
    g_pre, b_in, conv_b, ln_g, ln_b, g_post, g_ffn_pre, ffn_b, g_ffn_post = map(
        _row, (g_pre, b_in, conv_b, ln_g, ln_b, g_post, g_ffn_pre, ffn_b, g_ffn_post))

    bias = _bias_table(rel_bias)
    keys_s = 2 * WINDOW
    assert t <= WINDOW
    bias_s = bias[:, :t].reshape(kv_heads, group * t, keys_s)
    sink_rows = jnp.repeat(sinks.astype(_f32), t).reshape(kv_heads, group * t, 1)

    outs = []
    for x3, is_prompt in ((xp, True), (xs, False)):
        n = x3.shape[0]
        x = x3.reshape(-1, d)
        glu, qkv, gates = _inproj(x, g_pre, w_in_b, b_in, d_conv, dq + 2 * dkv, 2 * d)
        k_new = qkv[:, dq:dq + dkv].reshape(n, -1, kv_heads, hd)
        v_new = qkv[:, dq + dkv:].reshape(n, -1, kv_heads, hd)
        if is_prompt:
            conv_out = _conv_prompt(glu, conv_k, conv_b, ln_g, ln_b, w_conv_b, nseq)
            attn = _attn_prompt(qkv, bias, sinks.astype(_f32), nseq, n_heads, kv_heads, hd)
            new_conv = glu.reshape(n, seq, d)[:, seq - (conv_k.shape[0] - 1):]
            new_k, new_v = k_new[:, seq - keep:], v_new[:, seq - keep:]
        else:
            glu_ext = jnp.concatenate([st_conv, glu.reshape(n, t, d)], axis=1)
            conv_out = _conv_sample(glu_ext, conv_k, conv_b, ln_g, ln_b, w_conv_b)
            new_conv = glu_ext[:, t:]
            k_all = jnp.concatenate([cache_k, k_new], axis=1)
            v_all = jnp.concatenate([cache_v, v_new], axis=1)
            new_k, new_v = k_all[:, t:], v_all[:, t:]
            pad = ((0, 0), (0, keys_s - keep - t), (0, 0))
            q = qkv[:, :dq].reshape(n, t, kv_heads, group, hd).transpose(0, 2, 3, 1, 4).reshape(n, kv_heads, group * t, hd)
            o = _attn_sample(q, jnp.pad(k_all.reshape(n, keep + t, dkv), pad), jnp.pad(v_all.reshape(n, keep + t, dkv), pad),
                             bias_s, sink_rows)
            attn = o.reshape(n, kv_heads, group, t, hd).transpose(0, 3, 1, 2, 4).reshape(n * t, dq)
        h = _mix(attn, conv_out, gates, x, w_attn_b, w_out_b, g_post)
        if is_prompt:
            y, new_ffn = _ffn_prompt(h, g_ffn_pre, w_up_b, ffn_k, ffn_b, w_down_b, g_ffn_post, nseq)
        else:
            y, new_ffn = _ffn_sample(h, st_ffn, g_ffn_pre, w_up_b, ffn_k, ffn_b, w_down_b, g_ffn_post, t)
        outs.append((y.reshape(x3.shape), new_k, new_v, new_conv, new_ffn))
    return outs


def kernel(x_prompt, x_sample, cache_k, cache_v, state_conv, state_ffn_conv, norm_mix_pre, w_in, b_in, conv_dw_k, conv_dw_b, conv_ln_g, conv_ln_b, w_conv_proj, attn_sinks, rel_bias, w_attn_proj, w_out, norm_mix_post, norm_ffn_pre, w_up, ffn_dw_k, ffn_dw_b, w_down, norm_ffn_post):
    y_p, y_s = x_prompt, x_sample
    per_layer = []
    for l in range(w_in.shape[0]):
        p = (norm_mix_pre[l], w_in[l], b_in[l], conv_dw_k[l], conv_dw_b[l], conv_ln_g[l], conv_ln_b[l], w_conv_proj[l],
             attn_sinks[l], rel_bias, w_attn_proj[l], w_out[l], norm_mix_post[l],
             norm_ffn_pre[l], w_up[l], ffn_dw_k[l], ffn_dw_b[l], w_down[l], norm_ffn_post[l])
        (y_p, *rest_p), (y_s, *rest_s) = _layer(y_p, y_s, cache_k[l], cache_v[l], state_conv[l], state_ffn_conv[l], p)
        per_layer.append(rest_p + rest_s)
    stacked = [jnp.stack(leaves) for leaves in zip(*per_layer)]
    return (y_p, y_s, *stacked)
```

```python
import functools
import math

import jax
import jax.numpy as jnp
from jax import lax
from jax.experimental import pallas as pl
from jax.experimental.pallas import tpu as pltpu

EPS = 1e-6
WINDOW = 128
N_BUCKETS = 32
MAX_DISTANCE = 128
MASKED = -1e30
VMEM_LIMIT_BYTES = 56 * 1024 * 1024
HALO = 32

_bf16 = jnp.bfloat16
_f32 = jnp.float32


def _params(*sem):
    return pltpu.CompilerParams(dimension_semantics=sem, vmem_limit_bytes=VMEM_LIMIT_BYTES)


def _dot(a, b):
    return jnp.dot(a, b, preferred_element_type=_f32)


def _sigmoid(x):
    return 1.0 / (1.0 + jnp.exp(-x))


def _rms(x, g):
    return x * lax.rsqrt(jnp.mean(x * x, axis=-1, keepdims=True) + EPS) * g


def _inproj_kernel(x_ref, g_ref, wa_ref, wb_ref, ba_ref, bb_ref, glu_ref, qkv_ref, gate_ref, xn_ref, *, n_glu, n_qkv):
    j = pl.program_id(1)

    @pl.when(j == 0)
    def _():
        xn_ref[...] = _rms(x_ref[...], g_ref[...]).astype(_bf16)

    half = wb_ref.shape[2] // 2
    halves = [slice(0, half), slice(half, 2 * half)]

    def u(w_ref, b_ref, cs):
        return _dot(xn_ref[...], w_ref[0, :, cs]) + b_ref[:, cs]

    @pl.when(j < n_glu)
    def _():
        for cs in halves:
            glu_ref[:, cs] = u(wa_ref, ba_ref, cs) * _sigmoid(u(wb_ref, bb_ref, cs))

    @pl.when((j >= n_glu) & (j < n_glu + n_qkv))
    def _():
        for cs in halves:
            qkv_ref[:, cs] = u(wb_ref, bb_ref, cs)

    @pl.when(j >= n_glu + n_qkv)
    def _():
        for cs in halves:
            gate_ref[:, cs] = _sigmoid(u(wb_ref, bb_ref, cs))


def _column_tiles(w, tn):
    d, n = w.shape
    return w.reshape(d, n // tn, tn).transpose(1, 0, 2)


def _inproj(x, g, w, b, d_glu, d_qkv, d_gate, tm=1024):
    m, d = x.shape
    tm = min(tm, m)
    tn = w.shape[2]
    n_glu, n_qkv, n_gate = d_glu // tn, d_qkv // tn, d_gate // tn
    assert w.shape[0] == 2 * n_glu + n_qkv + n_gate
    first_w = lambda i, j: (jnp.minimum(j, n_glu - 1), 0, 0)
    rest_w = lambda i, j: (j + n_glu, 0, 0)
    first = lambda i, j: (0, jnp.minimum(j, n_glu - 1))
    rest = lambda i, j: (0, j + n_glu)
    return pl.pallas_call(
        functools.partial(_inproj_kernel, n_glu=n_glu, n_qkv=n_qkv),
        grid=(m // tm, n_glu + n_qkv + n_gate),
        in_specs=[
            pl.BlockSpec((tm, d), lambda i, j: (i, 0)),
            pl.BlockSpec((1, d), lambda i, j: (0, 0)),
            pl.BlockSpec((1, d, tn), first_w),
            pl.BlockSpec((1, d, tn), rest_w),
            pl.BlockSpec((1, tn), first),
            pl.BlockSpec((1, tn), rest),
        ],
        out_specs=[
            pl.BlockSpec((tm, tn), lambda i, j: (i, jnp.minimum(j, n_glu - 1))),
            pl.BlockSpec((tm, tn), lambda i, j: (i, jnp.clip(j - n_glu, 0, n_qkv - 1))),
            pl.BlockSpec((tm, tn), lambda i, j: (i, jnp.maximum(j - n_glu - n_qkv, 0))),
        ],
        out_shape=[
            jax.ShapeDtypeStruct((m, d_glu), _f32),
            jax.ShapeDtypeStruct((m, d_qkv), _f32),
            jax.ShapeDtypeStruct((m, d_gate), _f32),
        ],
        scratch_shapes=[pltpu.VMEM((tm, d), _bf16)],
        compiler_params=_params("arbitrary", "arbitrary"),
        name="inproj",
    )(x, g, w, w, b, b)


def _ln_silu_proj(conv_ref, act_ref, lg_ref, lb_ref, w_ref, o_ref, rows):
    rt = 16

    def body(r, carry):
        r0 = pl.multiple_of(r * rt, rt)
        c = conv_ref[pl.ds(r0, rt), :]
        mu = jnp.mean(c, axis=-1, keepdims=True)
        cc = c - mu
        var = jnp.mean(cc * cc, axis=-1, keepdims=True)
        y = cc * lax.rsqrt(var + EPS) * lg_ref[...] + lb_ref[...]
        act_ref[pl.ds(r0, rt), :] = (y * _sigmoid(y)).astype(_bf16)
        return carry

    lax.fori_loop(0, rows // rt, body, 0, unroll=4)
    o_ref[...] = _dot(act_ref[...], w_ref[...])


def _conv_prompt_kernel(glu_ref, k_ref, cb_ref, lg_ref, lb_ref, w_ref, o_ref, ext_ref, conv_ref, act_ref, *, tm, width):
    d = glu_ref.shape[1]
    i = pl.program_id(1)

    @pl.when(i == 0)
    def _():
        ext_ref[0:HALO, :] = jnp.zeros((HALO, d), _f32)

    @pl.when(i > 0)
    def _():
        ext_ref[0:HALO, :] = ext_ref[tm:tm + HALO, :]

    ext_ref[HALO:HALO + tm, :] = glu_ref[...]

    rt, ct, sub = 64, 128, 8
    first = HALO - (width - 1)

    def body(r, carry):
        r0 = pl.multiple_of(r * rt, rt)
        for c in range(d // ct):
            cs = slice(c * ct, (c + 1) * ct)
            blk = ext_ref[pl.ds(r0, rt + HALO), cs]
            acc = jnp.broadcast_to(cb_ref[:, cs], (rt, ct))
            for phase in range(sub):
                taps = [w for w in range(width) if (first + w) % sub == phase]
                rows = rt if phase == 0 else rt + sub
                part = None
                for w in taps:
                    lo = first + w - phase
                    term = k_ref[w:w + 1, cs] * blk[lo:lo + rows, :]
                    part = term if part is None else part + term
                acc = acc + part[phase:phase + rt, :]
            conv_ref[pl.ds(r0, rt), cs] = acc
        return carry

    lax.fori_loop(0, tm // rt, body, 0)
    _ln_silu_proj(conv_ref, act_ref, lg_ref, lb_ref, w_ref, o_ref, tm)


def _conv_prompt(glu, conv_k, conv_b, ln_g, ln_b, w, nseq, tm=256):
    m, d = glu.shape
    nt = m // nseq // tm
    width = conv_k.shape[0]
    const = lambda b, i: (0, 0)
    return pl.pallas_call(
        functools.partial(_conv_prompt_kernel, tm=tm, width=width),
        grid=(nseq, nt),
        in_specs=[
            pl.BlockSpec((tm, d), lambda b, i: (b * nt + i, 0)),
            pl.BlockSpec((width, d), const),
            pl.BlockSpec((1, d), const),
            pl.BlockSpec((1, d), const),
            pl.BlockSpec((1, d), const),
            pl.BlockSpec((d, d), const),
        ],
        out_specs=pl.BlockSpec((tm, d), lambda b, i: (b * nt + i, 0)),
        out_shape=jax.ShapeDtypeStruct((m, d), _f32),
        scratch_shapes=[pltpu.VMEM((tm + HALO, d), _f32), pltpu.VMEM((tm, d), _f32), pltpu.VMEM((tm, d), _bf16)],
        compiler_params=_params("arbitrary", "arbitrary"),
        name="conv_prompt",
    )(glu, conv_k, conv_b, ln_g, ln_b, w)


def _conv_sample_kernel(ext_ref, k_ref, cb_ref, lg_ref, lb_ref, w_ref, o_ref, conv_ref, act_ref, *, nb, t, width):
    d = ext_ref.shape[2]
    ct = 512

    def body(n, carry):
        r0 = pl.multiple_of(n * t, t)
        for c in range(d // ct):
            cs = slice(c * ct, (c + 1) * ct)
            blk = ext_ref[n, :, cs]
            acc = jnp.broadcast_to(cb_ref[:, cs], (t, ct))
            for w in range(width):
                acc = acc + k_ref[w:w + 1, cs] * blk[w:w + t, :]
            conv_ref[pl.ds(r0, t), cs] = acc
        return carry

    lax.fori_loop(0, nb, body, 0)
    _ln_silu_proj(conv_ref, act_ref, lg_ref, lb_ref, w_ref, o_ref, nb * t)


def _conv_sample(glu_ext, conv_k, conv_b, ln_g, ln_b, w, nb=32):
    n, rows, d = glu_ext.shape
    width = conv_k.shape[0]
    t = rows - (width - 1)
    const = lambda i: (0, 0)
    return pl.pallas_call(
        functools.partial(_conv_sample_kernel, nb=nb, t=t, width=width),
        grid=(n // nb,),
        in_specs=[
            pl.BlockSpec((nb, rows, d), lambda i: (i, 0, 0)),
            pl.BlockSpec((width, d), const),
            pl.BlockSpec((1, d), const),
            pl.BlockSpec((1, d), const),
            pl.BlockSpec((1, d), const),
            pl.BlockSpec((d, d), const),
        ],
        out_specs=pl.BlockSpec((nb * t, d), lambda i: (i, 0)),
        out_shape=jax.ShapeDtypeStruct((n * t, d), _f32),
        scratch_shapes=[pltpu.VMEM((nb * t, d), _f32), pltpu.VMEM((nb * t, d), _bf16)],
        compiler_params=_params("arbitrary"),
        name="conv_sample",
    )(glu_ext, conv_k, conv_b, ln_g, ln_b, w)


def _attn_prompt_kernel(q_ref, kc_ref, kp_ref, vc_ref, vp_ref, bias_ref, sink_ref, o_ref, *, kv_heads, hd):
    blk = q_ref.shape[0]
    grp = 2 * hd
    pairs = q_ref.shape[1] // (kv_heads * grp)
    rows = pairs * blk
    scale = hd ** -0.5
    nt = (((1,), (1,)), ((), ()))
    low = lax.broadcasted_iota(jnp.int32, (2 * blk, grp), 1) < hd
    qi = lax.broadcasted_iota(jnp.int32, (rows, blk), 0) & (blk - 1)
    cur_visible = lax.broadcasted_iota(jnp.int32, (rows, blk), 1) <= qi

    def halves(p_ref, c_ref, h):
        g0 = (h // 2) * grp
        x = jnp.concatenate([p_ref[:, g0:g0 + grp], c_ref[:, g0:g0 + grp]], axis=0)
        swapped = pltpu.roll(x, hd, axis=1)
        in_low, in_high = (x, swapped) if h % 2 == 0 else (swapped, x)
        return jnp.where(low, in_low, 0.0).astype(_bf16), jnp.where(low, 0.0, in_high).astype(_bf16)

    for h in range(kv_heads):
        ks = halves(kp_ref, kc_ref, h)
        vs = halves(vp_ref, vc_ref, h)
        lanes = [slice((h * pairs + p) * grp, (h * pairs + p + 1) * grp) for p in range(pairs)]
        qs = (jnp.concatenate([q_ref[:, ls] for ls in lanes], axis=0) * scale).astype(_bf16)
        out = None
        for parity in range(2):
            s2 = lax.dot_general(qs, ks[parity], nt, preferred_element_type=_f32)
            s = jnp.where(cur_visible, s2[:, blk:], s2[:, :blk]) + bias_ref[0, h, parity]
            sink = sink_ref[h, parity]
            m = jnp.maximum(jnp.max(s, axis=-1, keepdims=True), sink)
            p = jnp.exp(s - m)
            den = jnp.sum(p, axis=-1, keepdims=True) + jnp.exp(sink - m)
            p2 = jnp.concatenate([jnp.where(cur_visible, 0.0, p), jnp.where(cur_visible, p, 0.0)], axis=1)
            o = _dot(p2.astype(_bf16), vs[parity]) * (1.0 / den)
            out = o if out is None else out + o
        for p, ls in enumerate(lanes):
            o_ref[:, ls] = out[p * blk:(p + 1) * blk, :]


def _attn_prompt(qkv, bias, sinks, nseq, n_heads, kv_heads, hd):
    m = qkv.shape[0]
    dq, dkv = n_heads * hd, kv_heads * hd
    nb = m // nseq // WINDOW
    pairs = n_heads // kv_heads // 2
    rows = pairs * WINDOW
    assert WINDOW & (WINDOW - 1) == 0 and 2 * hd == 128 and kv_heads % 2 == 0
    qi = jnp.arange(WINDOW)[:, None]
    cur_visible = jnp.arange(WINDOW)[None, :] <= qi
    folded = jnp.stack([jnp.where(cur_visible, bias[:, :, WINDOW:], MASKED),
                        jnp.where(cur_visible, bias[:, :, WINDOW:], bias[:, :, :WINDOW])])
    folded = folded.reshape(2, kv_heads, pairs, 2, WINDOW, WINDOW).transpose(0, 1, 3, 2, 4, 5)
    folded = folded.reshape(2, kv_heads, 2, rows, WINDOW)
    sink_rows = jnp.repeat(sinks.reshape(kv_heads, pairs, 2).transpose(0, 2, 1), WINDOW, axis=-1)
    sink_rows = sink_rows.reshape(kv_heads, 2, rows, 1)
    kcol, vcol = dq // dkv, dq // dkv + 1
    cur = lambda b, i: b * nb + i
    prev = lambda b, i: b * nb + jnp.maximum(i - 1, 0)
    return pl.pallas_call(
        functools.partial(_attn_prompt_kernel, kv_heads=kv_heads, hd=hd),
        grid=(nseq, nb),
        in_specs=[
            pl.BlockSpec((WINDOW, dq), lambda b, i: (cur(b, i), 0)),
            pl.BlockSpec((WINDOW, dkv), lambda b, i: (cur(b, i), kcol)),
            pl.BlockSpec((WINDOW, dkv), lambda b, i: (prev(b, i), kcol)),
            pl.BlockSpec((WINDOW, dkv), lambda b, i: (cur(b, i), vcol)),
            pl.BlockSpec((WINDOW, dkv), lambda b, i: (prev(b, i), vcol)),
            pl.BlockSpec((1, kv_heads, 2, rows, WINDOW), lambda b, i: (jnp.minimum(i, 1), 0, 0, 0, 0)),
            pl.BlockSpec((kv_heads, 2, rows, 1), lambda b, i: (0, 0, 0, 0)),
        ],
        out_specs=pl.BlockSpec((WINDOW, dq), lambda b, i: (cur(b, i), 0)),
        out_shape=jax.ShapeDtypeStruct((m, dq), _f32),
        compiler_params=_params("arbitrary", "arbitrary"),
        name="attn_prompt",
    )(qkv, qkv, qkv, qkv, qkv, folded, sink_rows)


def _attn_sample_kernel(qkv_ref, ck_ref, cv_ref, bias_ref, sink_ref, o_ref, ko_ref, vo_ref, *,
                        nb, t, n_heads, kv_heads, hd):
    dq, dkv = n_heads * hd, kv_heads * hd
    group = n_heads // kv_heads
    keep = ck_ref.shape[1]
    keys = bias_ref.shape[0]
    grp = 2 * hd
    scale = hd ** -0.5
    nt = (((1,), (1,)), ((), ()))
    low = lax.broadcasted_iota(jnp.int32, (t, grp), 1) < hd
    blank = jnp.zeros((t, grp), _f32)
    pad = jnp.zeros((keys - keep - t, dkv), _f32)

    def body(n, carry):
        r0 = pl.multiple_of(n * t, t)
        q = qkv_ref[pl.ds(r0, t), 0:dq]
        kn = qkv_ref[pl.ds(r0, t), dq:dq + dkv]
        vn = qkv_ref[pl.ds(r0, t), dq + dkv:dq + 2 * dkv]
        ck, cv = ck_ref[n], cv_ref[n]
        ko_ref[n, 0:keep - t, :] = ck[t:, :]
        ko_ref[n, keep - t:keep, :] = kn
        vo_ref[n, 0:keep - t, :] = cv[t:, :]
        vo_ref[n, keep - t:keep, :] = vn
        kp = jnp.concatenate([ck, kn, pad], axis=0).astype(_bf16)
        vp = jnp.concatenate([cv, vn, pad], axis=0).astype(_bf16)

        rows = []
        for h in range(kv_heads):
            for pair in range(group // 2):
                qv = q[:, (h * (group // 2) + pair) * grp:(h * (group // 2) + pair + 1) * grp]
                swapped = pltpu.roll(qv, hd, axis=1)
                for parity in range(2):
                    x = qv if parity == h % 2 else swapped
                    x = jnp.where(low, x, 0.0) if h % 2 == 0 else jnp.where(low, 0.0, x)
                    pieces = [blank] * (dkv // grp)
                    pieces[h // 2] = x
                    rows.append(jnp.concatenate(pieces, axis=1))
        wt = (jnp.concatenate(rows, axis=0) * scale).astype(_bf16)

        st = lax.dot_general(kp, wt, nt, preferred_element_type=_f32) + bias_ref[...]
        sink = sink_ref[...]
        m = jnp.maximum(jnp.max(st, axis=0, keepdims=True), sink)
        p = jnp.exp(st - m)
        den = jnp.sum(p, axis=0, keepdims=True) + jnp.exp(sink - m)
        pn = p * (1.0 / den)
        of = _dot(pn.T.astype(_bf16), vp)

        outs = []
        for h in range(kv_heads):
            ls = slice((h // 2) * grp, (h // 2 + 1) * grp)
            for pair in range(group // 2):
                r = (h * group + 2 * pair) * t
                even, odd = of[r:r + t, ls], of[r + t:r + 2 * t, ls]
                if h % 2 == 0:
                    outs.append(jnp.where(low, even, pltpu.roll(odd, hd, axis=1)))
                else:
                    outs.append(jnp.where(low, pltpu.roll(even, hd, axis=1), odd))
        o_ref[pl.ds(r0, t), :] = jnp.concatenate(outs, axis=1)
        return carry

    lax.fori_loop(0, nb, body, 0)


def _attn_sample(qkv, cache_k, cache_v, bias, sinks, t, n_heads, kv_heads, hd, nb=8):
    n, keep, dkv = cache_k.shape
    dq = n_heads * hd
    keys = bias.shape[2]
    assert keep == WINDOW and keep + t <= keys and 2 * hd == 128 and kv_heads % 2 == 0
    bias_t = bias[:, :t].transpose(2, 0, 1).reshape(keys, n_heads * t)
    sink_row = jnp.repeat(sinks, t).reshape(1, n_heads * t)
    const = lambda i: (0, 0)
    return pl.pallas_call(
        functools.partial(_attn_sample_kernel, nb=nb, t=t, n_heads=n_heads, kv_heads=kv_heads, hd=hd),
        grid=(n // nb,),
        in_specs=[
            pl.BlockSpec((nb * t, qkv.shape[1]), lambda i: (i, 0)),
            pl.BlockSpec((nb, keep, dkv), lambda i: (i, 0, 0)),
            pl.BlockSpec((nb, keep, dkv), lambda i: (i, 0, 0)),
            pl.BlockSpec((keys, n_heads * t), const),
            pl.BlockSpec((1, n_heads * t), const),
        ],
        out_specs=[
            pl.BlockSpec((nb * t, dq), lambda i: (i, 0)),
            pl.BlockSpec((nb, keep, dkv), lambda i: (i, 0, 0)),
            pl.BlockSpec((nb, keep, dkv), lambda i: (i, 0, 0)),
        ],
        out_shape=[
            jax.ShapeDtypeStruct((n * t, dq), _f32),
            jax.ShapeDtypeStruct((n, keep, dkv), _f32),
            jax.ShapeDtypeStruct((n, keep, dkv), _f32),
        ],
        compiler_params=_params("arbitrary"),
        name="attn_sample",
    )(qkv, cache_k, cache_v, bias_t, sink_row)


def _mix_kernel(attn_ref, conv_ref, gc_ref, ga_ref, x_ref, wa_ref, wo_ref, g_ref, h_ref):
    ao = _dot(attn_ref[...].astype(_bf16), wa_ref[...])
    mixed = gc_ref[...] * conv_ref[...] + ga_ref[...] * ao
    o = _dot(mixed.astype(_bf16), wo_ref[...])
    h_ref[...] = x_ref[...] + _rms(o, g_ref[...])


def _mix(attn, conv_out, gates, x, w_attn, w_out, g_post, tm=256):
    m, d = x.shape
    row = lambda i: (i, 0)
    const = lambda i: (0, 0)
    once = pl.Buffered(1)
    return pl.pallas_call(
        _mix_kernel,
        grid=(m // tm,),
        in_specs=[
            pl.BlockSpec((tm, d), row),
            pl.BlockSpec((tm, d), row),
            pl.BlockSpec((tm, d), lambda i: (i, 0)),
            pl.BlockSpec((tm, d), lambda i: (i, 1)),
            pl.BlockSpec((tm, d), row),
            pl.BlockSpec((d, d), const, pipeline_mode=once),
            pl.BlockSpec((d, d), const, pipeline_mode=once),
            pl.BlockSpec((1, d), const),
        ],
        out_specs=pl.BlockSpec((tm, d), row),
        out_shape=jax.ShapeDtypeStruct((m, d), _f32),
        compiler_params=_params("arbitrary"),
        name="mix",
    )(attn, conv_out, gates, gates, x, w_attn, w_out, g_post)


def _gelu_tanh(x):
    return 0.5 * x * (1.0 + jnp.tanh(math.sqrt(2.0 / math.pi) * (x + 0.044715 * (x * x * x))))


def _ffn_finish(c, nc, acc_ref, h_ref, gp_ref, y_ref):
    @pl.when(c == nc - 1)
    def _():
        y_ref[...] = h_ref[...] + _rms(acc_ref[...], gp_ref[...])


def _ffn_prompt_kernel(h_ref, g_ref, wg_ref, wv_ref, kg_ref, kv_ref, bg_ref, bv_ref, wd_ref, gp_ref,
                       y_ref, tg_ref, tv_ref, hn_ref, acc_ref, act_ref, ug_ref, uv_ref, cg_ref, cv_ref, *, tm, nc):
    i, c = pl.program_id(1), pl.program_id(2)
    tc = wg_ref.shape[1]

    @pl.when(c == 0)
    def _():
        hn_ref[...] = _rms(h_ref[...], g_ref[...]).astype(_bf16)
        acc_ref[...] = jnp.zeros_like(acc_ref)

    @pl.when(i == 0)
    def _():
        ug_ref[0:8, :] = jnp.zeros((8, tc), _f32)
        uv_ref[0:8, :] = jnp.zeros((8, tc), _f32)

    @pl.when(i > 0)
    def _():
        ug_ref[0:8, :] = cg_ref[c]
        uv_ref[0:8, :] = cv_ref[c]

    def conv(u_ref, w_ref, k_ref, b_ref, cs):
        u_ref[8:8 + tm, cs] = _dot(hn_ref[...], w_ref[:, cs])
        return (k_ref[2:3, cs] * u_ref[8:8 + tm, cs] + k_ref[1:2, cs] * u_ref[7:7 + tm, cs]
                + k_ref[0:1, cs] * u_ref[6:6 + tm, cs] + b_ref[:, cs])

    half = tc // 2
    for cs in (slice(0, half), slice(half, tc)):
        gate = conv(ug_ref, wg_ref, kg_ref, bg_ref, cs)
        val = conv(uv_ref, wv_ref, kv_ref, bv_ref, cs)
        act_ref[:, cs] = (_gelu_tanh(gate) * val).astype(_bf16)
    acc_ref[...] += _dot(act_ref[...], wd_ref[...])
    cg_ref[c] = ug_ref[tm:tm + 8, :]
    cv_ref[c] = uv_ref[tm:tm + 8, :]
    tg_ref[0, 0] = ug_ref[tm + 6:tm + 8, :]
    tv_ref[0, 0] = uv_ref[tm + 6:tm + 8, :]
    _ffn_finish(c, nc, acc_ref, h_ref, gp_ref, y_ref)


def _ffn_prompt(h, g_pre, w_up, ffn_k, ffn_b, w_down, g_post, nseq, tm=512, tc=512):
    m, d = h.shape
    dff = w_down.shape[0]
    nt, nc = m // nseq // tm, dff // tc
    width = ffn_k.shape[0]
    assert width == 3
    row = lambda b, i, c: (b * nt + i, 0)
    const = lambda b, i, c: (0, 0)
    lo = lambda b, i, c: (0, c)
    hi = lambda b, i, c: (0, c + nc)
    y, tg, tv = pl.pallas_call(
        functools.partial(_ffn_prompt_kernel, tm=tm, nc=nc),
        grid=(nseq, nt, nc),
        in_specs=[
            pl.BlockSpec((tm, d), row),
            pl.BlockSpec((1, d), const),
            pl.BlockSpec((d, tc), lo),
            pl.BlockSpec((d, tc), hi),
            pl.BlockSpec((width, tc), lo),
            pl.BlockSpec((width, tc), hi),
            pl.BlockSpec((1, tc), lo),
            pl.BlockSpec((1, tc), hi),
            pl.BlockSpec((tc, d), lambda b, i, c: (c, 0)),
            pl.BlockSpec((1, d), const),
        ],
        out_specs=[
            pl.BlockSpec((tm, d), row),
            pl.BlockSpec((1, 1, width - 1, tc), lambda b, i, c: (b, i, 0, c)),
            pl.BlockSpec((1, 1, width - 1, tc), lambda b, i, c: (b, i, 0, c)),
        ],
        out_shape=[
            jax.ShapeDtypeStruct((m, d), _f32),
            jax.ShapeDtypeStruct((nseq, nt, width - 1, dff), _f32),
            jax.ShapeDtypeStruct((nseq, nt, width - 1, dff), _f32),
        ],
        scratch_shapes=[
            pltpu.VMEM((tm, d), _bf16),
            pltpu.VMEM((tm, d), _f32),
            pltpu.VMEM((tm, tc), _bf16),
            pltpu.VMEM((tm + 8, tc), _f32),
            pltpu.VMEM((tm + 8, tc), _f32),
            pltpu.VMEM((nc, 8, tc), _f32),
            pltpu.VMEM((nc, 8, tc), _f32),
        ],
        compiler_params=_params("arbitrary", "arbitrary", "arbitrary"),
        name="ffn_prompt",
    )(h, g_pre, w_up, w_up, ffn_k, ffn_k, ffn_b, ffn_b, w_down, g_post)
    return y, jnp.concatenate([tg[:, -1], tv[:, -1]], axis=-1)


def _ffn_sample_kernel(h_ref, g_ref, wg_ref, wv_ref, kg_ref, kv_ref, bg_ref, bv_ref, sg_ref, sv_ref, wd_ref, gp_ref,
                       y_ref, tg_ref, tv_ref, hn_ref, acc_ref, act_ref, ug_ref, uv_ref, *, nb, t, nc):
    c = pl.program_id(1)
    tc = wg_ref.shape[1]

    @pl.when(c == 0)
    def _():
        hn_ref[...] = _rms(h_ref[...], g_ref[...]).astype(_bf16)
        acc_ref[...] = jnp.zeros_like(acc_ref)

    ug_ref[:, 6:8, :] = sg_ref[...]
    uv_ref[:, 6:8, :] = sv_ref[...]

    def conv(u_ref, w_ref, k_ref, b_ref, cs):
        half = cs.stop - cs.start
        u_ref[:, 8:8 + t, cs] = _dot(hn_ref[...], w_ref[:, cs]).reshape(nb, t, half)
        cv = (k_ref[2:3, cs] * u_ref[:, 8:8 + t, cs] + k_ref[1:2, cs] * u_ref[:, 7:7 + t, cs]
              + k_ref[0:1, cs] * u_ref[:, 6:6 + t, cs] + b_ref[:, cs])
        return cv.reshape(nb * t, half)

    half = tc // 2
    for cs in (slice(0, half), slice(half, tc)):
        gate = conv(ug_ref, wg_ref, kg_ref, bg_ref, cs)
        val = conv(uv_ref, wv_ref, kv_ref, bv_ref, cs)
        act_ref[:, cs] = (_gelu_tanh(gate) * val).astype(_bf16)
    acc_ref[...] += _dot(act_ref[...], wd_ref[...])
    tg_ref[...] = ug_ref[:, 6 + t:8 + t, :]
    tv_ref[...] = uv_ref[:, 6 + t:8 + t, :]
    _ffn_finish(c, nc, acc_ref, h_ref, gp_ref, y_ref)


def _ffn_sample(h, state, g_pre, w_up, ffn_k, ffn_b, w_down, g_post, t, nb=64, tc=512):
    m, d = h.shape
    n = m // t
    dff = w_down.shape[0]
    nc = dff // tc
    width = ffn_k.shape[0]
    assert width == 3 and t >= width - 1
    tm = nb * t
    row = lambda i, c: (i, 0)
    const = lambda i, c: (0, 0)
    lo = lambda i, c: (0, c)
    hi = lambda i, c: (0, c + nc)
    y, tg, tv = pl.pallas_call(
        functools.partial(_ffn_sample_kernel, nb=nb, t=t, nc=nc),
        grid=(n // nb, nc),
        in_specs=[
            pl.BlockSpec((tm, d), row),
            pl.BlockSpec((1, d), const),
            pl.BlockSpec((d, tc), lo),
            pl.BlockSpec((d, tc), hi),
            pl.BlockSpec((width, tc), lo),
            pl.BlockSpec((width, tc), hi),
            pl.BlockSpec((1, tc), lo),
            pl.BlockSpec((1, tc), hi),
            pl.BlockSpec((nb, width - 1, tc), lambda i, c: (i, 0, c)),
            pl.BlockSpec((nb, width - 1, tc), lambda i, c: (i, 0, c + nc)),
            pl.BlockSpec((tc, d), lambda i, c: (c, 0)),
            pl.BlockSpec((1, d), const),
        ],
        out_specs=[
            pl.BlockSpec((tm, d), row),
            pl.BlockSpec((nb, width - 1, tc), lambda i, c: (i, 0, c)),
            pl.BlockSpec((nb, width - 1, tc), lambda i, c: (i, 0, c)),
        ],
        out_shape=[
            jax.ShapeDtypeStruct((m, d), _f32),
            jax.ShapeDtypeStruct((n, width - 1, dff), _f32),
            jax.ShapeDtypeStruct((n, width - 1, dff), _f32),
        ],
        scratch_shapes=[
            pltpu.VMEM((tm, d), _bf16),
            pltpu.VMEM((tm, d), _f32),
            pltpu.VMEM((tm, tc), _bf16),
            pltpu.VMEM((nb, 8 + t, tc), _f32),
            pltpu.VMEM((nb, 8 + t, tc), _f32),
        ],
        compiler_params=_params("arbitrary", "arbitrary"),
        name="ffn_sample",
    )(h, g_pre, w_up, w_up, ffn_k, ffn_k, ffn_b, ffn_b, state, state, w_down, g_post)
    return y, jnp.concatenate([tg, tv], axis=-1)


def _rel_bucket(dist):
    max_exact = N_BUCKETS // 2
    dd = jnp.maximum(dist, 1).astype(_f32)
    large = max_exact + (jnp.log(dd / max_exact) / math.log(MAX_DISTANCE / max_exact)
                         * (N_BUCKETS - max_exact)).astype(jnp.int32)
    large = jnp.minimum(large, N_BUCKETS - 1)
    return jnp.where(dist < max_exact, dist, large)


def _bias_table(rel_bias):
    n = WINDOW
    by_dist = rel_bias[_rel_bucket(jnp.arange(n))].astype(_f32).T
    g = jnp.concatenate([jnp.full_like(by_dist, MASKED), by_dist[:, ::-1],
                         jnp.full_like(by_dist, MASKED)], axis=1)
    heads, length = g.shape
    skew = jnp.broadcast_to(g[:, None, :], (heads, n, length)).reshape(heads, n * length)
    skew = skew[:, :n * (length - 1)].reshape(heads, n, length - 1)
    return skew[:, :, n - 1:3 * n - 1]


def _row(v):
    return v.reshape(1, -1)


def _layer(xp, xs, cache_k, cache_v, st_conv, st_ffn, p):
    (g_pre, w_in, b_in, conv_k, conv_b, ln_g, ln_b, w_conv, sinks, rel_bias, w_attn, w_out, g_post,
     g_ffn_pre, w_up, ffn_k, ffn_b, w_down, g_ffn_post) = p
    nseq, seq, d = xp.shape
    ns, t, _ = xs.shape
    n_heads = sinks.shape[0]
    keep, kv_heads, hd = cache_k.shape[1:]
    group = n_heads // kv_heads
    dq, dkv = n_heads * hd, kv_heads * hd
    d_conv = conv_k.shape[1]
    assert d_conv == d and dq == d and keep == WINDOW and seq % WINDOW == 0

    w_in_b, w_conv_b, w_attn_b, w_out_b = (w.astype(_bf16) for w in (w_in, w_conv, w_attn, w_out))
    w_in_b = _column_tiles(w_in_b, 512)
    w_up_b, w_down_b = w_up.astype(_bf16), w_down.astype(_bf16)
    g_pre, b_in, conv_b, ln_g, ln_b, g_post, g_ffn_pre, ffn_b, g_ffn_post = map(
        _row, (g_pre, b_in, conv_b, ln_g, ln_b, g_post, g_ffn_pre, ffn_b, g_ffn_post))

    bias = _bias_table(rel_bias)

    outs = []
    for x3, is_prompt in ((xp, True), (xs, False)):
        n = x3.shape[0]
        x = x3.reshape(-1, d)
        glu, qkv, gates = _inproj(x, g_pre, w_in_b, b_in, d_conv, dq + 2 * dkv, 2 * d)
        k_new = qkv[:, dq:dq + dkv].reshape(n, -1, kv_heads, hd)
        v_new = qkv[:, dq + dkv:].reshape(n, -1, kv_heads, hd)
        if is_prompt:
            conv_out = _conv_prompt(glu, conv_k, conv_b, ln_g, ln_b, w_conv_b, nseq)
            attn = _attn_prompt(qkv, bias, sinks.astype(_f32), nseq, n_heads, kv_heads, hd)
            new_conv = glu.reshape(n, seq, d)[:, seq - (conv_k.shape[0] - 1):]
            new_k, new_v = k_new[:, seq - keep:], v_new[:, seq - keep:]
        else:
            glu_ext = jnp.concatenate([st_conv, glu.reshape(n, t, d)], axis=1)
            conv_out = _conv_sample(glu_ext, conv_k, conv_b, ln_g, ln_b, w_conv_b)
            new_conv = glu_ext[:, t:]
            attn, new_k, new_v = _attn_sample(qkv, cache_k.reshape(n, keep, dkv), cache_v.reshape(n, keep, dkv), bias,
                                              sinks.astype(_f32), t, n_heads, kv_heads, hd)
            new_k, new_v = (a.reshape(n, keep, kv_heads, hd) for a in (new_k, new_v))
        h = _mix(attn, conv_out, gates, x, w_attn_b, w_out_b, g_post)
        if is_prompt:
            y, new_ffn = _ffn_prompt(h, g_ffn_pre, w_up_b, ffn_k, ffn_b, w_down_b, g_ffn_post, nseq)
        else:
            y, new_ffn = _ffn_sample(h, st_ffn, g_ffn_pre, w_up_b, ffn_k, ffn_b, w_down_b, g_ffn_post, t)
        outs.append((y.reshape(x3.shape), new_k, new_v, new_conv, new_ffn))
    return outs


def kernel(x_prompt, x_sample, cache_k, cache_v, state_conv, state_ffn_conv, norm_mix_pre, w_in, b_in, conv_dw_k, conv_dw_b, conv_ln_g, conv_ln_b, w_conv_proj, attn_sinks, rel_bias, w_attn_proj, w_out, norm_mix_post, norm_ffn_pre, w_up, ffn_dw_k, ffn_dw_b, w_down, norm_ffn_post):
    y_p, y_s = x_prompt, x_sample
    per_layer = []
    for l in range(w_in.shape[0]):
        p = (norm_mix_pre[l], w_in[l], b_in[l], conv_dw_k[l], conv_dw_b[l], conv_ln_g[l], conv_ln_b[l], w_conv_proj[l],
             attn_sinks[l], rel_bias, w_attn_proj[l], w_out[l], norm_mix_post[l],
             norm_ffn_pre[l], w_up[l], ffn_dw_k[l], ffn_dw_b[l], w_down[l], norm_ffn_post[l])
        (y_p, *rest_p), (y_s, *rest_s) = _layer(y_p, y_s, cache_k[l], cache_v[l], state_conv[l], state_ffn_conv[l], p)
        per_layer.append(rest_p + rest_s)
    stacked = [jnp.stack(leaves) for leaves in zip(*per_layer)]
    return (y_p, y_s, *stacked)
```

```python
import functools
import math

import jax
import jax.numpy as jnp
from jax import lax
from jax.experimental import pallas as pl
from jax.experimental.pallas import tpu as pltpu

EPS = 1e-6
WINDOW = 128
N_BUCKETS = 32
MAX_DISTANCE = 128
MASKED = -1e30
VMEM_LIMIT_BYTES = 60 * 1024 * 1024
HALO = 32

_bf16 = jnp.bfloat16
_f32 = jnp.float32


def _params(*sem):
    return pltpu.CompilerParams(dimension_semantics=sem, vmem_limit_bytes=VMEM_LIMIT_BYTES)


def _dot(a, b):
    return jnp.dot(a, b, preferred_element_type=_f32)


def _sigmoid(x):
    return 1.0 / (1.0 + jnp.exp(-x))


def _rms(x, g):
    return x * lax.rsqrt(jnp.mean(x * x, axis=-1, keepdims=True) + EPS) * g


def _inproj_kernel(x_ref, g_ref, wa_ref, wb_ref, ba_ref, bb_ref, glu_ref, qkv_ref, gate_ref, xn_ref, *, n_glu, n_qkv):
    j = pl.program_id(1)

    @pl.when(j == 0)
    def _():
        xn_ref[...] = _rms(x_ref[...], g_ref[...]).astype(_bf16)

    half = wb_ref.shape[2] // 2
    halves = [slice(0, half), slice(half, 2 * half)]

    def u(w_ref, b_ref, cs):
        return _dot(xn_ref[...], w_ref[0, :, cs]) + b_ref[:, cs]

    @pl.when(j < n_glu)
    def _():
        for cs in halves:
            glu_ref[:, cs] = u(wa_ref, ba_ref, cs) * _sigmoid(u(wb_ref, bb_ref, cs))

    @pl.when((j >= n_glu) & (j < n_glu + n_qkv))
    def _():
        for cs in halves:
            qkv_ref[:, cs] = u(wb_ref, bb_ref, cs)

    @pl.when(j >= n_glu + n_qkv)
    def _():
        for cs in halves:
            gate_ref[:, cs] = _sigmoid(u(wb_ref, bb_ref, cs))


def _column_tiles(w, tn):
    d, n = w.shape
    return w.reshape(d, n // tn, tn).transpose(1, 0, 2)


def _inproj(x, g, w, b, d_glu, d_qkv, d_gate, tm=1024):
    m, d = x.shape
    tm = min(tm, m)
    tn = w.shape[2]
    n_glu, n_qkv, n_gate = d_glu // tn, d_qkv // tn, d_gate // tn
    assert w.shape[0] == 2 * n_glu + n_qkv + n_gate
    first_w = lambda i, j: (jnp.minimum(j, n_glu - 1), 0, 0)
    rest_w = lambda i, j: (j + n_glu, 0, 0)
    first = lambda i, j: (0, jnp.minimum(j, n_glu - 1))
    rest = lambda i, j: (0, j + n_glu)
    return pl.pallas_call(
        functools.partial(_inproj_kernel, n_glu=n_glu, n_qkv=n_qkv),
        grid=(m // tm, n_glu + n_qkv + n_gate),
        in_specs=[
            pl.BlockSpec((tm, d), lambda i, j: (i, 0)),
            pl.BlockSpec((1, d), lambda i, j: (0, 0)),
            pl.BlockSpec((1, d, tn), first_w),
            pl.BlockSpec((1, d, tn), rest_w),
            pl.BlockSpec((1, tn), first),
            pl.BlockSpec((1, tn), rest),
        ],
        out_specs=[
            pl.BlockSpec((tm, tn), lambda i, j: (i, jnp.minimum(j, n_glu - 1))),
            pl.BlockSpec((tm, tn), lambda i, j: (i, jnp.clip(j - n_glu, 0, n_qkv - 1))),
            pl.BlockSpec((tm, tn), lambda i, j: (i, jnp.maximum(j - n_glu - n_qkv, 0))),
        ],
        out_shape=[
            jax.ShapeDtypeStruct((m, d_glu), _f32),
            jax.ShapeDtypeStruct((m, d_qkv), _f32),
            jax.ShapeDtypeStruct((m, d_gate), _f32),
        ],
        scratch_shapes=[pltpu.VMEM((tm, d), _bf16)],
        compiler_params=_params("arbitrary", "arbitrary"),
        name="inproj",
    )(x, g, w, w, b, b)


def _ln_silu_proj(conv_ref, act_ref, lg_ref, lb_ref, w_ref, o_ref, rows):
    rt = 16

    def body(r, carry):
        r0 = pl.multiple_of(r * rt, rt)
        c = conv_ref[pl.ds(r0, rt), :]
        mu = jnp.mean(c, axis=-1, keepdims=True)
        cc = c - mu
        var = jnp.mean(cc * cc, axis=-1, keepdims=True)
        y = cc * lax.rsqrt(var + EPS) * lg_ref[...] + lb_ref[...]
        act_ref[pl.ds(r0, rt), :] = (y * _sigmoid(y)).astype(_bf16)
        return carry

    lax.fori_loop(0, rows // rt, body, 0, unroll=4)
    o_ref[...] = _dot(act_ref[...], w_ref[...])


def _conv_prompt_kernel(glu_ref, k_ref, cb_ref, lg_ref, lb_ref, w_ref, o_ref, ext_ref, conv_ref, act_ref, *, tm, width):
    d = glu_ref.shape[1]
    i = pl.program_id(1)

    @pl.when(i == 0)
    def _():
        ext_ref[0:HALO, :] = jnp.zeros((HALO, d), _f32)

    @pl.when(i > 0)
    def _():
        ext_ref[0:HALO, :] = ext_ref[tm:tm + HALO, :]

    ext_ref[HALO:HALO + tm, :] = glu_ref[...]

    rt, ct, sub = 64, 128, 8
    first = HALO - (width - 1)

    def body(r, carry):
        r0 = pl.multiple_of(r * rt, rt)
        for c in range(d // ct):
            cs = slice(c * ct, (c + 1) * ct)
            blk = ext_ref[pl.ds(r0, rt + HALO), cs]
            acc = jnp.broadcast_to(cb_ref[:, cs], (rt, ct))
            for phase in range(sub):
                taps = [w for w in range(width) if (first + w) % sub == phase]
                rows = rt if phase == 0 else rt + sub
                part = None
                for w in taps:
                    lo = first + w - phase
                    term = k_ref[w:w + 1, cs] * blk[lo:lo + rows, :]
                    part = term if part is None else part + term
                acc = acc + part[phase:phase + rt, :]
            conv_ref[pl.ds(r0, rt), cs] = acc
        return carry

    lax.fori_loop(0, tm // rt, body, 0)
    _ln_silu_proj(conv_ref, act_ref, lg_ref, lb_ref, w_ref, o_ref, tm)


def _conv_prompt(glu, conv_k, conv_b, ln_g, ln_b, w, nseq, tm=256):
    m, d = glu.shape
    nt = m // nseq // tm
    width = conv_k.shape[0]
    const = lambda b, i: (0, 0)
    return pl.pallas_call(
        functools.partial(_conv_prompt_kernel, tm=tm, width=width),
        grid=(nseq, nt),
        in_specs=[
            pl.BlockSpec((tm, d), lambda b, i: (b * nt + i, 0)),
            pl.BlockSpec((width, d), const),
            pl.BlockSpec((1, d), const),
            pl.BlockSpec((1, d), const),
            pl.BlockSpec((1, d), const),
            pl.BlockSpec((d, d), const),
        ],
        out_specs=pl.BlockSpec((tm, d), lambda b, i: (b * nt + i, 0)),
        out_shape=jax.ShapeDtypeStruct((m, d), _f32),
        scratch_shapes=[pltpu.VMEM((tm + HALO, d), _f32), pltpu.VMEM((tm, d), _f32), pltpu.VMEM((tm, d), _bf16)],
        compiler_params=_params("arbitrary", "arbitrary"),
        name="conv_prompt",
    )(glu, conv_k, conv_b, ln_g, ln_b, w)


def _conv_sample_kernel(ext_ref, k_ref, cb_ref, lg_ref, lb_ref, w_ref, o_ref, conv_ref, act_ref, *, nb, t, width):
    d = ext_ref.shape[2]
    ct = 512

    def body(n, carry):
        r0 = pl.multiple_of(n * t, t)
        for c in range(d // ct):
            cs = slice(c * ct, (c + 1) * ct)
            blk = ext_ref[n, :, cs]
            acc = jnp.broadcast_to(cb_ref[:, cs], (t, ct))
            for w in range(width):
                acc = acc + k_ref[w:w + 1, cs] * blk[w:w + t, :]
            conv_ref[pl.ds(r0, t), cs] = acc
        return carry

    lax.fori_loop(0, nb, body, 0)
    _ln_silu_proj(conv_ref, act_ref, lg_ref, lb_ref, w_ref, o_ref, nb * t)


def _conv_sample(glu_ext, conv_k, conv_b, ln_g, ln_b, w, nb=32):
    n, rows, d = glu_ext.shape
    width = conv_k.shape[0]
    t = rows - (width - 1)
    const = lambda i: (0, 0)
    return pl.pallas_call(
        functools.partial(_conv_sample_kernel, nb=nb, t=t, width=width),
        grid=(n // nb,),
        in_specs=[
            pl.BlockSpec((nb, rows, d), lambda i: (i, 0, 0)),
            pl.BlockSpec((width, d), const),
            pl.BlockSpec((1, d), const),
            pl.BlockSpec((1, d), const),
            pl.BlockSpec((1, d), const),
            pl.BlockSpec((d, d), const),
        ],
        out_specs=pl.BlockSpec((nb * t, d), lambda i: (i, 0)),
        out_shape=jax.ShapeDtypeStruct((n * t, d), _f32),
        scratch_shapes=[pltpu.VMEM((nb * t, d), _f32), pltpu.VMEM((nb * t, d), _bf16)],
        compiler_params=_params("arbitrary"),
        name="conv_sample",
    )(glu_ext, conv_k, conv_b, ln_g, ln_b, w)


def _attn_prompt_kernel(q_ref, kc_ref, kp_ref, vc_ref, vp_ref, bias_ref, sink_ref, o_ref, *, kv_heads, hd):
    blk = q_ref.shape[0]
    grp = 2 * hd
    pairs = q_ref.shape[1] // (kv_heads * grp)
    rows = pairs * blk
    scale = hd ** -0.5
    nt = (((1,), (1,)), ((), ()))
    low = lax.broadcasted_iota(jnp.int32, (2 * blk, grp), 1) < hd
    qi = lax.broadcasted_iota(jnp.int32, (rows, blk), 0) & (blk - 1)
    cur_visible = lax.broadcasted_iota(jnp.int32, (rows, blk), 1) <= qi

    def halves(p_ref, c_ref, h):
        g0 = (h // 2) * grp
        x = jnp.concatenate([p_ref[:, g0:g0 + grp], c_ref[:, g0:g0 + grp]], axis=0)
        swapped = pltpu.roll(x, hd, axis=1)
        in_low, in_high = (x, swapped) if h % 2 == 0 else (swapped, x)
        return jnp.where(low, in_low, 0.0).astype(_bf16), jnp.where(low, 0.0, in_high).astype(_bf16)

    for h in range(kv_heads):
        ks = halves(kp_ref, kc_ref, h)
        vs = halves(vp_ref, vc_ref, h)
        lanes = [slice((h * pairs + p) * grp, (h * pairs + p + 1) * grp) for p in range(pairs)]
        qs = (jnp.concatenate([q_ref[:, ls] for ls in lanes], axis=0) * scale).astype(_bf16)
        out = None
        for parity in range(2):
            s2 = lax.dot_general(qs, ks[parity], nt, preferred_element_type=_f32)
            s = jnp.where(cur_visible, s2[:, blk:], s2[:, :blk]) + bias_ref[0, h, parity]
            sink = sink_ref[h, parity]
            m = jnp.maximum(jnp.max(s, axis=-1, keepdims=True), sink)
            p = jnp.exp(s - m)
            den = jnp.sum(p, axis=-1, keepdims=True) + jnp.exp(sink - m)
            p2 = jnp.concatenate([jnp.where(cur_visible, 0.0, p), jnp.where(cur_visible, p, 0.0)], axis=1)
            o = _dot(p2.astype(_bf16), vs[parity]) * (1.0 / den)
            out = o if out is None else out + o
        for p, ls in enumerate(lanes):
            o_ref[:, ls] = out[p * blk:(p + 1) * blk, :]


def _attn_prompt(qkv, bias, sinks, nseq, n_heads, kv_heads, hd):
    m = qkv.shape[0]
    dq, dkv = n_heads * hd, kv_heads * hd
    nb = m // nseq // WINDOW
    pairs = n_heads // kv_heads // 2
    rows = pairs * WINDOW
    assert WINDOW & (WINDOW - 1) == 0 and 2 * hd == 128 and kv_heads % 2 == 0
    qi = jnp.arange(WINDOW)[:, None]
    cur_visible = jnp.arange(WINDOW)[None, :] <= qi
    folded = jnp.stack([jnp.where(cur_visible, bias[:, :, WINDOW:], MASKED),
                        jnp.where(cur_visible, bias[:, :, WINDOW:], bias[:, :, :WINDOW])])
    folded = folded.reshape(2, kv_heads, pairs, 2, WINDOW, WINDOW).transpose(0, 1, 3, 2, 4, 5)
    folded = folded.reshape(2, kv_heads, 2, rows, WINDOW)
    sink_rows = jnp.repeat(sinks.reshape(kv_heads, pairs, 2).transpose(0, 2, 1), WINDOW, axis=-1)
    sink_rows = sink_rows.reshape(kv_heads, 2, rows, 1)
    kcol, vcol = dq // dkv, dq // dkv + 1
    cur = lambda b, i: b * nb + i
    prev = lambda b, i: b * nb + jnp.maximum(i - 1, 0)
    return pl.pallas_call(
        functools.partial(_attn_prompt_kernel, kv_heads=kv_heads, hd=hd),
        grid=(nseq, nb),
        in_specs=[
            pl.BlockSpec((WINDOW, dq), lambda b, i: (cur(b, i), 0)),
            pl.BlockSpec((WINDOW, dkv), lambda b, i: (cur(b, i), kcol)),
            pl.BlockSpec((WINDOW, dkv), lambda b, i: (prev(b, i), kcol)),
            pl.BlockSpec((WINDOW, dkv), lambda b, i: (cur(b, i), vcol)),
            pl.BlockSpec((WINDOW, dkv), lambda b, i: (prev(b, i), vcol)),
            pl.BlockSpec((1, kv_heads, 2, rows, WINDOW), lambda b, i: (jnp.minimum(i, 1), 0, 0, 0, 0)),
            pl.BlockSpec((kv_heads, 2, rows, 1), lambda b, i: (0, 0, 0, 0)),
        ],
        out_specs=pl.BlockSpec((WINDOW, dq), lambda b, i: (cur(b, i), 0)),
        out_shape=jax.ShapeDtypeStruct((m, dq), _f32),
        compiler_params=_params("arbitrary", "arbitrary"),
        name="attn_prompt",
    )(qkv, qkv, qkv, qkv, qkv, folded, sink_rows)


def _attn_sample_kernel(qkv_ref, ck_ref, cv_ref, bias_ref, sink_ref, o_ref, ko_ref, vo_ref, *,
                        nb, t, n_heads, kv_heads, hd):
    dq, dkv = n_heads * hd, kv_heads * hd
    group = n_heads // kv_heads
    keep = ck_ref.shape[1]
    keys = bias_ref.shape[0]
    grp = 2 * hd
    scale = hd ** -0.5
    nt = (((1,), (1,)), ((), ()))
    low = lax.broadcasted_iota(jnp.int32, (t, grp), 1) < hd
    blank = jnp.zeros((t, grp), _f32)
    pad = jnp.zeros((keys - keep - t, dkv), _f32)

    def body(n, carry):
        r0 = pl.multiple_of(n * t, t)
        q = qkv_ref[pl.ds(r0, t), 0:dq]
        kn = qkv_ref[pl.ds(r0, t), dq:dq + dkv]
        vn = qkv_ref[pl.ds(r0, t), dq + dkv:dq + 2 * dkv]
        ck, cv = ck_ref[n], cv_ref[n]
        ko_ref[n, 0:keep - t, :] = ck[t:, :]
        ko_ref[n, keep - t:keep, :] = kn
        vo_ref[n, 0:keep - t, :] = cv[t:, :]
        vo_ref[n, keep - t:keep, :] = vn
        kp = jnp.concatenate([ck, kn, pad], axis=0).astype(_bf16)
        vp = jnp.concatenate([cv, vn, pad], axis=0).astype(_bf16)

        rows = []
        for h in range(kv_heads):
            for pair in range(group // 2):
                qv = q[:, (h * (group // 2) + pair) * grp:(h * (group // 2) + pair + 1) * grp]
                swapped = pltpu.roll(qv, hd, axis=1)
                for parity in range(2):
                    x = qv if parity == h % 2 else swapped
                    x = jnp.where(low, x, 0.0) if h % 2 == 0 else jnp.where(low, 0.0, x)
                    pieces = [blank] * (dkv // grp)
                    pieces[h // 2] = x
                    rows.append(jnp.concatenate(pieces, axis=1))
        wt = (jnp.concatenate(rows, axis=0) * scale).astype(_bf16)

        st = lax.dot_general(kp, wt, nt, preferred_element_type=_f32) + bias_ref[...]
        sink = sink_ref[...]
        m = jnp.maximum(jnp.max(st, axis=0, keepdims=True), sink)
        p = jnp.exp(st - m)
        den = jnp.sum(p, axis=0, keepdims=True) + jnp.exp(sink - m)
        pn = p * (1.0 / den)
        of = _dot(pn.T.astype(_bf16), vp)

        outs = []
        for h in range(kv_heads):
            ls = slice((h // 2) * grp, (h // 2 + 1) * grp)
            for pair in range(group // 2):
                r = (h * group + 2 * pair) * t
                even, odd = of[r:r + t, ls], of[r + t:r + 2 * t, ls]
                if h % 2 == 0:
                    outs.append(jnp.where(low, even, pltpu.roll(odd, hd, axis=1)))
                else:
                    outs.append(jnp.where(low, pltpu.roll(even, hd, axis=1), odd))
        o_ref[pl.ds(r0, t), :] = jnp.concatenate(outs, axis=1)
        return carry

    lax.fori_loop(0, nb, body, 0)


def _attn_sample(qkv, cache_k, cache_v, bias, sinks, t, n_heads, kv_heads, hd, nb=8):
    n, keep, dkv = cache_k.shape
    dq = n_heads * hd
    keys = bias.shape[2]
    assert keep == WINDOW and keep + t <= keys and 2 * hd == 128 and kv_heads % 2 == 0
    bias_t = bias[:, :t].transpose(2, 0, 1).reshape(keys, n_heads * t)
    sink_row = jnp.repeat(sinks, t).reshape(1, n_heads * t)
    const = lambda i: (0, 0)
    return pl.pallas_call(
        functools.partial(_attn_sample_kernel, nb=nb, t=t, n_heads=n_heads, kv_heads=kv_heads, hd=hd),
        grid=(n // nb,),
        in_specs=[
            pl.BlockSpec((nb * t, qkv.shape[1]), lambda i: (i, 0)),
            pl.BlockSpec((nb, keep, dkv), lambda i: (i, 0, 0)),
            pl.BlockSpec((nb, keep, dkv), lambda i: (i, 0, 0)),
            pl.BlockSpec((keys, n_heads * t), const),
            pl.BlockSpec((1, n_heads * t), const),
        ],
        out_specs=[
            pl.BlockSpec((nb * t, dq), lambda i: (i, 0)),
            pl.BlockSpec((nb, keep, dkv), lambda i: (i, 0, 0)),
            pl.BlockSpec((nb, keep, dkv), lambda i: (i, 0, 0)),
        ],
        out_shape=[
            jax.ShapeDtypeStruct((n * t, dq), _f32),
            jax.ShapeDtypeStruct((n, keep, dkv), _f32),
            jax.ShapeDtypeStruct((n, keep, dkv), _f32),
        ],
        compiler_params=_params("arbitrary"),
        name="attn_sample",
    )(qkv, cache_k, cache_v, bias_t, sink_row)


def _mix_kernel(attn_ref, conv_ref, gc_ref, ga_ref, x_ref, wa_ref, wo_ref, g_ref, h_ref):
    ao = _dot(attn_ref[...].astype(_bf16), wa_ref[...])
    mixed = gc_ref[...] * conv_ref[...] + ga_ref[...] * ao
    o = _dot(mixed.astype(_bf16), wo_ref[...])
    h_ref[...] = x_ref[...] + _rms(o, g_ref[...])


def _mix(attn, conv_out, gates, x, w_attn, w_out, g_post, tm=256):
    m, d = x.shape
    row = lambda i: (i, 0)
    const = lambda i: (0, 0)
    once = pl.Buffered(1)
    return pl.pallas_call(
        _mix_kernel,
        grid=(m // tm,),
        in_specs=[
            pl.BlockSpec((tm, d), row),
            pl.BlockSpec((tm, d), row),
            pl.BlockSpec((tm, d), lambda i: (i, 0)),
            pl.BlockSpec((tm, d), lambda i: (i, 1)),
            pl.BlockSpec((tm, d), row),
            pl.BlockSpec((d, d), const, pipeline_mode=once),
            pl.BlockSpec((d, d), const, pipeline_mode=once),
            pl.BlockSpec((1, d), const),
        ],
        out_specs=pl.BlockSpec((tm, d), row),
        out_shape=jax.ShapeDtypeStruct((m, d), _f32),
        compiler_params=_params("arbitrary"),
        name="mix",
    )(attn, conv_out, gates, gates, x, w_attn, w_out, g_post)


def _gelu_tanh(x):
    return 0.5 * x * (1.0 + jnp.tanh(math.sqrt(2.0 / math.pi) * (x + 0.044715 * (x * x * x))))


MXU_COLUMNS = 256
FFN_CHUNK = 1024


def _chunks(n, width=MXU_COLUMNS):
    return [slice(a, a + width) for a in range(0, n, width)]


def _up_tiles(w_up, tc):
    d = w_up.shape[0]
    return w_up.reshape(d, 2, -1, tc).transpose(2, 0, 1, 3).reshape(-1, d, 2 * tc).astype(_bf16)


def _ffn_finish(c, nc, acc_ref, h_ref, gp_ref, y_ref):
    @pl.when(c == nc - 1)
    def _():
        y_ref[...] = h_ref[...] + _rms(acc_ref[...], gp_ref[...])


def _ffn_prompt_kernel(h_ref, g_ref, w_ref, kg_ref, kv_ref, bg_ref, bv_ref, wd_ref, gp_ref,
                       y_ref, tg_ref, tv_ref, hn_ref, acc_ref, act_ref, ug_ref, uv_ref, cg_ref, cv_ref, *, tm, nc):
    i, c = pl.program_id(1), pl.program_id(2)
    tc = w_ref.shape[2] // 2

    @pl.when(c == 0)
    def _():
        hn_ref[...] = _rms(h_ref[...], g_ref[...]).astype(_bf16)
        acc_ref[...] = jnp.zeros_like(acc_ref)

    @pl.when(i == 0)
    def _():
        ug_ref[0:8, :] = jnp.zeros((8, tc), _f32)
        uv_ref[0:8, :] = jnp.zeros((8, tc), _f32)

    @pl.when(i > 0)
    def _():
        ug_ref[0:8, :] = cg_ref[c]
        uv_ref[0:8, :] = cv_ref[c]

    def conv(u_ref, col0, k_ref, b_ref, cs):
        u_ref[8:8 + tm, cs] = _dot(hn_ref[...], w_ref[0, :, col0 + cs.start:col0 + cs.stop])
        return (k_ref[2:3, cs] * u_ref[8:8 + tm, cs] + k_ref[1:2, cs] * u_ref[7:7 + tm, cs]
                + k_ref[0:1, cs] * u_ref[6:6 + tm, cs] + b_ref[:, cs])

    for cs in _chunks(tc):
        gate = conv(ug_ref, 0, kg_ref, bg_ref, cs)
        val = conv(uv_ref, tc, kv_ref, bv_ref, cs)
        act_ref[:, cs] = (_gelu_tanh(gate) * val).astype(_bf16)
    acc_ref[...] += _dot(act_ref[...], wd_ref[...])
    cg_ref[c] = ug_ref[tm:tm + 8, :]
    cv_ref[c] = uv_ref[tm:tm + 8, :]
    tg_ref[0, 0] = ug_ref[tm + 6:tm + 8, :]
    tv_ref[0, 0] = uv_ref[tm + 6:tm + 8, :]
    _ffn_finish(c, nc, acc_ref, h_ref, gp_ref, y_ref)


def _ffn_prompt(h, g_pre, w_up, ffn_k, ffn_b, w_down, g_post, nseq, tm=512):
    m, d = h.shape
    dff = w_down.shape[0]
    nc, tc = w_up.shape[0], w_up.shape[2] // 2
    nt = m // nseq // tm
    width = ffn_k.shape[0]
    assert width == 3 and nc * tc == dff
    row = lambda b, i, c: (b * nt + i, 0)
    const = lambda b, i, c: (0, 0)
    lo = lambda b, i, c: (0, c)
    hi = lambda b, i, c: (0, c + nc)
    y, tg, tv = pl.pallas_call(
        functools.partial(_ffn_prompt_kernel, tm=tm, nc=nc),
        grid=(nseq, nt, nc),
        in_specs=[
            pl.BlockSpec((tm, d), row),
            pl.BlockSpec((1, d), const),
            pl.BlockSpec((1, d, 2 * tc), lambda b, i, c: (c, 0, 0)),
            pl.BlockSpec((width, tc), lo),
            pl.BlockSpec((width, tc), hi),
            pl.BlockSpec((1, tc), lo),
            pl.BlockSpec((1, tc), hi),
            pl.BlockSpec((tc, d), lambda b, i, c: (c, 0)),
            pl.BlockSpec((1, d), const),
        ],
        out_specs=[
            pl.BlockSpec((tm, d), row, pipeline_mode=pl.Buffered(1)),
            pl.BlockSpec((1, 1, width - 1, tc), lambda b, i, c: (b, i, 0, c)),
            pl.BlockSpec((1, 1, width - 1, tc), lambda b, i, c: (b, i, 0, c)),
        ],
        out_shape=[
            jax.ShapeDtypeStruct((m, d), _f32),
            jax.ShapeDtypeStruct((nseq, nt, width - 1, dff), _f32),
            jax.ShapeDtypeStruct((nseq, nt, width - 1, dff), _f32),
        ],
        scratch_shapes=[
            pltpu.VMEM((tm, d), _bf16),
            pltpu.VMEM((tm, d), _f32),
            pltpu.VMEM((tm, tc), _bf16),
            pltpu.VMEM((tm + 8, tc), _f32),
            pltpu.VMEM((tm + 8, tc), _f32),
            pltpu.VMEM((nc, 8, tc), _f32),
            pltpu.VMEM((nc, 8, tc), _f32),
        ],
        compiler_params=_params("arbitrary", "arbitrary", "arbitrary"),
        name="ffn_prompt",
    )(h, g_pre, w_up, ffn_k, ffn_k, ffn_b, ffn_b, w_down, g_post)
    return y, jnp.concatenate([tg[:, -1], tv[:, -1]], axis=-1)


def _ffn_sample_kernel(h_ref, g_ref, w_ref, kg_ref, kv_ref, bg_ref, bv_ref, sg_ref, sv_ref, wd_ref, gp_ref,
                       y_ref, tg_ref, tv_ref, hn_ref, acc_ref, act_ref, ug_ref, uv_ref, *, nb, t, nc):
    c = pl.program_id(1)
    tc = w_ref.shape[2] // 2

    @pl.when(c == 0)
    def _():
        hn_ref[...] = _rms(h_ref[...], g_ref[...]).astype(_bf16)
        acc_ref[...] = jnp.zeros_like(acc_ref)

    ug_ref[:, 6:8, :] = sg_ref[...]
    uv_ref[:, 6:8, :] = sv_ref[...]

    def conv(u_ref, col0, k_ref, b_ref, cs):
        width = cs.stop - cs.start
        u_ref[:, 8:8 + t, cs] = _dot(hn_ref[...], w_ref[0, :, col0 + cs.start:col0 + cs.stop]).reshape(nb, t, width)
        cv = (k_ref[2:3, cs] * u_ref[:, 8:8 + t, cs] + k_ref[1:2, cs] * u_ref[:, 7:7 + t, cs]
              + k_ref[0:1, cs] * u_ref[:, 6:6 + t, cs] + b_ref[:, cs])
        return cv.reshape(nb * t, width)

    for cs in _chunks(tc):
        gate = conv(ug_ref, 0, kg_ref, bg_ref, cs)
        val = conv(uv_ref, tc, kv_ref, bv_ref, cs)
        act_ref[:, cs] = (_gelu_tanh(gate) * val).astype(_bf16)
    acc_ref[...] += _dot(act_ref[...], wd_ref[...])
    tg_ref[...] = ug_ref[:, 6 + t:8 + t, :]
    tv_ref[...] = uv_ref[:, 6 + t:8 + t, :]
    _ffn_finish(c, nc, acc_ref, h_ref, gp_ref, y_ref)


def _ffn_sample(h, state, g_pre, w_up, ffn_k, ffn_b, w_down, g_post, t, nb=64):
    m, d = h.shape
    n = m // t
    dff = w_down.shape[0]
    nc, tc = w_up.shape[0], w_up.shape[2] // 2
    width = ffn_k.shape[0]
    assert width == 3 and t >= width - 1 and nc * tc == dff
    tm = nb * t
    row = lambda i, c: (i, 0)
    const = lambda i, c: (0, 0)
    lo = lambda i, c: (0, c)
    hi = lambda i, c: (0, c + nc)
    y, tg, tv = pl.pallas_call(
        functools.partial(_ffn_sample_kernel, nb=nb, t=t, nc=nc),
        grid=(n // nb, nc),
        in_specs=[
            pl.BlockSpec((tm, d), row, pipeline_mode=pl.Buffered(1)),
            pl.BlockSpec((1, d), const),
            pl.BlockSpec((1, d, 2 * tc), lambda i, c: (c, 0, 0)),
            pl.BlockSpec((width, tc), lo),
            pl.BlockSpec((width, tc), hi),
            pl.BlockSpec((1, tc), lo),
            pl.BlockSpec((1, tc), hi),
            pl.BlockSpec((nb, width - 1, tc), lambda i, c: (i, 0, c)),
            pl.BlockSpec((nb, width - 1, tc), lambda i, c: (i, 0, c + nc)),
            pl.BlockSpec((tc, d), lambda i, c: (c, 0)),
            pl.BlockSpec((1, d), const),
        ],
        out_specs=[
            pl.BlockSpec((tm, d), row, pipeline_mode=pl.Buffered(1)),
            pl.BlockSpec((nb, width - 1, tc), lambda i, c: (i, 0, c)),
            pl.BlockSpec((nb, width - 1, tc), lambda i, c: (i, 0, c)),
        ],
        out_shape=[
            jax.ShapeDtypeStruct((m, d), _f32),
            jax.ShapeDtypeStruct((n, width - 1, dff), _f32),
            jax.ShapeDtypeStruct((n, width - 1, dff), _f32),
        ],
        scratch_shapes=[
            pltpu.VMEM((tm, d), _bf16),
            pltpu.VMEM((tm, d), _f32),
            pltpu.VMEM((tm, tc), _bf16),
            pltpu.VMEM((nb, 8 + t, tc), _f32),
            pltpu.VMEM((nb, 8 + t, tc), _f32),
        ],
        compiler_params=_params("arbitrary", "arbitrary"),
        name="ffn_sample",
    )(h, g_pre, w_up, ffn_k, ffn_k, ffn_b, ffn_b, state, state, w_down, g_post)
    return y, jnp.concatenate([tg, tv], axis=-1)


def _rel_bucket(dist):
    max_exact = N_BUCKETS // 2
    dd = jnp.maximum(dist, 1).astype(_f32)
    large = max_exact + (jnp.log(dd / max_exact) / math.log(MAX_DISTANCE / max_exact)
                         * (N_BUCKETS - max_exact)).astype(jnp.int32)
    large = jnp.minimum(large, N_BUCKETS - 1)
    return jnp.where(dist < max_exact, dist, large)


def _bias_table(rel_bias):
    n = WINDOW
    by_dist = rel_bias[_rel_bucket(jnp.arange(n))].astype(_f32).T
    g = jnp.concatenate([jnp.full_like(by_dist, MASKED), by_dist[:, ::-1],
                         jnp.full_like(by_dist, MASKED)], axis=1)
    heads, length = g.shape
    skew = jnp.broadcast_to(g[:, None, :], (heads, n, length)).reshape(heads, n * length)
    skew = skew[:, :n * (length - 1)].reshape(heads, n, length - 1)
    return skew[:, :, n - 1:3 * n - 1]


def _row(v):
    return v.reshape(1, -1)


def _layer(xp, xs, cache_k, cache_v, st_conv, st_ffn, p):
    (g_pre, w_in, b_in, conv_k, conv_b, ln_g, ln_b, w_conv, sinks, rel_bias, w_attn, w_out, g_post,
     g_ffn_pre, w_up, ffn_k, ffn_b, w_down, g_ffn_post) = p
    nseq, seq, d = xp.shape
    ns, t, _ = xs.shape
    n_heads = sinks.shape[0]
    keep, kv_heads, hd = cache_k.shape[1:]
    group = n_heads // kv_heads
    dq, dkv = n_heads * hd, kv_heads * hd
    d_conv = conv_k.shape[1]
    assert d_conv == d and dq == d and keep == WINDOW and seq % WINDOW == 0

    w_conv_b, w_attn_b, w_out_b, w_down_b = (w.astype(_bf16) for w in (w_conv, w_attn, w_out, w_down))
    w_in_b = _column_tiles(w_in, 512).astype(_bf16)
    w_up_b = _up_tiles(w_up, min(FFN_CHUNK, w_down.shape[0]))
    g_pre, b_in, conv_b, ln_g, ln_b, g_post, g_ffn_pre, ffn_b, g_ffn_post = map(
        _row, (g_pre, b_in, conv_b, ln_g, ln_b, g_post, g_ffn_pre, ffn_b, g_ffn_post))

    bias = _bias_table(rel_bias)

    outs = []
    for x3, is_prompt in ((xp, True), (xs, False)):
        n = x3.shape[0]
        x = x3.reshape(-1, d)
        glu, qkv, gates = _inproj(x, g_pre, w_in_b, b_in, d_conv, dq + 2 * dkv, 2 * d)
        k_new = qkv[:, dq:dq + dkv].reshape(n, -1, kv_heads, hd)
        v_new = qkv[:, dq + dkv:].reshape(n, -1, kv_heads, hd)
        if is_prompt:
            conv_out = _conv_prompt(glu, conv_k, conv_b, ln_g, ln_b, w_conv_b, nseq)
            attn = _attn_prompt(qkv, bias, sinks.astype(_f32), nseq, n_heads, kv_heads, hd)
            new_conv = glu.reshape(n, seq, d)[:, seq - (conv_k.shape[0] - 1):]
            new_k, new_v = k_new[:, seq - keep:], v_new[:, seq - keep:]
        else:
            glu_ext = jnp.concatenate([st_conv, glu.reshape(n, t, d)], axis=1)
            conv_out = _conv_sample(glu_ext, conv_k, conv_b, ln_g, ln_b, w_conv_b)
            new_conv = glu_ext[:, t:]
            attn, new_k, new_v = _attn_sample(qkv, cache_k.reshape(n, keep, dkv), cache_v.reshape(n, keep, dkv), bias,
                                              sinks.astype(_f32), t, n_heads, kv_heads, hd)
            new_k, new_v = (a.reshape(n, keep, kv_heads, hd) for a in (new_k, new_v))
        h = _mix(attn, conv_out, gates, x, w_attn_b, w_out_b, g_post)
        if is_prompt:
            y, new_ffn = _ffn_prompt(h, g_ffn_pre, w_up_b, ffn_k, ffn_b, w_down_b, g_ffn_post, nseq)
        else:
            y, new_ffn = _ffn_sample(h, st_ffn, g_ffn_pre, w_up_b, ffn_k, ffn_b, w_down_b, g_ffn_post, t)
        outs.append((y.reshape(x3.shape), new_k, new_v, new_conv, new_ffn))
    return outs


def kernel(x_prompt, x_sample, cache_k, cache_v, state_conv, state_ffn_conv, norm_mix_pre, w_in, b_in, conv_dw_k, conv_dw_b, conv_ln_g, conv_ln_b, w_conv_proj, attn_sinks, rel_bias, w_attn_proj, w_out, norm_mix_post, norm_ffn_pre, w_up, ffn_dw_k, ffn_dw_b, w_down, norm_ffn_post):
    y_p, y_s = x_prompt, x_sample
    per_layer = []
    for l in range(w_in.shape[0]):
        p = (norm_mix_pre[l], w_in[l], b_in[l], conv_dw_k[l], conv_dw_b[l], conv_ln_g[l], conv_ln_b[l], w_conv_proj[l],
             attn_sinks[l], rel_bias, w_attn_proj[l], w_out[l], norm_mix_post[l],
             norm_ffn_pre[l], w_up[l], ffn_dw_k[l], ffn_dw_b[l], w_down[l], norm_ffn_post[l])
        (y_p, *rest_p), (y_s, *rest_s) = _layer(y_p, y_s, cache_k[l], cache_v[l], state_conv[l], state_ffn_conv[l], p)
        per_layer.append(rest_p + rest_s)
    stacked = [jnp.stack(leaves) for leaves in zip(*per_layer)]
    return (y_p, y_s, *stacked)
```

```python
import functools
import math

import jax
import jax.numpy as jnp
from jax import lax
from jax.experimental import pallas as pl
from jax.experimental.pallas import tpu as pltpu

EPS = 1e-6
WINDOW = 128
N_BUCKETS = 32
MAX_DISTANCE = 128
MASKED = -1e30
VMEM_LIMIT_BYTES = 60 * 1024 * 1024
HALO = 32

_bf16 = jnp.bfloat16
_f32 = jnp.float32


def _params(*sem):
    return pltpu.CompilerParams(dimension_semantics=sem, vmem_limit_bytes=VMEM_LIMIT_BYTES)


def _dot(a, b):
    return jnp.dot(a, b, preferred_element_type=_f32)


def _sigmoid(x):
    return 1.0 / (1.0 + jnp.exp(-x))


def _rms(x, g):
    return x * lax.rsqrt(jnp.mean(x * x, axis=-1, keepdims=True) + EPS) * g


def _inproj_kernel(x_ref, g_ref, wa_ref, wb_ref, ba_ref, bb_ref, glu_ref, qkv_ref, gate_ref, xn_ref, *, n_glu, n_qkv):
    j = pl.program_id(1)

    @pl.when(j == 0)
    def _():
        xn_ref[...] = _rms(x_ref[...], g_ref[...]).astype(_bf16)

    half = wb_ref.shape[2] // 2
    halves = [slice(0, half), slice(half, 2 * half)]

    def u(w_ref, b_ref, cs):
        return _dot(xn_ref[...], w_ref[0, :, cs]) + b_ref[:, cs]

    @pl.when(j < n_glu)
    def _():
        for cs in halves:
            glu_ref[:, cs] = u(wa_ref, ba_ref, cs) * _sigmoid(u(wb_ref, bb_ref, cs))

    @pl.when((j >= n_glu) & (j < n_glu + n_qkv))
    def _():
        for cs in halves:
            qkv_ref[:, cs] = u(wb_ref, bb_ref, cs)

    @pl.when(j >= n_glu + n_qkv)
    def _():
        for cs in halves:
            gate_ref[:, cs] = _sigmoid(u(wb_ref, bb_ref, cs))


def _cast_tile_kernel(w_ref, o_ref):
    o_ref[0] = w_ref[...].astype(_bf16)


def _column_tiles(w, tn, groups=1):
    d, n = w.shape
    nt = n // groups // tn
    return pl.pallas_call(
        _cast_tile_kernel,
        grid=(nt, groups),
        in_specs=[pl.BlockSpec((d, tn), lambda c, g: (0, g * nt + c))],
        out_specs=pl.BlockSpec((1, d, tn), lambda c, g: (c, 0, g)),
        out_shape=jax.ShapeDtypeStruct((nt, d, groups * tn), _bf16),
        compiler_params=_params("arbitrary", "arbitrary"),
        name="weight_tiles",
    )(w)


def _inproj(x, g, w, b, d_glu, d_qkv, d_gate, tm=1024):
    m, d = x.shape
    tm = min(tm, m)
    tn = w.shape[2]
    n_glu, n_qkv, n_gate = d_glu // tn, d_qkv // tn, d_gate // tn
    assert w.shape[0] == 2 * n_glu + n_qkv + n_gate
    first_w = lambda i, j: (jnp.minimum(j, n_glu - 1), 0, 0)
    rest_w = lambda i, j: (j + n_glu, 0, 0)
    first = lambda i, j: (0, jnp.minimum(j, n_glu - 1))
    rest = lambda i, j: (0, j + n_glu)
    return pl.pallas_call(
        functools.partial(_inproj_kernel, n_glu=n_glu, n_qkv=n_qkv),
        grid=(m // tm, n_glu + n_qkv + n_gate),
        in_specs=[
            pl.BlockSpec((tm, d), lambda i, j: (i, 0)),
            pl.BlockSpec((1, d), lambda i, j: (0, 0)),
            pl.BlockSpec((1, d, tn), first_w),
            pl.BlockSpec((1, d, tn), rest_w),
            pl.BlockSpec((1, tn), first),
            pl.BlockSpec((1, tn), rest),
        ],
        out_specs=[
            pl.BlockSpec((tm, tn), lambda i, j: (i, jnp.minimum(j, n_glu - 1))),
            pl.BlockSpec((tm, tn), lambda i, j: (i, jnp.clip(j - n_glu, 0, n_qkv - 1))),
            pl.BlockSpec((tm, tn), lambda i, j: (i, jnp.maximum(j - n_glu - n_qkv, 0))),
        ],
        out_shape=[
            jax.ShapeDtypeStruct((m, d_glu), _f32),
            jax.ShapeDtypeStruct((m, d_qkv), _f32),
            jax.ShapeDtypeStruct((m, d_gate), _f32),
        ],
        scratch_shapes=[pltpu.VMEM((tm, d), _bf16)],
        compiler_params=_params("arbitrary", "arbitrary"),
        name="inproj",
    )(x, g, w, w, b, b)


def _ln_silu_proj(conv_ref, act_ref, lg_ref, lb_ref, w_ref, o_ref, rows):
    rt = 16

    def body(r, carry):
        r0 = pl.multiple_of(r * rt, rt)
        c = conv_ref[pl.ds(r0, rt), :]
        mu = jnp.mean(c, axis=-1, keepdims=True)
        cc = c - mu
        var = jnp.mean(cc * cc, axis=-1, keepdims=True)
        y = cc * lax.rsqrt(var + EPS) * lg_ref[...] + lb_ref[...]
        act_ref[pl.ds(r0, rt), :] = (y * _sigmoid(y)).astype(_bf16)
        return carry

    lax.fori_loop(0, rows // rt, body, 0, unroll=4)
    o_ref[...] = _dot(act_ref[...], w_ref[...])


def _conv_prompt_kernel(glu_ref, k_ref, cb_ref, lg_ref, lb_ref, w_ref, o_ref, ext_ref, conv_ref, act_ref, *, tm, width):
    d = glu_ref.shape[1]
    i = pl.program_id(1)

    @pl.when(i == 0)
    def _():
        ext_ref[0:HALO, :] = jnp.zeros((HALO, d), _f32)

    @pl.when(i > 0)
    def _():
        ext_ref[0:HALO, :] = ext_ref[tm:tm + HALO, :]

    ext_ref[HALO:HALO + tm, :] = glu_ref[...]

    rt, ct, sub = 64, 128, 8
    first = HALO - (width - 1)

    def body(r, carry):
        r0 = pl.multiple_of(r * rt, rt)
        for c in range(d // ct):
            cs = slice(c * ct, (c + 1) * ct)
            blk = ext_ref[pl.ds(r0, rt + HALO), cs]
            acc = jnp.broadcast_to(cb_ref[:, cs], (rt, ct))
            for phase in range(sub):
                taps = [w for w in range(width) if (first + w) % sub == phase]
                rows = rt if phase == 0 else rt + sub
                part = None
                for w in taps:
                    lo = first + w - phase
                    term = k_ref[w:w + 1, cs] * blk[lo:lo + rows, :]
                    part = term if part is None else part + term
                acc = acc + part[phase:phase + rt, :]
            conv_ref[pl.ds(r0, rt), cs] = acc
        return carry

    lax.fori_loop(0, tm // rt, body, 0)
    _ln_silu_proj(conv_ref, act_ref, lg_ref, lb_ref, w_ref, o_ref, tm)


def _conv_prompt(glu, conv_k, conv_b, ln_g, ln_b, w, nseq, tm=256):
    m, d = glu.shape
    nt = m // nseq // tm
    width = conv_k.shape[0]
    const = lambda b, i: (0, 0)
    return pl.pallas_call(
        functools.partial(_conv_prompt_kernel, tm=tm, width=width),
        grid=(nseq, nt),
        in_specs=[
            pl.BlockSpec((tm, d), lambda b, i: (b * nt + i, 0)),
            pl.BlockSpec((width, d), const),
            pl.BlockSpec((1, d), const),
            pl.BlockSpec((1, d), const),
            pl.BlockSpec((1, d), const),
            pl.BlockSpec((d, d), const),
        ],
        out_specs=pl.BlockSpec((tm, d), lambda b, i: (b * nt + i, 0)),
        out_shape=jax.ShapeDtypeStruct((m, d), _f32),
        scratch_shapes=[pltpu.VMEM((tm + HALO, d), _f32), pltpu.VMEM((tm, d), _f32), pltpu.VMEM((tm, d), _bf16)],
        compiler_params=_params("arbitrary", "arbitrary"),
        name="conv_prompt",
    )(glu, conv_k, conv_b, ln_g, ln_b, w)


def _conv_sample_kernel(st_ref, glu_ref, k_ref, cb_ref, lg_ref, lb_ref, w_ref, o_ref, ns_ref,
                        ext_ref, conv_ref, act_ref, *, nb, t, width):
    d = glu_ref.shape[1]
    hist = width - 1
    ct = 512

    def body(n, carry):
        r0 = pl.multiple_of(n * t, t)
        ext_ref[0:hist, :] = st_ref[0, n]
        ext_ref[hist:hist + t, :] = glu_ref[pl.ds(r0, t), :]
        ns_ref[0, n] = ext_ref[t:t + hist, :]
        for c in range(d // ct):
            cs = slice(c * ct, (c + 1) * ct)
            blk = ext_ref[0:hist + t, cs]
            acc = jnp.broadcast_to(cb_ref[:, cs], (t, ct))
            for w in range(width):
                acc = acc + k_ref[w:w + 1, cs] * blk[w:w + t, :]
            conv_ref[pl.ds(r0, t), cs] = acc
        return carry

    lax.fori_loop(0, nb, body, 0)
    _ln_silu_proj(conv_ref, act_ref, lg_ref, lb_ref, w_ref, o_ref, nb * t)


def _conv_sample(state, layer, glu, conv_k, conv_b, ln_g, ln_b, w, nb=32):
    _, n, hist, d = state.shape
    width = conv_k.shape[0]
    t = glu.shape[0] // n
    assert hist == width - 1 and t <= hist
    const = lambda i: (0, 0)
    return pl.pallas_call(
        functools.partial(_conv_sample_kernel, nb=nb, t=t, width=width),
        grid=(n // nb,),
        in_specs=[
            pl.BlockSpec((1, nb, hist, d), lambda i: (layer, i, 0, 0)),
            pl.BlockSpec((nb * t, d), lambda i: (i, 0)),
            pl.BlockSpec((width, d), const),
            pl.BlockSpec((1, d), const),
            pl.BlockSpec((1, d), const),
            pl.BlockSpec((1, d), const),
            pl.BlockSpec((d, d), const),
        ],
        out_specs=[
            pl.BlockSpec((nb * t, d), lambda i: (i, 0)),
            pl.BlockSpec((1, nb, hist, d), lambda i: (0, i, 0, 0)),
        ],
        out_shape=[
            jax.ShapeDtypeStruct((n * t, d), _f32),
            jax.ShapeDtypeStruct((1, n, hist, d), _f32),
        ],
        scratch_shapes=[pltpu.VMEM((hist + t + (-(hist + t)) % 8, d), _f32), pltpu.VMEM((nb * t, d), _f32),
                        pltpu.VMEM((nb * t, d), _bf16)],
        compiler_params=_params("arbitrary"),
        name="conv_sample",
    )(state, glu, conv_k, conv_b, ln_g, ln_b, w)


def _attn_prompt_kernel(q_ref, kc_ref, kp_ref, vc_ref, vp_ref, bias_ref, sink_ref, o_ref, *, kv_heads, hd):
    blk = q_ref.shape[0]
    grp = 2 * hd
    pairs = q_ref.shape[1] // (kv_heads * grp)
    rows = pairs * blk
    scale = hd ** -0.5
    nt = (((1,), (1,)), ((), ()))
    low = lax.broadcasted_iota(jnp.int32, (2 * blk, grp), 1) < hd
    qi = lax.broadcasted_iota(jnp.int32, (rows, blk), 0) & (blk - 1)
    cur_visible = lax.broadcasted_iota(jnp.int32, (rows, blk), 1) <= qi

    def halves(p_ref, c_ref, h):
        g0 = (h // 2) * grp
        x = jnp.concatenate([p_ref[:, g0:g0 + grp], c_ref[:, g0:g0 + grp]], axis=0)
        swapped = pltpu.roll(x, hd, axis=1)
        in_low, in_high = (x, swapped) if h % 2 == 0 else (swapped, x)
        return jnp.where(low, in_low, 0.0).astype(_bf16), jnp.where(low, 0.0, in_high).astype(_bf16)

    for h in range(kv_heads):
        ks = halves(kp_ref, kc_ref, h)
        vs = halves(vp_ref, vc_ref, h)
        lanes = [slice((h * pairs + p) * grp, (h * pairs + p + 1) * grp) for p in range(pairs)]
        qs = (jnp.concatenate([q_ref[:, ls] for ls in lanes], axis=0) * scale).astype(_bf16)
        out = None
        for parity in range(2):
            s2 = lax.dot_general(qs, ks[parity], nt, preferred_element_type=_f32)
            s = jnp.where(cur_visible, s2[:, blk:], s2[:, :blk]) + bias_ref[0, h, parity]
            sink = sink_ref[h, parity]
            m = jnp.maximum(jnp.max(s, axis=-1, keepdims=True), sink)
            p = jnp.exp(s - m)
            den = jnp.sum(p, axis=-1, keepdims=True) + jnp.exp(sink - m)
            p2 = jnp.concatenate([jnp.where(cur_visible, 0.0, p), jnp.where(cur_visible, p, 0.0)], axis=1)
            o = _dot(p2.astype(_bf16), vs[parity]) * (1.0 / den)
            out = o if out is None else out + o
        for p, ls in enumerate(lanes):
            o_ref[:, ls] = out[p * blk:(p + 1) * blk, :]


def _attn_prompt(qkv, bias, sinks, nseq, n_heads, kv_heads, hd):
    m = qkv.shape[0]
    dq, dkv = n_heads * hd, kv_heads * hd
    nb = m // nseq // WINDOW
    pairs = n_heads // kv_heads // 2
    rows = pairs * WINDOW
    assert WINDOW & (WINDOW - 1) == 0 and 2 * hd == 128 and kv_heads % 2 == 0
    qi = jnp.arange(WINDOW)[:, None]
    cur_visible = jnp.arange(WINDOW)[None, :] <= qi
    folded = jnp.stack([jnp.where(cur_visible, bias[:, :, WINDOW:], MASKED),
                        jnp.where(cur_visible, bias[:, :, WINDOW:], bias[:, :, :WINDOW])])
    folded = folded.reshape(2, kv_heads, pairs, 2, WINDOW, WINDOW).transpose(0, 1, 3, 2, 4, 5)
    folded = folded.reshape(2, kv_heads, 2, rows, WINDOW)
    sink_rows = jnp.repeat(sinks.reshape(kv_heads, pairs, 2).transpose(0, 2, 1), WINDOW, axis=-1)
    sink_rows = sink_rows.reshape(kv_heads, 2, rows, 1)
    kcol, vcol = dq // dkv, dq // dkv + 1
    cur = lambda b, i: b * nb + i
    prev = lambda b, i: b * nb + jnp.maximum(i - 1, 0)
    return pl.pallas_call(
        functools.partial(_attn_prompt_kernel, kv_heads=kv_heads, hd=hd),
        grid=(nseq, nb),
        in_specs=[
            pl.BlockSpec((WINDOW, dq), lambda b, i: (cur(b, i), 0)),
            pl.BlockSpec((WINDOW, dkv), lambda b, i: (cur(b, i), kcol)),
            pl.BlockSpec((WINDOW, dkv), lambda b, i: (prev(b, i), kcol)),
            pl.BlockSpec((WINDOW, dkv), lambda b, i: (cur(b, i), vcol)),
            pl.BlockSpec((WINDOW, dkv), lambda b, i: (prev(b, i), vcol)),
            pl.BlockSpec((1, kv_heads, 2, rows, WINDOW), lambda b, i: (jnp.minimum(i, 1), 0, 0, 0, 0)),
            pl.BlockSpec((kv_heads, 2, rows, 1), lambda b, i: (0, 0, 0, 0)),
        ],
        out_specs=pl.BlockSpec((WINDOW, dq), lambda b, i: (cur(b, i), 0)),
        out_shape=jax.ShapeDtypeStruct((m, dq), _f32),
        compiler_params=_params("arbitrary", "arbitrary"),
        name="attn_prompt",
    )(qkv, qkv, qkv, qkv, qkv, folded, sink_rows)


def _attn_sample_kernel(qkv_ref, ck_ref, cv_ref, bias_ref, sink_ref, o_ref, ko_ref, vo_ref, *,
                        nb, t, n_heads, kv_heads, hd):
    dq, dkv = n_heads * hd, kv_heads * hd
    group = n_heads // kv_heads
    keep = ck_ref.shape[1]
    keys = bias_ref.shape[0]
    grp = 2 * hd
    scale = hd ** -0.5
    nt = (((1,), (1,)), ((), ()))
    low = lax.broadcasted_iota(jnp.int32, (t, grp), 1) < hd
    blank = jnp.zeros((t, grp), _f32)
    pad = jnp.zeros((keys - keep - t, dkv), _f32)

    def body(n, carry):
        r0 = pl.multiple_of(n * t, t)
        q = qkv_ref[pl.ds(r0, t), 0:dq]
        kn = qkv_ref[pl.ds(r0, t), dq:dq + dkv]
        vn = qkv_ref[pl.ds(r0, t), dq + dkv:dq + 2 * dkv]
        ck, cv = ck_ref[n], cv_ref[n]
        ko_ref[n, 0:keep - t, :] = ck[t:, :]
        ko_ref[n, keep - t:keep, :] = kn
        vo_ref[n, 0:keep - t, :] = cv[t:, :]
        vo_ref[n, keep - t:keep, :] = vn
        kp = jnp.concatenate([ck, kn, pad], axis=0).astype(_bf16)
        vp = jnp.concatenate([cv, vn, pad], axis=0).astype(_bf16)

        rows = []
        for h in range(kv_heads):
            for pair in range(group // 2):
                qv = q[:, (h * (group // 2) + pair) * grp:(h * (group // 2) + pair + 1) * grp]
                swapped = pltpu.roll(qv, hd, axis=1)
                for parity in range(2):
                    x = qv if parity == h % 2 else swapped
                    x = jnp.where(low, x, 0.0) if h % 2 == 0 else jnp.where(low, 0.0, x)
                    pieces = [blank] * (dkv // grp)
                    pieces[h // 2] = x
                    rows.append(jnp.concatenate(pieces, axis=1))
        wt = (jnp.concatenate(rows, axis=0) * scale).astype(_bf16)

        st = lax.dot_general(kp, wt, nt, preferred_element_type=_f32) + bias_ref[...]
        sink = sink_ref[...]
        m = jnp.maximum(jnp.max(st, axis=0, keepdims=True), sink)
        p = jnp.exp(st - m)
        den = jnp.sum(p, axis=0, keepdims=True) + jnp.exp(sink - m)
        pn = p * (1.0 / den)
        of = _dot(pn.T.astype(_bf16), vp)

        outs = []
        for h in range(kv_heads):
            ls = slice((h // 2) * grp, (h // 2 + 1) * grp)
            for pair in range(group // 2):
                r = (h * group + 2 * pair) * t
                even, odd = of[r:r + t, ls], of[r + t:r + 2 * t, ls]
                if h % 2 == 0:
                    outs.append(jnp.where(low, even, pltpu.roll(odd, hd, axis=1)))
                else:
                    outs.append(jnp.where(low, pltpu.roll(even, hd, axis=1), odd))
        o_ref[pl.ds(r0, t), :] = jnp.concatenate(outs, axis=1)
        return carry

    lax.fori_loop(0, nb, body, 0)


def _attn_sample(qkv, cache_k, cache_v, bias, sinks, t, n_heads, kv_heads, hd, nb=8):
    n, keep, dkv = cache_k.shape
    dq = n_heads * hd
    keys = bias.shape[2]
    assert keep == WINDOW and keep + t <= keys and 2 * hd == 128 and kv_heads % 2 == 0
    bias_t = bias[:, :t].transpose(2, 0, 1).reshape(keys, n_heads * t)
    sink_row = jnp.repeat(sinks, t).reshape(1, n_heads * t)
    const = lambda i: (0, 0)
    return pl.pallas_call(
        functools.partial(_attn_sample_kernel, nb=nb, t=t, n_heads=n_heads, kv_heads=kv_heads, hd=hd),
        grid=(n // nb,),
        in_specs=[
            pl.BlockSpec((nb * t, qkv.shape[1]), lambda i: (i, 0)),
            pl.BlockSpec((nb, keep, dkv), lambda i: (i, 0, 0)),
            pl.BlockSpec((nb, keep, dkv), lambda i: (i, 0, 0)),
            pl.BlockSpec((keys, n_heads * t), const),
            pl.BlockSpec((1, n_heads * t), const),
        ],
        out_specs=[
            pl.BlockSpec((nb * t, dq), lambda i: (i, 0)),
            pl.BlockSpec((nb, keep, dkv), lambda i: (i, 0, 0)),
            pl.BlockSpec((nb, keep, dkv), lambda i: (i, 0, 0)),
        ],
        out_shape=[
            jax.ShapeDtypeStruct((n * t, dq), _f32),
            jax.ShapeDtypeStruct((n, keep, dkv), _f32),
            jax.ShapeDtypeStruct((n, keep, dkv), _f32),
        ],
        compiler_params=_params("arbitrary"),
        name="attn_sample",
    )(qkv, cache_k, cache_v, bias_t, sink_row)


def _mix_kernel(attn_ref, conv_ref, gc_ref, ga_ref, x_ref, wa_ref, wo_ref, g_ref, h_ref):
    ao = _dot(attn_ref[...].astype(_bf16), wa_ref[...])
    mixed = gc_ref[...] * conv_ref[...] + ga_ref[...] * ao
    o = _dot(mixed.astype(_bf16), wo_ref[...])
    h_ref[...] = x_ref[...] + _rms(o, g_ref[...])


def _mix(attn, conv_out, gates, x, w_attn, w_out, g_post, tm=256):
    m, d = x.shape
    row = lambda i: (i, 0)
    const = lambda i: (0, 0)
    once = pl.Buffered(1)
    return pl.pallas_call(
        _mix_kernel,
        grid=(m // tm,),
        in_specs=[
            pl.BlockSpec((tm, d), row),
            pl.BlockSpec((tm, d), row),
            pl.BlockSpec((tm, d), lambda i: (i, 0)),
            pl.BlockSpec((tm, d), lambda i: (i, 1)),
            pl.BlockSpec((tm, d), row),
            pl.BlockSpec((d, d), const, pipeline_mode=once),
            pl.BlockSpec((d, d), const, pipeline_mode=once),
            pl.BlockSpec((1, d), const),
        ],
        out_specs=pl.BlockSpec((tm, d), row),
        out_shape=jax.ShapeDtypeStruct((m, d), _f32),
        compiler_params=_params("arbitrary"),
        name="mix",
    )(attn, conv_out, gates, gates, x, w_attn, w_out, g_post)


def _gelu_tanh(x):
    return 0.5 * x * (1.0 + jnp.tanh(math.sqrt(2.0 / math.pi) * (x + 0.044715 * (x * x * x))))


MXU_COLUMNS = 256
FFN_CHUNK = 1024


def _chunks(n, width=MXU_COLUMNS):
    return [slice(a, a + width) for a in range(0, n, width)]


def _ffn_finish(c, nc, acc_ref, h_ref, gp_ref, y_ref):
    @pl.when(c == nc - 1)
    def _():
        y_ref[...] = h_ref[...] + _rms(acc_ref[...], gp_ref[...])


def _ffn_prompt_kernel(h_ref, g_ref, w_ref, kg_ref, kv_ref, bg_ref, bv_ref, wd_ref, gp_ref,
                       y_ref, tg_ref, tv_ref, hn_ref, acc_ref, act_ref, ug_ref, uv_ref, cg_ref, cv_ref, *, tm, nc):
    i, c = pl.program_id(1), pl.program_id(2)
    tc = w_ref.shape[2] // 2

    @pl.when(c == 0)
    def _():
        hn_ref[...] = _rms(h_ref[...], g_ref[...]).astype(_bf16)
        acc_ref[...] = jnp.zeros_like(acc_ref)

    @pl.when(i == 0)
    def _():
        ug_ref[0:8, :] = jnp.zeros((8, tc), _f32)
        uv_ref[0:8, :] = jnp.zeros((8, tc), _f32)

    @pl.when(i > 0)
    def _():
        ug_ref[0:8, :] = cg_ref[c]
        uv_ref[0:8, :] = cv_ref[c]

    def conv(u_ref, col0, k_ref, b_ref, cs):
        u_ref[8:8 + tm, cs] = _dot(hn_ref[...], w_ref[0, :, col0 + cs.start:col0 + cs.stop])
        return (k_ref[2:3, cs] * u_ref[8:8 + tm, cs] + k_ref[1:2, cs] * u_ref[7:7 + tm, cs]
                + k_ref[0:1, cs] * u_ref[6:6 + tm, cs] + b_ref[:, cs])

    for cs in _chunks(tc):
        gate = conv(ug_ref, 0, kg_ref, bg_ref, cs)
        val = conv(uv_ref, tc, kv_ref, bv_ref, cs)
        act_ref[:, cs] = (_gelu_tanh(gate) * val).astype(_bf16)
    acc_ref[...] += _dot(act_ref[...], wd_ref[...])
    cg_ref[c] = ug_ref[tm:tm + 8, :]
    cv_ref[c] = uv_ref[tm:tm + 8, :]
    tg_ref[0, 0] = ug_ref[tm + 6:tm + 8, :]
    tv_ref[0, 0] = uv_ref[tm + 6:tm + 8, :]
    _ffn_finish(c, nc, acc_ref, h_ref, gp_ref, y_ref)


def _ffn_prompt(h, g_pre, w_up, ffn_k, ffn_b, w_down, g_post, nseq, tm=512):
    m, d = h.shape
    dff = w_down.shape[0]
    nc, tc = w_up.shape[0], w_up.shape[2] // 2
    nt = m // nseq // tm
    width = ffn_k.shape[0]
    assert width == 3 and nc * tc == dff
    row = lambda b, i, c: (b * nt + i, 0)
    const = lambda b, i, c: (0, 0)
    lo = lambda b, i, c: (0, c)
    hi = lambda b, i, c: (0, c + nc)
    y, tg, tv = pl.pallas_call(
        functools.partial(_ffn_prompt_kernel, tm=tm, nc=nc),
        grid=(nseq, nt, nc),
        in_specs=[
            pl.BlockSpec((tm, d), row),
            pl.BlockSpec((1, d), const),
            pl.BlockSpec((1, d, 2 * tc), lambda b, i, c: (c, 0, 0)),
            pl.BlockSpec((width, tc), lo),
            pl.BlockSpec((width, tc), hi),
            pl.BlockSpec((1, tc), lo),
            pl.BlockSpec((1, tc), hi),
            pl.BlockSpec((tc, d), lambda b, i, c: (c, 0)),
            pl.BlockSpec((1, d), const),
        ],
        out_specs=[
            pl.BlockSpec((tm, d), row, pipeline_mode=pl.Buffered(1)),
            pl.BlockSpec((1, 1, width - 1, tc), lambda b, i, c: (b, i, 0, c)),
            pl.BlockSpec((1, 1, width - 1, tc), lambda b, i, c: (b, i, 0, c)),
        ],
        out_shape=[
            jax.ShapeDtypeStruct((m, d), _f32),
            jax.ShapeDtypeStruct((nseq, nt, width - 1, dff), _f32),
            jax.ShapeDtypeStruct((nseq, nt, width - 1, dff), _f32),
        ],
        scratch_shapes=[
            pltpu.VMEM((tm, d), _bf16),
            pltpu.VMEM((tm, d), _f32),
            pltpu.VMEM((tm, tc), _bf16),
            pltpu.VMEM((tm + 8, tc), _f32),
            pltpu.VMEM((tm + 8, tc), _f32),
            pltpu.VMEM((nc, 8, tc), _f32),
            pltpu.VMEM((nc, 8, tc), _f32),
        ],
        compiler_params=_params("arbitrary", "arbitrary", "arbitrary"),
        name="ffn_prompt",
    )(h, g_pre, w_up, ffn_k, ffn_k, ffn_b, ffn_b, w_down, g_post)
    return y, jnp.concatenate([tg[:, -1], tv[:, -1]], axis=-1)


def _ffn_sample_kernel(h_ref, g_ref, w_ref, kg_ref, kv_ref, bg_ref, bv_ref, sg_ref, sv_ref, wd_ref, gp_ref,
                       y_ref, tg_ref, tv_ref, hn_ref, acc_ref, act_ref, ug_ref, uv_ref, *, nb, t, nc):
    c = pl.program_id(1)
    tc = w_ref.shape[2] // 2

    @pl.when(c == 0)
    def _():
        hn_ref[...] = _rms(h_ref[...], g_ref[...]).astype(_bf16)
        acc_ref[...] = jnp.zeros_like(acc_ref)

    ug_ref[:, 6:8, :] = sg_ref[...]
    uv_ref[:, 6:8, :] = sv_ref[...]

    def conv(u_ref, col0, k_ref, b_ref, cs):
        width = cs.stop - cs.start
        u_ref[:, 8:8 + t, cs] = _dot(hn_ref[...], w_ref[0, :, col0 + cs.start:col0 + cs.stop]).reshape(nb, t, width)
        cv = (k_ref[2:3, cs] * u_ref[:, 8:8 + t, cs] + k_ref[1:2, cs] * u_ref[:, 7:7 + t, cs]
              + k_ref[0:1, cs] * u_ref[:, 6:6 + t, cs] + b_ref[:, cs])
        return cv.reshape(nb * t, width)

    for cs in _chunks(tc):
        gate = conv(ug_ref, 0, kg_ref, bg_ref, cs)
        val = conv(uv_ref, tc, kv_ref, bv_ref, cs)
        act_ref[:, cs] = (_gelu_tanh(gate) * val).astype(_bf16)
    acc_ref[...] += _dot(act_ref[...], wd_ref[...])
    tg_ref[...] = ug_ref[:, 6 + t:8 + t, :]
    tv_ref[...] = uv_ref[:, 6 + t:8 + t, :]
    _ffn_finish(c, nc, acc_ref, h_ref, gp_ref, y_ref)


def _ffn_sample(h, state, g_pre, w_up, ffn_k, ffn_b, w_down, g_post, t, nb=64):
    m, d = h.shape
    n = m // t
    dff = w_down.shape[0]
    nc, tc = w_up.shape[0], w_up.shape[2] // 2
    width = ffn_k.shape[0]
    assert width == 3 and t >= width - 1 and nc * tc == dff
    tm = nb * t
    row = lambda i, c: (i, 0)
    const = lambda i, c: (0, 0)
    lo = lambda i, c: (0, c)
    hi = lambda i, c: (0, c + nc)
    y, tg, tv = pl.pallas_call(
        functools.partial(_ffn_sample_kernel, nb=nb, t=t, nc=nc),
        grid=(n // nb, nc),
        in_specs=[
            pl.BlockSpec((tm, d), row, pipeline_mode=pl.Buffered(1)),
            pl.BlockSpec((1, d), const),
            pl.BlockSpec((1, d, 2 * tc), lambda i, c: (c, 0, 0)),
            pl.BlockSpec((width, tc), lo),
            pl.BlockSpec((width, tc), hi),
            pl.BlockSpec((1, tc), lo),
            pl.BlockSpec((1, tc), hi),
            pl.BlockSpec((nb, width - 1, tc), lambda i, c: (i, 0, c)),
            pl.BlockSpec((nb, width - 1, tc), lambda i, c: (i, 0, c + nc)),
            pl.BlockSpec((tc, d), lambda i, c: (c, 0)),
            pl.BlockSpec((1, d), const),
        ],
        out_specs=[
            pl.BlockSpec((tm, d), row, pipeline_mode=pl.Buffered(1)),
            pl.BlockSpec((nb, width - 1, tc), lambda i, c: (i, 0, c)),
            pl.BlockSpec((nb, width - 1, tc), lambda i, c: (i, 0, c)),
        ],
        out_shape=[
            jax.ShapeDtypeStruct((m, d), _f32),
            jax.ShapeDtypeStruct((n, width - 1, dff), _f32),
            jax.ShapeDtypeStruct((n, width - 1, dff), _f32),
        ],
        scratch_shapes=[
            pltpu.VMEM((tm, d), _bf16),
            pltpu.VMEM((tm, d), _f32),
            pltpu.VMEM((tm, tc), _bf16),
            pltpu.VMEM((nb, 8 + t, tc), _f32),
            pltpu.VMEM((nb, 8 + t, tc), _f32),
        ],
        compiler_params=_params("arbitrary", "arbitrary"),
        name="ffn_sample",
    )(h, g_pre, w_up, ffn_k, ffn_k, ffn_b, ffn_b, state, state, w_down, g_post)
    return y, jnp.concatenate([tg, tv], axis=-1)


def _rel_bucket(dist):
    max_exact = N_BUCKETS // 2
    dd = jnp.maximum(dist, 1).astype(_f32)
    large = max_exact + (jnp.log(dd / max_exact) / math.log(MAX_DISTANCE / max_exact)
                         * (N_BUCKETS - max_exact)).astype(jnp.int32)
    large = jnp.minimum(large, N_BUCKETS - 1)
    return jnp.where(dist < max_exact, dist, large)


def _bias_table(rel_bias):
    n = WINDOW
    by_dist = rel_bias[_rel_bucket(jnp.arange(n))].astype(_f32).T
    g = jnp.concatenate([jnp.full_like(by_dist, MASKED), by_dist[:, ::-1],
                         jnp.full_like(by_dist, MASKED)], axis=1)
    heads, length = g.shape
    skew = jnp.broadcast_to(g[:, None, :], (heads, n, length)).reshape(heads, n * length)
    skew = skew[:, :n * (length - 1)].reshape(heads, n, length - 1)
    return skew[:, :, n - 1:3 * n - 1]


def _row(v):
    return v.reshape(1, -1)


def _layer(xp, xs, cache_k, cache_v, state_conv, layer, st_ffn, p):
    (g_pre, w_in, b_in, conv_k, conv_b, ln_g, ln_b, w_conv, sinks, rel_bias, w_attn, w_out, g_post,
     g_ffn_pre, w_up, ffn_k, ffn_b, w_down, g_ffn_post) = p
    nseq, seq, d = xp.shape
    ns, t, _ = xs.shape
    n_heads = sinks.shape[0]
    keep, kv_heads, hd = cache_k.shape[1:]
    group = n_heads // kv_heads
    dq, dkv = n_heads * hd, kv_heads * hd
    d_conv = conv_k.shape[1]
    assert d_conv == d and dq == d and keep == WINDOW and seq % WINDOW == 0

    w_conv_b, w_attn_b, w_out_b, w_down_b = (w.astype(_bf16) for w in (w_conv, w_attn, w_out, w_down))
    w_in_b = _column_tiles(w_in, 512)
    w_up_b = _column_tiles(w_up, min(FFN_CHUNK, w_down.shape[0]), groups=2)
    g_pre, b_in, conv_b, ln_g, ln_b, g_post, g_ffn_pre, ffn_b, g_ffn_post = map(
        _row, (g_pre, b_in, conv_b, ln_g, ln_b, g_post, g_ffn_pre, ffn_b, g_ffn_post))

    bias = _bias_table(rel_bias)

    outs = []
    for x3, is_prompt in ((xp, True), (xs, False)):
        n = x3.shape[0]
        x = x3.reshape(-1, d)
        glu, qkv, gates = _inproj(x, g_pre, w_in_b, b_in, d_conv, dq + 2 * dkv, 2 * d)
        if is_prompt:
            conv_out = _conv_prompt(glu, conv_k, conv_b, ln_g, ln_b, w_conv_b, nseq)
            attn = _attn_prompt(qkv, bias, sinks.astype(_f32), nseq, n_heads, kv_heads, hd)
            new_conv = glu.reshape(n, seq, d)[:, seq - (conv_k.shape[0] - 1):]
            last = qkv.reshape(n, seq, -1)[:, seq - keep:]
            new_k = last[:, :, dq:dq + dkv].reshape(n, keep, kv_heads, hd)
            new_v = last[:, :, dq + dkv:].reshape(n, keep, kv_heads, hd)
        else:
            conv_out, new_conv = _conv_sample(state_conv, layer, glu, conv_k, conv_b, ln_g, ln_b, w_conv_b)
            new_conv = new_conv[0]
            attn, new_k, new_v = _attn_sample(qkv, cache_k.reshape(n, keep, dkv), cache_v.reshape(n, keep, dkv), bias,
                                              sinks.astype(_f32), t, n_heads, kv_heads, hd)
            new_k, new_v = (a.reshape(n, keep, kv_heads, hd) for a in (new_k, new_v))
        h = _mix(attn, conv_out, gates, x, w_attn_b, w_out_b, g_post)
        if is_prompt:
            y, new_ffn = _ffn_prompt(h, g_ffn_pre, w_up_b, ffn_k, ffn_b, w_down_b, g_ffn_post, nseq)
        else:
            y, new_ffn = _ffn_sample(h, st_ffn, g_ffn_pre, w_up_b, ffn_k, ffn_b, w_down_b, g_ffn_post, t)
        outs.append((y.reshape(x3.shape), new_k, new_v, new_conv, new_ffn))
    return outs


def kernel(x_prompt, x_sample, cache_k, cache_v, state_conv, state_ffn_conv, norm_mix_pre, w_in, b_in, conv_dw_k, conv_dw_b, conv_ln_g, conv_ln_b, w_conv_proj, attn_sinks, rel_bias, w_attn_proj, w_out, norm_mix_post, norm_ffn_pre, w_up, ffn_dw_k, ffn_dw_b, w_down, norm_ffn_post):
    y_p, y_s = x_prompt, x_sample
    per_layer = []
    for l in range(w_in.shape[0]):
        p = (norm_mix_pre[l], w_in[l], b_in[l], conv_dw_k[l], conv_dw_b[l], conv_ln_g[l], conv_ln_b[l], w_conv_proj[l],
             attn_sinks[l], rel_bias, w_attn_proj[l], w_out[l], norm_mix_post[l],
             norm_ffn_pre[l], w_up[l], ffn_dw_k[l], ffn_dw_b[l], w_down[l], norm_ffn_post[l])
        (y_p, *rest_p), (y_s, *rest_s) = _layer(y_p, y_s, cache_k[l], cache_v[l], state_conv, l, state_ffn_conv[l], p)
        per_layer.append(rest_p + rest_s)
    stacked = [jnp.stack(leaves) for leaves in zip(*per_layer)]
    return (y_p, y_s, *stacked)
```

```python
import functools
import math

import jax
import jax.numpy as jnp
from jax import lax
from jax.experimental import pallas as pl
from jax.experimental.pallas import tpu as pltpu

EPS = 1e-6
WINDOW = 128
N_BUCKETS = 32
MAX_DISTANCE = 128
MASKED = -1e30
VMEM_LIMIT_BYTES = 60 * 1024 * 1024
HALO = 32

_bf16 = jnp.bfloat16
_f32 = jnp.float32


def _params(*sem):
    return pltpu.CompilerParams(dimension_semantics=sem, vmem_limit_bytes=VMEM_LIMIT_BYTES)


def _dot(a, b):
    return jnp.dot(a, b, preferred_element_type=_f32)


def _sigmoid(x):
    return 1.0 / (1.0 + jnp.exp(-x))


def _rms(x, g):
    return x * lax.rsqrt(jnp.mean(x * x, axis=-1, keepdims=True) + EPS) * g


def _inproj_kernel(x_ref, g_ref, wa_ref, wb_ref, ba_ref, bb_ref, glu_ref, qkv_ref, gate_ref, xn_ref, *, n_glu, n_qkv):
    j = pl.program_id(1)

    @pl.when(j == 0)
    def _():
        xn_ref[...] = _rms(x_ref[...], g_ref[...]).astype(_bf16)

    half = wb_ref.shape[2] // 2
    halves = [slice(0, half), slice(half, 2 * half)]

    def u(w_ref, b_ref, cs):
        return _dot(xn_ref[...], w_ref[0, :, cs]) + b_ref[:, cs]

    @pl.when(j < n_glu)
    def _():
        for cs in halves:
            glu_ref[:, cs] = u(wa_ref, ba_ref, cs) * _sigmoid(u(wb_ref, bb_ref, cs))

    @pl.when((j >= n_glu) & (j < n_glu + n_qkv))
    def _():
        for cs in halves:
            qkv_ref[:, cs] = u(wb_ref, bb_ref, cs)

    @pl.when(j >= n_glu + n_qkv)
    def _():
        for cs in halves:
            gate_ref[:, cs] = _sigmoid(u(wb_ref, bb_ref, cs))


def _cast_tile_kernel(w_ref, o_ref):
    o_ref[0] = w_ref[...].astype(_bf16)


def _column_tiles(w, tn, groups=1):
    d, n = w.shape
    nt = n // groups // tn
    return pl.pallas_call(
        _cast_tile_kernel,
        grid=(nt, groups),
        in_specs=[pl.BlockSpec((d, tn), lambda c, g: (0, g * nt + c))],
        out_specs=pl.BlockSpec((1, d, tn), lambda c, g: (c, 0, g)),
        out_shape=jax.ShapeDtypeStruct((nt, d, groups * tn), _bf16),
        compiler_params=_params("arbitrary", "arbitrary"),
        name="weight_tiles",
    )(w)


def _inproj(x, g, w, b, d_glu, d_qkv, d_gate, tm=1024):
    m, d = x.shape
    tm = min(tm, m)
    tn = w.shape[2]
    n_glu, n_qkv, n_gate = d_glu // tn, d_qkv // tn, d_gate // tn
    assert w.shape[0] == 2 * n_glu + n_qkv + n_gate
    first_w = lambda i, j: (jnp.minimum(j, n_glu - 1), 0, 0)
    rest_w = lambda i, j: (j + n_glu, 0, 0)
    first = lambda i, j: (0, jnp.minimum(j, n_glu - 1))
    rest = lambda i, j: (0, j + n_glu)
    return pl.pallas_call(
        functools.partial(_inproj_kernel, n_glu=n_glu, n_qkv=n_qkv),
        grid=(m // tm, n_glu + n_qkv + n_gate),
        in_specs=[
            pl.BlockSpec((tm, d), lambda i, j: (i, 0)),
            pl.BlockSpec((1, d), lambda i, j: (0, 0)),
            pl.BlockSpec((1, d, tn), first_w),
            pl.BlockSpec((1, d, tn), rest_w),
            pl.BlockSpec((1, tn), first),
            pl.BlockSpec((1, tn), rest),
        ],
        out_specs=[
            pl.BlockSpec((tm, tn), lambda i, j: (i, jnp.minimum(j, n_glu - 1))),
            pl.BlockSpec((tm, tn), lambda i, j: (i, jnp.clip(j - n_glu, 0, n_qkv - 1))),
            pl.BlockSpec((tm, tn), lambda i, j: (i, jnp.maximum(j - n_glu - n_qkv, 0))),
        ],
        out_shape=[
            jax.ShapeDtypeStruct((m, d_glu), _f32),
            jax.ShapeDtypeStruct((m, d_qkv), _f32),
            jax.ShapeDtypeStruct((m, d_gate), _f32),
        ],
        scratch_shapes=[pltpu.VMEM((tm, d), _bf16)],
        compiler_params=_params("arbitrary", "arbitrary"),
        name="inproj",
    )(x, g, w, w, b, b)


def _ln_silu_proj(conv_ref, act_ref, lg_ref, lb_ref, w_ref, o_ref, rows):
    rt = 16

    def body(r, carry):
        r0 = pl.multiple_of(r * rt, rt)
        c = conv_ref[pl.ds(r0, rt), :]
        mu = jnp.mean(c, axis=-1, keepdims=True)
        cc = c - mu
        var = jnp.mean(cc * cc, axis=-1, keepdims=True)
        y = cc * lax.rsqrt(var + EPS) * lg_ref[...] + lb_ref[...]
        act_ref[pl.ds(r0, rt), :] = (y * _sigmoid(y)).astype(_bf16)
        return carry

    lax.fori_loop(0, rows // rt, body, 0, unroll=4)
    o_ref[...] = _dot(act_ref[...], w_ref[...])


def _conv_prompt_kernel(glu_ref, k_ref, cb_ref, lg_ref, lb_ref, w_ref, o_ref, ext_ref, conv_ref, act_ref, *, tm, width):
    d = glu_ref.shape[1]
    i = pl.program_id(1)

    @pl.when(i == 0)
    def _():
        ext_ref[0:HALO, :] = jnp.zeros((HALO, d), _f32)

    @pl.when(i > 0)
    def _():
        ext_ref[0:HALO, :] = ext_ref[tm:tm + HALO, :]

    ext_ref[HALO:HALO + tm, :] = glu_ref[...]

    rt, ct, sub = 64, 128, 8
    first = HALO - (width - 1)

    def body(r, carry):
        r0 = pl.multiple_of(r * rt, rt)
        for c in range(d // ct):
            cs = slice(c * ct, (c + 1) * ct)
            blk = ext_ref[pl.ds(r0, rt + HALO), cs]
            acc = jnp.broadcast_to(cb_ref[:, cs], (rt, ct))
            for phase in range(sub):
                taps = [w for w in range(width) if (first + w) % sub == phase]
                rows = rt if phase == 0 else rt + sub
                part = None
                for w in taps:
                    lo = first + w - phase
                    term = k_ref[w:w + 1, cs] * blk[lo:lo + rows, :]
                    part = term if part is None else part + term
                acc = acc + part[phase:phase + rt, :]
            conv_ref[pl.ds(r0, rt), cs] = acc
        return carry

    lax.fori_loop(0, tm // rt, body, 0)
    _ln_silu_proj(conv_ref, act_ref, lg_ref, lb_ref, w_ref, o_ref, tm)


def _conv_prompt(glu, conv_k, conv_b, ln_g, ln_b, w, nseq, tm=256):
    m, d = glu.shape
    nt = m // nseq // tm
    width = conv_k.shape[0]
    const = lambda b, i: (0, 0)
    return pl.pallas_call(
        functools.partial(_conv_prompt_kernel, tm=tm, width=width),
        grid=(nseq, nt),
        in_specs=[
            pl.BlockSpec((tm, d), lambda b, i: (b * nt + i, 0)),
            pl.BlockSpec((width, d), const),
            pl.BlockSpec((1, d), const),
            pl.BlockSpec((1, d), const),
            pl.BlockSpec((1, d), const),
            pl.BlockSpec((d, d), const),
        ],
        out_specs=pl.BlockSpec((tm, d), lambda b, i: (b * nt + i, 0)),
        out_shape=jax.ShapeDtypeStruct((m, d), _f32),
        scratch_shapes=[pltpu.VMEM((tm + HALO, d), _f32), pltpu.VMEM((tm, d), _f32), pltpu.VMEM((tm, d), _bf16)],
        compiler_params=_params("arbitrary", "arbitrary"),
        name="conv_prompt",
    )(glu, conv_k, conv_b, ln_g, ln_b, w)


def _conv_sample_kernel(st_ref, glu_ref, k_ref, cb_ref, lg_ref, lb_ref, w_ref, o_ref, ns_ref,
                        ext_ref, conv_ref, act_ref, *, nb, t, width):
    d = glu_ref.shape[1]
    hist = width - 1
    ct = 512

    def body(n, carry):
        r0 = pl.multiple_of(n * t, t)
        ext_ref[0:hist, :] = st_ref[0, n]
        ext_ref[hist:hist + t, :] = glu_ref[pl.ds(r0, t), :]
        ns_ref[0, n] = ext_ref[t:t + hist, :]
        for c in range(d // ct):
            cs = slice(c * ct, (c + 1) * ct)
            blk = ext_ref[0:hist + t, cs]
            acc = jnp.broadcast_to(cb_ref[:, cs], (t, ct))
            for w in range(width):
                acc = acc + k_ref[w:w + 1, cs] * blk[w:w + t, :]
            conv_ref[pl.ds(r0, t), cs] = acc
        return carry

    lax.fori_loop(0, nb, body, 0)
    _ln_silu_proj(conv_ref, act_ref, lg_ref, lb_ref, w_ref, o_ref, nb * t)


def _conv_sample(state, layer, glu, conv_k, conv_b, ln_g, ln_b, w, nb=32):
    _, n, hist, d = state.shape
    width = conv_k.shape[0]
    t = glu.shape[0] // n
    assert hist == width - 1 and t <= hist
    const = lambda i: (0, 0)
    return pl.pallas_call(
        functools.partial(_conv_sample_kernel, nb=nb, t=t, width=width),
        grid=(n // nb,),
        in_specs=[
            pl.BlockSpec((1, nb, hist, d), lambda i: (layer, i, 0, 0)),
            pl.BlockSpec((nb * t, d), lambda i: (i, 0)),
            pl.BlockSpec((width, d), const),
            pl.BlockSpec((1, d), const),
            pl.BlockSpec((1, d), const),
            pl.BlockSpec((1, d), const),
            pl.BlockSpec((d, d), const),
        ],
        out_specs=[
            pl.BlockSpec((nb * t, d), lambda i: (i, 0)),
            pl.BlockSpec((1, nb, hist, d), lambda i: (0, i, 0, 0)),
        ],
        out_shape=[
            jax.ShapeDtypeStruct((n * t, d), _f32),
            jax.ShapeDtypeStruct((1, n, hist, d), _f32),
        ],
        scratch_shapes=[pltpu.VMEM((hist + t + (-(hist + t)) % 8, d), _f32), pltpu.VMEM((nb * t, d), _f32),
                        pltpu.VMEM((nb * t, d), _bf16)],
        compiler_params=_params("arbitrary"),
        name="conv_sample",
    )(state, glu, conv_k, conv_b, ln_g, ln_b, w)


def _attn_prompt_kernel(q_ref, kc_ref, kp_ref, vc_ref, vp_ref, bias_ref, sink_ref, o_ref, *, kv_heads, hd):
    blk = q_ref.shape[0]
    grp = 2 * hd
    pairs = q_ref.shape[1] // (kv_heads * grp)
    cols = pairs * blk
    scale = hd ** -0.5
    nt = (((1,), (1,)), ((), ()))
    low = lax.broadcasted_iota(jnp.int32, (2 * blk, grp), 1) < hd
    qi = lax.broadcasted_iota(jnp.int32, (blk, cols), 1) & (blk - 1)
    cur_visible = lax.broadcasted_iota(jnp.int32, (blk, cols), 0) <= qi

    def halves(p_ref, c_ref, h):
        g0 = (h // 2) * grp
        x = jnp.concatenate([p_ref[:, g0:g0 + grp], c_ref[:, g0:g0 + grp]], axis=0)
        swapped = pltpu.roll(x, hd, axis=1)
        in_low, in_high = (x, swapped) if h % 2 == 0 else (swapped, x)
        return jnp.where(low, in_low, 0.0), jnp.where(low, 0.0, in_high)

    for h in range(kv_heads):
        ks = [x.astype(_bf16) for x in halves(kp_ref, kc_ref, h)]
        vts = [x.T.astype(_bf16) for x in halves(vp_ref, vc_ref, h)]
        lanes = [slice((h * pairs + p) * grp, (h * pairs + p + 1) * grp) for p in range(pairs)]
        qs = (jnp.concatenate([q_ref[:, ls] for ls in lanes], axis=0) * scale).astype(_bf16)
        out_t = None
        for parity in range(2):
            s2 = lax.dot_general(ks[parity], qs, nt, preferred_element_type=_f32)
            s = jnp.where(cur_visible, s2[blk:, :], s2[:blk, :]) + bias_ref[0, h, parity]
            sink = sink_ref[h, parity]
            m = jnp.maximum(jnp.max(s, axis=0, keepdims=True), sink)
            p = jnp.exp(s - m)
            den = jnp.sum(p, axis=0, keepdims=True) + jnp.exp(sink - m)
            pn = p * (1.0 / den)
            p2 = jnp.concatenate([jnp.where(cur_visible, 0.0, pn), jnp.where(cur_visible, pn, 0.0)], axis=0)
            o_t = _dot(vts[parity], p2.astype(_bf16))
            out_t = o_t if out_t is None else out_t + o_t
        out = out_t.T
        for p, ls in enumerate(lanes):
            o_ref[:, ls] = out[p * blk:(p + 1) * blk, :]


def _attn_prompt(qkv, bias, sinks, nseq, n_heads, kv_heads, hd):
    m = qkv.shape[0]
    dq, dkv = n_heads * hd, kv_heads * hd
    nb = m // nseq // WINDOW
    pairs = n_heads // kv_heads // 2
    rows = pairs * WINDOW
    assert WINDOW & (WINDOW - 1) == 0 and 2 * hd == 128 and kv_heads % 2 == 0
    qi = jnp.arange(WINDOW)[:, None]
    cur_visible = jnp.arange(WINDOW)[None, :] <= qi
    folded = jnp.stack([jnp.where(cur_visible, bias[:, :, WINDOW:], MASKED),
                        jnp.where(cur_visible, bias[:, :, WINDOW:], bias[:, :, :WINDOW])])
    folded = folded.reshape(2, kv_heads, pairs, 2, WINDOW, WINDOW).transpose(0, 1, 3, 5, 2, 4)
    folded = folded.reshape(2, kv_heads, 2, WINDOW, rows)
    sink_rows = jnp.repeat(sinks.reshape(kv_heads, pairs, 2).transpose(0, 2, 1), WINDOW, axis=-1)
    sink_rows = sink_rows.reshape(kv_heads, 2, 1, rows)
    kcol, vcol = dq // dkv, dq // dkv + 1
    cur = lambda b, i: b * nb + i
    prev = lambda b, i: b * nb + jnp.maximum(i - 1, 0)
    return pl.pallas_call(
        functools.partial(_attn_prompt_kernel, kv_heads=kv_heads, hd=hd),
        grid=(nseq, nb),
        in_specs=[
            pl.BlockSpec((WINDOW, dq), lambda b, i: (cur(b, i), 0)),
            pl.BlockSpec((WINDOW, dkv), lambda b, i: (cur(b, i), kcol)),
            pl.BlockSpec((WINDOW, dkv), lambda b, i: (prev(b, i), kcol)),
            pl.BlockSpec((WINDOW, dkv), lambda b, i: (cur(b, i), vcol)),
            pl.BlockSpec((WINDOW, dkv), lambda b, i: (prev(b, i), vcol)),
            pl.BlockSpec((1, kv_heads, 2, WINDOW, rows), lambda b, i: (jnp.minimum(i, 1), 0, 0, 0, 0)),
            pl.BlockSpec((kv_heads, 2, 1, rows), lambda b, i: (0, 0, 0, 0)),
        ],
        out_specs=pl.BlockSpec((WINDOW, dq), lambda b, i: (cur(b, i), 0)),
        out_shape=jax.ShapeDtypeStruct((m, dq), _f32),
        compiler_params=_params("arbitrary", "arbitrary"),
        name="attn_prompt",
    )(qkv, qkv, qkv, qkv, qkv, folded, sink_rows)


def _attn_sample_kernel(qkv_ref, ck_ref, cv_ref, bias_ref, sink_ref, o_ref, ko_ref, vo_ref, *,
                        nb, t, n_heads, kv_heads, hd):
    dq, dkv = n_heads * hd, kv_heads * hd
    group = n_heads // kv_heads
    keep = ck_ref.shape[1]
    keys = bias_ref.shape[0]
    grp = 2 * hd
    scale = hd ** -0.5
    nt = (((1,), (1,)), ((), ()))
    low = lax.broadcasted_iota(jnp.int32, (t, grp), 1) < hd
    blank = jnp.zeros((t, grp), _f32)
    pad = jnp.zeros((keys - keep - t, dkv), _f32)

    def body(n, carry):
        r0 = pl.multiple_of(n * t, t)
        q = qkv_ref[pl.ds(r0, t), 0:dq]
        kn = qkv_ref[pl.ds(r0, t), dq:dq + dkv]
        vn = qkv_ref[pl.ds(r0, t), dq + dkv:dq + 2 * dkv]
        ck, cv = ck_ref[n], cv_ref[n]
        ko_ref[n, 0:keep - t, :] = ck[t:, :]
        ko_ref[n, keep - t:keep, :] = kn
        vo_ref[n, 0:keep - t, :] = cv[t:, :]
        vo_ref[n, keep - t:keep, :] = vn
        kp = jnp.concatenate([ck, kn, pad], axis=0).astype(_bf16)
        vp = jnp.concatenate([cv, vn, pad], axis=0).astype(_bf16)

        rows = []
        for h in range(kv_heads):
            for pair in range(group // 2):
                qv = q[:, (h * (group // 2) + pair) * grp:(h * (group // 2) + pair + 1) * grp]
                swapped = pltpu.roll(qv, hd, axis=1)
                for parity in range(2):
                    x = qv if parity == h % 2 else swapped
                    x = jnp.where(low, x, 0.0) if h % 2 == 0 else jnp.where(low, 0.0, x)
                    pieces = [blank] * (dkv // grp)
                    pieces[h // 2] = x
                    rows.append(jnp.concatenate(pieces, axis=1))
        wt = (jnp.concatenate(rows, axis=0) * scale).astype(_bf16)

        st = lax.dot_general(kp, wt, nt, preferred_element_type=_f32) + bias_ref[...]
        sink = sink_ref[...]
        m = jnp.maximum(jnp.max(st, axis=0, keepdims=True), sink)
        p = jnp.exp(st - m)
        den = jnp.sum(p, axis=0, keepdims=True) + jnp.exp(sink - m)
        pn = p * (1.0 / den)
        of = _dot(pn.T.astype(_bf16), vp)

        outs = []
        for h in range(kv_heads):
            ls = slice((h // 2) * grp, (h // 2 + 1) * grp)
            for pair in range(group // 2):
                r = (h * group + 2 * pair) * t
                even, odd = of[r:r + t, ls], of[r + t:r + 2 * t, ls]
                if h % 2 == 0:
                    outs.append(jnp.where(low, even, pltpu.roll(odd, hd, axis=1)))
                else:
                    outs.append(jnp.where(low, pltpu.roll(even, hd, axis=1), odd))
        o_ref[pl.ds(r0, t), :] = jnp.concatenate(outs, axis=1)
        return carry

    lax.fori_loop(0, nb, body, 0)


def _attn_sample(qkv, cache_k, cache_v, bias, sinks, t, n_heads, kv_heads, hd, nb=8):
    n, keep, dkv = cache_k.shape
    dq = n_heads * hd
    keys = bias.shape[2]
    assert keep == WINDOW and keep + t <= keys and 2 * hd == 128 and kv_heads % 2 == 0
    bias_t = bias[:, :t].transpose(2, 0, 1).reshape(keys, n_heads * t)
    sink_row = jnp.repeat(sinks, t).reshape(1, n_heads * t)
    const = lambda i: (0, 0)
    return pl.pallas_call(
        functools.partial(_attn_sample_kernel, nb=nb, t=t, n_heads=n_heads, kv_heads=kv_heads, hd=hd),
        grid=(n // nb,),
        in_specs=[
            pl.BlockSpec((nb * t, qkv.shape[1]), lambda i: (i, 0)),
            pl.BlockSpec((nb, keep, dkv), lambda i: (i, 0, 0)),
            pl.BlockSpec((nb, keep, dkv), lambda i: (i, 0, 0)),
            pl.BlockSpec((keys, n_heads * t), const),
            pl.BlockSpec((1, n_heads * t), const),
        ],
        out_specs=[
            pl.BlockSpec((nb * t, dq), lambda i: (i, 0)),
            pl.BlockSpec((nb, keep, dkv), lambda i: (i, 0, 0)),
            pl.BlockSpec((nb, keep, dkv), lambda i: (i, 0, 0)),
        ],
        out_shape=[
            jax.ShapeDtypeStruct((n * t, dq), _f32),
            jax.ShapeDtypeStruct((n, keep, dkv), _f32),
            jax.ShapeDtypeStruct((n, keep, dkv), _f32),
        ],
        compiler_params=_params("arbitrary"),
        name="attn_sample",
    )(qkv, cache_k, cache_v, bias_t, sink_row)


def _mix_kernel(attn_ref, conv_ref, gc_ref, ga_ref, x_ref, wa_ref, wo_ref, g_ref, h_ref):
    ao = _dot(attn_ref[...].astype(_bf16), wa_ref[...])
    mixed = gc_ref[...] * conv_ref[...] + ga_ref[...] * ao
    o = _dot(mixed.astype(_bf16), wo_ref[...])
    h_ref[...] = x_ref[...] + _rms(o, g_ref[...])


def _mix(attn, conv_out, gates, x, w_attn, w_out, g_post, tm=256):
    m, d = x.shape
    row = lambda i: (i, 0)
    const = lambda i: (0, 0)
    once = pl.Buffered(1)
    return pl.pallas_call(
        _mix_kernel,
        grid=(m // tm,),
        in_specs=[
            pl.BlockSpec((tm, d), row),
            pl.BlockSpec((tm, d), row),
            pl.BlockSpec((tm, d), lambda i: (i, 0)),
            pl.BlockSpec((tm, d), lambda i: (i, 1)),
            pl.BlockSpec((tm, d), row),
            pl.BlockSpec((d, d), const, pipeline_mode=once),
            pl.BlockSpec((d, d), const, pipeline_mode=once),
            pl.BlockSpec((1, d), const),
        ],
        out_specs=pl.BlockSpec((tm, d), row),
        out_shape=jax.ShapeDtypeStruct((m, d), _f32),
        compiler_params=_params("arbitrary"),
        name="mix",
    )(attn, conv_out, gates, gates, x, w_attn, w_out, g_post)


def _gelu_tanh(x):
    return 0.5 * x * (1.0 + jnp.tanh(math.sqrt(2.0 / math.pi) * (x + 0.044715 * (x * x * x))))


MXU_COLUMNS = 256
FFN_CHUNK = 1024


def _chunks(n, width=MXU_COLUMNS):
    return [slice(a, a + width) for a in range(0, n, width)]


def _ffn_finish(c, nc, acc_ref, h_ref, gp_ref, y_ref):
    @pl.when(c == nc - 1)
    def _():
        y_ref[...] = h_ref[...] + _rms(acc_ref[...], gp_ref[...])


def _ffn_prompt_kernel(h_ref, g_ref, w_ref, kg_ref, kv_ref, bg_ref, bv_ref, wd_ref, gp_ref,
                       y_ref, tg_ref, tv_ref, hn_ref, acc_ref, act_ref, ug_ref, uv_ref, cg_ref, cv_ref, *, tm, nc):
    i, c = pl.program_id(1), pl.program_id(2)
    tc = w_ref.shape[2] // 2

    @pl.when(c == 0)
    def _():
        hn_ref[...] = _rms(h_ref[...], g_ref[...]).astype(_bf16)
        acc_ref[...] = jnp.zeros_like(acc_ref)

    @pl.when(i == 0)
    def _():
        ug_ref[0:8, :] = jnp.zeros((8, tc), _f32)
        uv_ref[0:8, :] = jnp.zeros((8, tc), _f32)

    @pl.when(i > 0)
    def _():
        ug_ref[0:8, :] = cg_ref[c]
        uv_ref[0:8, :] = cv_ref[c]

    def conv(u_ref, col0, k_ref, b_ref, cs):
        u_ref[8:8 + tm, cs] = _dot(hn_ref[...], w_ref[0, :, col0 + cs.start:col0 + cs.stop])
        return (k_ref[2:3, cs] * u_ref[8:8 + tm, cs] + k_ref[1:2, cs] * u_ref[7:7 + tm, cs]
                + k_ref[0:1, cs] * u_ref[6:6 + tm, cs] + b_ref[:, cs])

    for cs in _chunks(tc):
        gate = conv(ug_ref, 0, kg_ref, bg_ref, cs)
        val = conv(uv_ref, tc, kv_ref, bv_ref, cs)
        act_ref[:, cs] = (_gelu_tanh(gate) * val).astype(_bf16)
    acc_ref[...] += _dot(act_ref[...], wd_ref[...])
    cg_ref[c] = ug_ref[tm:tm + 8, :]
    cv_ref[c] = uv_ref[tm:tm + 8, :]
    tg_ref[0, 0] = ug_ref[tm + 6:tm + 8, :]
    tv_ref[0, 0] = uv_ref[tm + 6:tm + 8, :]
    _ffn_finish(c, nc, acc_ref, h_ref, gp_ref, y_ref)


def _ffn_prompt(h, g_pre, w_up, ffn_k, ffn_b, w_down, g_post, nseq, tm=512):
    m, d = h.shape
    dff = w_down.shape[0]
    nc, tc = w_up.shape[0], w_up.shape[2] // 2
    nt = m // nseq // tm
    width = ffn_k.shape[0]
    assert width == 3 and nc * tc == dff
    row = lambda b, i, c: (b * nt + i, 0)
    const = lambda b, i, c: (0, 0)
    lo = lambda b, i, c: (0, c)
    hi = lambda b, i, c: (0, c + nc)
    y, tg, tv = pl.pallas_call(
        functools.partial(_ffn_prompt_kernel, tm=tm, nc=nc),
        grid=(nseq, nt, nc),
        in_specs=[
            pl.BlockSpec((tm, d), row),
            pl.BlockSpec((1, d), const),
            pl.BlockSpec((1, d, 2 * tc), lambda b, i, c: (c, 0, 0)),
            pl.BlockSpec((width, tc), lo),
            pl.BlockSpec((width, tc), hi),
            pl.BlockSpec((1, tc), lo),
            pl.BlockSpec((1, tc), hi),
            pl.BlockSpec((tc, d), lambda b, i, c: (c, 0)),
            pl.BlockSpec((1, d), const),
        ],
        out_specs=[
            pl.BlockSpec((tm, d), row, pipeline_mode=pl.Buffered(1)),
            pl.BlockSpec((1, 1, width - 1, tc), lambda b, i, c: (b, i, 0, c)),
            pl.BlockSpec((1, 1, width - 1, tc), lambda b, i, c: (b, i, 0, c)),
        ],
        out_shape=[
            jax.ShapeDtypeStruct((m, d), _f32),
            jax.ShapeDtypeStruct((nseq, nt, width - 1, dff), _f32),
            jax.ShapeDtypeStruct((nseq, nt, width - 1, dff), _f32),
        ],
        scratch_shapes=[
            pltpu.VMEM((tm, d), _bf16),
            pltpu.VMEM((tm, d), _f32),
            pltpu.VMEM((tm, tc), _bf16),
            pltpu.VMEM((tm + 8, tc), _f32),
            pltpu.VMEM((tm + 8, tc), _f32),
            pltpu.VMEM((nc, 8, tc), _f32),
            pltpu.VMEM((nc, 8, tc), _f32),
        ],
        compiler_params=_params("arbitrary", "arbitrary", "arbitrary"),
        name="ffn_prompt",
    )(h, g_pre, w_up, ffn_k, ffn_k, ffn_b, ffn_b, w_down, g_post)
    return y, jnp.concatenate([tg[:, -1], tv[:, -1]], axis=-1)


def _ffn_sample_kernel(h_ref, g_ref, w_ref, kg_ref, kv_ref, bg_ref, bv_ref, sg_ref, sv_ref, wd_ref, gp_ref,
                       y_ref, tg_ref, tv_ref, hn_ref, acc_ref, act_ref, ug_ref, uv_ref, *, nb, t, nc):
    c = pl.program_id(1)
    tc = w_ref.shape[2] // 2

    @pl.when(c == 0)
    def _():
        hn_ref[...] = _rms(h_ref[...], g_ref[...]).astype(_bf16)
        acc_ref[...] = jnp.zeros_like(acc_ref)

    ug_ref[:, 6:8, :] = sg_ref[...]
    uv_ref[:, 6:8, :] = sv_ref[...]

    def conv(u_ref, col0, k_ref, b_ref, cs):
        width = cs.stop - cs.start
        u_ref[:, 8:8 + t, cs] = _dot(hn_ref[...], w_ref[0, :, col0 + cs.start:col0 + cs.stop]).reshape(nb, t, width)
        cv = (k_ref[2:3, cs] * u_ref[:, 8:8 + t, cs] + k_ref[1:2, cs] * u_ref[:, 7:7 + t, cs]
              + k_ref[0:1, cs] * u_ref[:, 6:6 + t, cs] + b_ref[:, cs])
        return cv.reshape(nb * t, width)

    for cs in _chunks(tc):
        gate = conv(ug_ref, 0, kg_ref, bg_ref, cs)
        val = conv(uv_ref, tc, kv_ref, bv_ref, cs)
        act_ref[:, cs] = (_gelu_tanh(gate) * val).astype(_bf16)
    acc_ref[...] += _dot(act_ref[...], wd_ref[...])
    tg_ref[...] = ug_ref[:, 6 + t:8 + t, :]
    tv_ref[...] = uv_ref[:, 6 + t:8 + t, :]
    _ffn_finish(c, nc, acc_ref, h_ref, gp_ref, y_ref)


def _ffn_sample(h, state, g_pre, w_up, ffn_k, ffn_b, w_down, g_post, t, nb=64):
    m, d = h.shape
    n = m // t
    dff = w_down.shape[0]
    nc, tc = w_up.shape[0], w_up.shape[2] // 2
    width = ffn_k.shape[0]
    assert width == 3 and t >= width - 1 and nc * tc == dff
    tm = nb * t
    row = lambda i, c: (i, 0)
    const = lambda i, c: (0, 0)
    lo = lambda i, c: (0, c)
    hi = lambda i, c: (0, c + nc)
    y, tg, tv = pl.pallas_call(
        functools.partial(_ffn_sample_kernel, nb=nb, t=t, nc=nc),
        grid=(n // nb, nc),
        in_specs=[
            pl.BlockSpec((tm, d), row, pipeline_mode=pl.Buffered(1)),
            pl.BlockSpec((1, d), const),
            pl.BlockSpec((1, d, 2 * tc), lambda i, c: (c, 0, 0)),
            pl.BlockSpec((width, tc), lo),
            pl.BlockSpec((width, tc), hi),
            pl.BlockSpec((1, tc), lo),
            pl.BlockSpec((1, tc), hi),
            pl.BlockSpec((nb, width - 1, tc), lambda i, c: (i, 0, c)),
            pl.BlockSpec((nb, width - 1, tc), lambda i, c: (i, 0, c + nc)),
            pl.BlockSpec((tc, d), lambda i, c: (c, 0)),
            pl.BlockSpec((1, d), const),
        ],
        out_specs=[
            pl.BlockSpec((tm, d), row, pipeline_mode=pl.Buffered(1)),
            pl.BlockSpec((nb, width - 1, tc), lambda i, c: (i, 0, c)),
            pl.BlockSpec((nb, width - 1, tc), lambda i, c: (i, 0, c)),
        ],
        out_shape=[
            jax.ShapeDtypeStruct((m, d), _f32),
            jax.ShapeDtypeStruct((n, width - 1, dff), _f32),
            jax.ShapeDtypeStruct((n, width - 1, dff), _f32),
        ],
        scratch_shapes=[
            pltpu.VMEM((tm, d), _bf16),
            pltpu.VMEM((tm, d), _f32),
            pltpu.VMEM((tm, tc), _bf16),
            pltpu.VMEM((nb, 8 + t, tc), _f32),
            pltpu.VMEM((nb, 8 + t, tc), _f32),
        ],
        compiler_params=_params("arbitrary", "arbitrary"),
        name="ffn_sample",
    )(h, g_pre, w_up, ffn_k, ffn_k, ffn_b, ffn_b, state, state, w_down, g_post)
    return y, jnp.concatenate([tg, tv], axis=-1)


def _rel_bucket(dist):
    max_exact = N_BUCKETS // 2
    dd = jnp.maximum(dist, 1).astype(_f32)
    large = max_exact + (jnp.log(dd / max_exact) / math.log(MAX_DISTANCE / max_exact)
                         * (N_BUCKETS - max_exact)).astype(jnp.int32)
    large = jnp.minimum(large, N_BUCKETS - 1)
    return jnp.where(dist < max_exact, dist, large)


def _bias_table(rel_bias):
    n = WINDOW
    by_dist = rel_bias[_rel_bucket(jnp.arange(n))].astype(_f32).T
    g = jnp.concatenate([jnp.full_like(by_dist, MASKED), by_dist[:, ::-1],
                         jnp.full_like(by_dist, MASKED)], axis=1)
    heads, length = g.shape
    skew = jnp.broadcast_to(g[:, None, :], (heads, n, length)).reshape(heads, n * length)
    skew = skew[:, :n * (length - 1)].reshape(heads, n, length - 1)
    return skew[:, :, n - 1:3 * n - 1]


def _row(v):
    return v.reshape(1, -1)


def _layer(xp, xs, cache_k, cache_v, state_conv, layer, st_ffn, p):
    (g_pre, w_in, b_in, conv_k, conv_b, ln_g, ln_b, w_conv, sinks, rel_bias, w_attn, w_out, g_post,
     g_ffn_pre, w_up, ffn_k, ffn_b, w_down, g_ffn_post) = p
    nseq, seq, d = xp.shape
    ns, t, _ = xs.shape
    n_heads = sinks.shape[0]
    keep, kv_heads, hd = cache_k.shape[1:]
    group = n_heads // kv_heads
    dq, dkv = n_heads * hd, kv_heads * hd
    d_conv = conv_k.shape[1]
    assert d_conv == d and dq == d and keep == WINDOW and seq % WINDOW == 0

    w_conv_b, w_attn_b, w_out_b, w_down_b = (w.astype(_bf16) for w in (w_conv, w_attn, w_out, w_down))
    w_in_b = _column_tiles(w_in, 512)
    w_up_b = _column_tiles(w_up, min(FFN_CHUNK, w_down.shape[0]), groups=2)
    g_pre, b_in, conv_b, ln_g, ln_b, g_post, g_ffn_pre, ffn_b, g_ffn_post = map(
        _row, (g_pre, b_in, conv_b, ln_g, ln_b, g_post, g_ffn_pre, ffn_b, g_ffn_post))

    bias = _bias_table(rel_bias)

    outs = []
    for x3, is_prompt in ((xp, True), (xs, False)):
        n = x3.shape[0]
        x = x3.reshape(-1, d)
        glu, qkv, gates = _inproj(x, g_pre, w_in_b, b_in, d_conv, dq + 2 * dkv, 2 * d)
        if is_prompt:
            conv_out = _conv_prompt(glu, conv_k, conv_b, ln_g, ln_b, w_conv_b, nseq)
            attn = _attn_prompt(qkv, bias, sinks.astype(_f32), nseq, n_heads, kv_heads, hd)
            new_conv = glu.reshape(n, seq, d)[:, seq - (conv_k.shape[0] - 1):]
            last = qkv.reshape(n, seq, -1)[:, seq - keep:]
            new_k = last[:, :, dq:dq + dkv].reshape(n, keep, kv_heads, hd)
            new_v = last[:, :, dq + dkv:].reshape(n, keep, kv_heads, hd)
        else:
            conv_out, new_conv = _conv_sample(state_conv, layer, glu, conv_k, conv_b, ln_g, ln_b, w_conv_b)
            new_conv = new_conv[0]
            attn, new_k, new_v = _attn_sample(qkv, cache_k.reshape(n, keep, dkv), cache_v.reshape(n, keep, dkv), bias,
                                              sinks.astype(_f32), t, n_heads, kv_heads, hd)
            new_k, new_v = (a.reshape(n, keep, kv_heads, hd) for a in (new_k, new_v))
        h = _mix(attn, conv_out, gates, x, w_attn_b, w_out_b, g_post)
        if is_prompt:
            y, new_ffn = _ffn_prompt(h, g_ffn_pre, w_up_b, ffn_k, ffn_b, w_down_b, g_ffn_post, nseq)
        else:
            y, new_ffn = _ffn_sample(h, st_ffn, g_ffn_pre, w_up_b, ffn_k, ffn_b, w_down_b, g_ffn_post, t)
        outs.append((y.reshape(x3.shape), new_k, new_v, new_conv, new_ffn))
    return outs


def kernel(x_prompt, x_sample, cache_k, cache_v, state_conv, state_ffn_conv, norm_mix_pre, w_in, b_in, conv_dw_k, conv_dw_b, conv_ln_g, conv_ln_b, w_conv_proj, attn_sinks, rel_bias, w_attn_proj, w_out, norm_mix_post, norm_ffn_pre, w_up, ffn_dw_k, ffn_dw_b, w_down, norm_ffn_post):
    y_p, y_s = x_prompt, x_sample
    per_layer = []
    for l in range(w_in.shape[0]):
        p = (norm_mix_pre[l], w_in[l], b_in[l], conv_dw_k[l], conv_dw_b[l], conv_ln_g[l], conv_ln_b[l], w_conv_proj[l],
             attn_sinks[l], rel_bias, w_attn_proj[l], w_out[l], norm_mix_post[l],
             norm_ffn_pre[l], w_up[l], ffn_dw_k[l], ffn_dw_b[l], w_down[l], norm_ffn_post[l])
        (y_p, *rest_p), (y_s, *rest_s) = _layer(y_p, y_s, cache_k[l], cache_v[l], state_conv, l, state_ffn_conv[l], p)
        per_layer.append(rest_p + rest_s)
    stacked = [jnp.stack(leaves) for leaves in zip(*per_layer)]
    return (y_p, y_s, *stacked)
```

```python
import functools
import math

import jax
import jax.numpy as jnp
from jax import lax
from jax.experimental import pallas as pl
from jax.experimental.pallas import tpu as pltpu

EPS = 1e-6
WINDOW = 128
N_BUCKETS = 32
MAX_DISTANCE = 128
MASKED = -1e30
VMEM_LIMIT_BYTES = 60 * 1024 * 1024
HALO = 32

_bf16 = jnp.bfloat16
_f32 = jnp.float32


def _params(*sem):
    return pltpu.CompilerParams(dimension_semantics=sem, vmem_limit_bytes=VMEM_LIMIT_BYTES)


def _dot(a, b):
    return jnp.dot(a, b, preferred_element_type=_f32)


def _sigmoid(x):
    return 0.5 * (1.0 + jnp.tanh(0.5 * x))


def _rms(x, g):
    return x * lax.rsqrt(jnp.mean(x * x, axis=-1, keepdims=True) + EPS) * g


def _inproj_kernel(x_ref, g_ref, wa_ref, wb_ref, ba_ref, bb_ref, glu_ref, qkv_ref, gate_ref, xn_ref, *, n_glu, n_qkv):
    j = pl.program_id(1)

    @pl.when(j == 0)
    def _():
        xn_ref[...] = _rms(x_ref[...], g_ref[...]).astype(_bf16)

    half = wb_ref.shape[2] // 2
    halves = [slice(0, half), slice(half, 2 * half)]

    def u(w_ref, b_ref, cs):
        return _dot(xn_ref[...], w_ref[0, :, cs]) + b_ref[:, cs]

    @pl.when(j < n_glu)
    def _():
        for cs in halves:
            glu_ref[:, cs] = u(wa_ref, ba_ref, cs) * _sigmoid(u(wb_ref, bb_ref, cs))

    @pl.when((j >= n_glu) & (j < n_glu + n_qkv))
    def _():
        for cs in halves:
            qkv_ref[:, cs] = u(wb_ref, bb_ref, cs)

    @pl.when(j >= n_glu + n_qkv)
    def _():
        for cs in halves:
            gate_ref[:, cs] = _sigmoid(u(wb_ref, bb_ref, cs))


def _cast_tile_kernel(w_ref, o_ref):
    o_ref[0] = w_ref[...].astype(_bf16)


def _column_tiles(w, tn, groups=1):
    d, n = w.shape
    nt = n // groups // tn
    return pl.pallas_call(
        _cast_tile_kernel,
        grid=(nt, groups),
        in_specs=[pl.BlockSpec((d, tn), lambda c, g: (0, g * nt + c))],
        out_specs=pl.BlockSpec((1, d, tn), lambda c, g: (c, 0, g)),
        out_shape=jax.ShapeDtypeStruct((nt, d, groups * tn), _bf16),
        compiler_params=_params("arbitrary", "arbitrary"),
        name="weight_tiles",
    )(w)


def _inproj(x, g, w, b, d_glu, d_qkv, d_gate, tm=1024):
    m, d = x.shape
    tm = min(tm, m)
    tn = w.shape[2]
    n_glu, n_qkv, n_gate = d_glu // tn, d_qkv // tn, d_gate // tn
    assert w.shape[0] == 2 * n_glu + n_qkv + n_gate
    first_w = lambda i, j: (jnp.minimum(j, n_glu - 1), 0, 0)
    rest_w = lambda i, j: (j + n_glu, 0, 0)
    first = lambda i, j: (0, jnp.minimum(j, n_glu - 1))
    rest = lambda i, j: (0, j + n_glu)
    return pl.pallas_call(
        functools.partial(_inproj_kernel, n_glu=n_glu, n_qkv=n_qkv),
        grid=(m // tm, n_glu + n_qkv + n_gate),
        in_specs=[
            pl.BlockSpec((tm, d), lambda i, j: (i, 0)),
            pl.BlockSpec((1, d), lambda i, j: (0, 0)),
            pl.BlockSpec((1, d, tn), first_w),
            pl.BlockSpec((1, d, tn), rest_w),
            pl.BlockSpec((1, tn), first),
            pl.BlockSpec((1, tn), rest),
        ],
        out_specs=[
            pl.BlockSpec((tm, tn), lambda i, j: (i, jnp.minimum(j, n_glu - 1))),
            pl.BlockSpec((tm, tn), lambda i, j: (i, jnp.clip(j - n_glu, 0, n_qkv - 1))),
            pl.BlockSpec((tm, tn), lambda i, j: (i, jnp.maximum(j - n_glu - n_qkv, 0))),
        ],
        out_shape=[
            jax.ShapeDtypeStruct((m, d_glu), _f32),
            jax.ShapeDtypeStruct((m, d_qkv), _f32),
            jax.ShapeDtypeStruct((m, d_gate), _f32),
        ],
        scratch_shapes=[pltpu.VMEM((tm, d), _bf16)],
        compiler_params=_params("arbitrary", "arbitrary"),
        name="inproj",
    )(x, g, w, w, b, b)


def _ln_silu_proj(conv_ref, act_ref, lg_ref, lb_ref, w_ref, o_ref, rows):
    rt = 16

    def body(r, carry):
        r0 = pl.multiple_of(r * rt, rt)
        c = conv_ref[pl.ds(r0, rt), :]
        mu = jnp.mean(c, axis=-1, keepdims=True)
        cc = c - mu
        var = jnp.mean(cc * cc, axis=-1, keepdims=True)
        y = cc * lax.rsqrt(var + EPS) * lg_ref[...] + lb_ref[...]
        act_ref[pl.ds(r0, rt), :] = (y * _sigmoid(y)).astype(_bf16)
        return carry

    lax.fori_loop(0, rows // rt, body, 0, unroll=4)
    o_ref[...] = _dot(act_ref[...], w_ref[...])


def _conv_prompt_kernel(glu_ref, k_ref, cb_ref, lg_ref, lb_ref, w_ref, o_ref, ext_ref, conv_ref, act_ref, *, tm, width):
    d = glu_ref.shape[1]
    i = pl.program_id(1)

    @pl.when(i == 0)
    def _():
        ext_ref[0:HALO, :] = jnp.zeros((HALO, d), _f32)

    @pl.when(i > 0)
    def _():
        ext_ref[0:HALO, :] = ext_ref[tm:tm + HALO, :]

    ext_ref[HALO:HALO + tm, :] = glu_ref[...]

    rt, ct, sub = 64, 128, 8
    first = HALO - (width - 1)

    def body(r, carry):
        r0 = pl.multiple_of(r * rt, rt)
        for c in range(d // ct):
            cs = slice(c * ct, (c + 1) * ct)
            blk = ext_ref[pl.ds(r0, rt + HALO), cs]
            acc = jnp.broadcast_to(cb_ref[:, cs], (rt, ct))
            for phase in range(sub):
                taps = [w for w in range(width) if (first + w) % sub == phase]
                rows = rt if phase == 0 else rt + sub
                part = None
                for w in taps:
                    lo = first + w - phase
                    term = k_ref[w:w + 1, cs] * blk[lo:lo + rows, :]
                    part = term if part is None else part + term
                acc = acc + part[phase:phase + rt, :]
            conv_ref[pl.ds(r0, rt), cs] = acc
        return carry

    lax.fori_loop(0, tm // rt, body, 0)
    _ln_silu_proj(conv_ref, act_ref, lg_ref, lb_ref, w_ref, o_ref, tm)


def _conv_prompt(glu, conv_k, conv_b, ln_g, ln_b, w, nseq, tm=256):
    m, d = glu.shape
    nt = m // nseq // tm
    width = conv_k.shape[0]
    const = lambda b, i: (0, 0)
    return pl.pallas_call(
        functools.partial(_conv_prompt_kernel, tm=tm, width=width),
        grid=(nseq, nt),
        in_specs=[
            pl.BlockSpec((tm, d), lambda b, i: (b * nt + i, 0)),
            pl.BlockSpec((width, d), const),
            pl.BlockSpec((1, d), const),
            pl.BlockSpec((1, d), const),
            pl.BlockSpec((1, d), const),
            pl.BlockSpec((d, d), const),
        ],
        out_specs=pl.BlockSpec((tm, d), lambda b, i: (b * nt + i, 0)),
        out_shape=jax.ShapeDtypeStruct((m, d), _f32),
        scratch_shapes=[pltpu.VMEM((tm + HALO, d), _f32), pltpu.VMEM((tm, d), _f32), pltpu.VMEM((tm, d), _bf16)],
        compiler_params=_params("arbitrary", "arbitrary"),
        name="conv_prompt",
    )(glu, conv_k, conv_b, ln_g, ln_b, w)


def _conv_sample_kernel(st_ref, glu_ref, k_ref, cb_ref, lg_ref, lb_ref, w_ref, o_ref, ns_ref,
                        ext_ref, conv_ref, act_ref, *, nb, t, width):
    d = glu_ref.shape[1]
    hist = width - 1
    ct = 512

    def body(n, carry):
        r0 = pl.multiple_of(n * t, t)
        ext_ref[0:hist, :] = st_ref[0, n]
        ext_ref[hist:hist + t, :] = glu_ref[pl.ds(r0, t), :]
        ns_ref[0, n] = ext_ref[t:t + hist, :]
        for c in range(d // ct):
            cs = slice(c * ct, (c + 1) * ct)
            blk = ext_ref[0:hist + t, cs]
            acc = jnp.broadcast_to(cb_ref[:, cs], (t, ct))
            for w in range(width):
                acc = acc + k_ref[w:w + 1, cs] * blk[w:w + t, :]
            conv_ref[pl.ds(r0, t), cs] = acc
        return carry

    lax.fori_loop(0, nb, body, 0, unroll=2)
    _ln_silu_proj(conv_ref, act_ref, lg_ref, lb_ref, w_ref, o_ref, nb * t)


def _conv_sample(state, layer, glu, conv_k, conv_b, ln_g, ln_b, w, nb=32):
    _, n, hist, d = state.shape
    width = conv_k.shape[0]
    t = glu.shape[0] // n
    assert hist == width - 1 and t <= hist
    const = lambda i: (0, 0)
    return pl.pallas_call(
        functools.partial(_conv_sample_kernel, nb=nb, t=t, width=width),
        grid=(n // nb,),
        in_specs=[
            pl.BlockSpec((1, nb, hist, d), lambda i: (layer, i, 0, 0)),
            pl.BlockSpec((nb * t, d), lambda i: (i, 0)),
            pl.BlockSpec((width, d), const),
            pl.BlockSpec((1, d), const),
            pl.BlockSpec((1, d), const),
            pl.BlockSpec((1, d), const),
            pl.BlockSpec((d, d), const),
        ],
        out_specs=[
            pl.BlockSpec((nb * t, d), lambda i: (i, 0)),
            pl.BlockSpec((1, nb, hist, d), lambda i: (0, i, 0, 0)),
        ],
        out_shape=[
            jax.ShapeDtypeStruct((n * t, d), _f32),
            jax.ShapeDtypeStruct((1, n, hist, d), _f32),
        ],
        scratch_shapes=[pltpu.VMEM((hist + t + (-(hist + t)) % 8, d), _f32), pltpu.VMEM((nb * t, d), _f32),
                        pltpu.VMEM((nb * t, d), _bf16)],
        compiler_params=_params("arbitrary"),
        name="conv_sample",
    )(state, glu, conv_k, conv_b, ln_g, ln_b, w)


def _attn_prompt_kernel(q_ref, kc_ref, kp_ref, vc_ref, vp_ref, bias_ref, sink_ref, o_ref, *, kv_heads, hd):
    blk = q_ref.shape[0]
    grp = 2 * hd
    pairs = q_ref.shape[1] // (kv_heads * grp)
    cols = pairs * blk
    scale = hd ** -0.5
    nt = (((1,), (1,)), ((), ()))
    low = lax.broadcasted_iota(jnp.int32, (2 * blk, grp), 1) < hd
    qi = lax.broadcasted_iota(jnp.int32, (blk, cols), 1) & (blk - 1)
    cur_visible = lax.broadcasted_iota(jnp.int32, (blk, cols), 0) <= qi

    def halves(p_ref, c_ref, h):
        g0 = (h // 2) * grp
        x = jnp.concatenate([p_ref[:, g0:g0 + grp], c_ref[:, g0:g0 + grp]], axis=0)
        swapped = pltpu.roll(x, hd, axis=1)
        in_low, in_high = (x, swapped) if h % 2 == 0 else (swapped, x)
        return jnp.where(low, in_low, 0.0), jnp.where(low, 0.0, in_high)

    for h in range(kv_heads):
        ks = [x.astype(_bf16) for x in halves(kp_ref, kc_ref, h)]
        vts = [x.T.astype(_bf16) for x in halves(vp_ref, vc_ref, h)]
        lanes = [slice((h * pairs + p) * grp, (h * pairs + p + 1) * grp) for p in range(pairs)]
        qs = (jnp.concatenate([q_ref[:, ls] for ls in lanes], axis=0) * scale).astype(_bf16)
        out_t = None
        for parity in range(2):
            s2 = lax.dot_general(ks[parity], qs, nt, preferred_element_type=_f32)
            s = jnp.where(cur_visible, s2[blk:, :], s2[:blk, :]) + bias_ref[0, h, parity]
            sink = sink_ref[h, parity]
            m = jnp.maximum(jnp.max(s, axis=0, keepdims=True), sink)
            p = jnp.exp(s - m)
            den = jnp.sum(p, axis=0, keepdims=True) + jnp.exp(sink - m)
            pn = p * (1.0 / den)
            p2 = jnp.concatenate([jnp.where(cur_visible, 0.0, pn), jnp.where(cur_visible, pn, 0.0)], axis=0)
            o_t = _dot(vts[parity], p2.astype(_bf16))
            out_t = o_t if out_t is None else out_t + o_t
        out = out_t.T
        for p, ls in enumerate(lanes):
            o_ref[:, ls] = out[p * blk:(p + 1) * blk, :]


def _attn_prompt(qkv, bias, sinks, nseq, n_heads, kv_heads, hd):
    m = qkv.shape[0]
    dq, dkv = n_heads * hd, kv_heads * hd
    nb = m // nseq // WINDOW
    pairs = n_heads // kv_heads // 2
    rows = pairs * WINDOW
    assert WINDOW & (WINDOW - 1) == 0 and 2 * hd == 128 and kv_heads % 2 == 0
    qi = jnp.arange(WINDOW)[:, None]
    cur_visible = jnp.arange(WINDOW)[None, :] <= qi
    folded = jnp.stack([jnp.where(cur_visible, bias[:, :, WINDOW:], MASKED),
                        jnp.where(cur_visible, bias[:, :, WINDOW:], bias[:, :, :WINDOW])])
    folded = folded.reshape(2, kv_heads, pairs, 2, WINDOW, WINDOW).transpose(0, 1, 3, 5, 2, 4)
    folded = folded.reshape(2, kv_heads, 2, WINDOW, rows)
    sink_rows = jnp.repeat(sinks.reshape(kv_heads, pairs, 2).transpose(0, 2, 1), WINDOW, axis=-1)
    sink_rows = sink_rows.reshape(kv_heads, 2, 1, rows)
    kcol, vcol = dq // dkv, dq // dkv + 1
    cur = lambda b, i: b * nb + i
    prev = lambda b, i: b * nb + jnp.maximum(i - 1, 0)
    return pl.pallas_call(
        functools.partial(_attn_prompt_kernel, kv_heads=kv_heads, hd=hd),
        grid=(nseq, nb),
        in_specs=[
            pl.BlockSpec((WINDOW, dq), lambda b, i: (cur(b, i), 0)),
            pl.BlockSpec((WINDOW, dkv), lambda b, i: (cur(b, i), kcol)),
            pl.BlockSpec((WINDOW, dkv), lambda b, i: (prev(b, i), kcol)),
            pl.BlockSpec((WINDOW, dkv), lambda b, i: (cur(b, i), vcol)),
            pl.BlockSpec((WINDOW, dkv), lambda b, i: (prev(b, i), vcol)),
            pl.BlockSpec((1, kv_heads, 2, WINDOW, rows), lambda b, i: (jnp.minimum(i, 1), 0, 0, 0, 0)),
            pl.BlockSpec((kv_heads, 2, 1, rows), lambda b, i: (0, 0, 0, 0)),
        ],
        out_specs=pl.BlockSpec((WINDOW, dq), lambda b, i: (cur(b, i), 0)),
        out_shape=jax.ShapeDtypeStruct((m, dq), _f32),
        compiler_params=_params("arbitrary", "arbitrary"),
        name="attn_prompt",
    )(qkv, qkv, qkv, qkv, qkv, folded, sink_rows)


def _attn_sample_kernel(qkv_ref, ck_ref, cv_ref, bias_ref, sink_ref, o_ref, ko_ref, vo_ref, *,
                        nb, t, n_heads, kv_heads, hd):
    dq, dkv = n_heads * hd, kv_heads * hd
    group = n_heads // kv_heads
    keep = ck_ref.shape[1]
    keys = bias_ref.shape[0]
    grp = 2 * hd
    scale = hd ** -0.5
    nt = (((1,), (1,)), ((), ()))
    low = lax.broadcasted_iota(jnp.int32, (t, grp), 1) < hd
    blank = jnp.zeros((t, grp), _f32)
    pad = jnp.zeros((keys - keep - t, dkv), _f32)

    def body(n, carry):
        r0 = pl.multiple_of(n * t, t)
        q = qkv_ref[pl.ds(r0, t), 0:dq]
        kn = qkv_ref[pl.ds(r0, t), dq:dq + dkv]
        vn = qkv_ref[pl.ds(r0, t), dq + dkv:dq + 2 * dkv]
        ck, cv = ck_ref[n], cv_ref[n]
        ko_ref[n, 0:keep - t, :] = ck[t:, :]
        ko_ref[n, keep - t:keep, :] = kn
        vo_ref[n, 0:keep - t, :] = cv[t:, :]
        vo_ref[n, keep - t:keep, :] = vn
        kp = jnp.concatenate([ck, kn, pad], axis=0).astype(_bf16)
        vp = jnp.concatenate([cv, vn, pad], axis=0).astype(_bf16)

        rows = []
        for h in range(kv_heads):
            for pair in range(group // 2):
                qv = q[:, (h * (group // 2) + pair) * grp:(h * (group // 2) + pair + 1) * grp]
                swapped = pltpu.roll(qv, hd, axis=1)
                for parity in range(2):
                    x = qv if parity == h % 2 else swapped
                    x = jnp.where(low, x, 0.0) if h % 2 == 0 else jnp.where(low, 0.0, x)
                    pieces = [blank] * (dkv // grp)
                    pieces[h // 2] = x
                    rows.append(jnp.concatenate(pieces, axis=1))
        wt = (jnp.concatenate(rows, axis=0) * scale).astype(_bf16)

        st = lax.dot_general(kp, wt, nt, preferred_element_type=_f32) + bias_ref[...]
        sink = sink_ref[...]
        m = jnp.maximum(jnp.max(st, axis=0, keepdims=True), sink)
        p = jnp.exp(st - m)
        den = jnp.sum(p, axis=0, keepdims=True) + jnp.exp(sink - m)
        pn = p * (1.0 / den)
        of = _dot(pn.T.astype(_bf16), vp)

        outs = []
        for h in range(kv_heads):
            ls = slice((h // 2) * grp, (h // 2 + 1) * grp)
            for pair in range(group // 2):
                r = (h * group + 2 * pair) * t
                even, odd = of[r:r + t, ls], of[r + t:r + 2 * t, ls]
                if h % 2 == 0:
                    outs.append(jnp.where(low, even, pltpu.roll(odd, hd, axis=1)))
                else:
                    outs.append(jnp.where(low, pltpu.roll(even, hd, axis=1), odd))
        o_ref[pl.ds(r0, t), :] = jnp.concatenate(outs, axis=1)
        return carry

    lax.fori_loop(0, nb, body, 0, unroll=2)


def _attn_sample(qkv, cache_k, cache_v, bias, sinks, t, n_heads, kv_heads, hd, nb=8):
    n, keep, dkv = cache_k.shape
    dq = n_heads * hd
    keys = bias.shape[2]
    assert keep == WINDOW and keep + t <= keys and 2 * hd == 128 and kv_heads % 2 == 0
    bias_t = bias[:, :t].transpose(2, 0, 1).reshape(keys, n_heads * t)
    sink_row = jnp.repeat(sinks, t).reshape(1, n_heads * t)
    const = lambda i: (0, 0)
    return pl.pallas_call(
        functools.partial(_attn_sample_kernel, nb=nb, t=t, n_heads=n_heads, kv_heads=kv_heads, hd=hd),
        grid=(n // nb,),
        in_specs=[
            pl.BlockSpec((nb * t, qkv.shape[1]), lambda i: (i, 0)),
            pl.BlockSpec((nb, keep, dkv), lambda i: (i, 0, 0)),
            pl.BlockSpec((nb, keep, dkv), lambda i: (i, 0, 0)),
            pl.BlockSpec((keys, n_heads * t), const),
            pl.BlockSpec((1, n_heads * t), const),
        ],
        out_specs=[
            pl.BlockSpec((nb * t, dq), lambda i: (i, 0)),
            pl.BlockSpec((nb, keep, dkv), lambda i: (i, 0, 0)),
            pl.BlockSpec((nb, keep, dkv), lambda i: (i, 0, 0)),
        ],
        out_shape=[
            jax.ShapeDtypeStruct((n * t, dq), _f32),
            jax.ShapeDtypeStruct((n, keep, dkv), _f32),
            jax.ShapeDtypeStruct((n, keep, dkv), _f32),
        ],
        compiler_params=_params("arbitrary"),
        name="attn_sample",
    )(qkv, cache_k, cache_v, bias_t, sink_row)


def _mix_kernel(attn_ref, conv_ref, gc_ref, ga_ref, x_ref, wa_ref, wo_ref, g_ref, h_ref):
    ao = _dot(attn_ref[...].astype(_bf16), wa_ref[...])
    mixed = gc_ref[...] * conv_ref[...] + ga_ref[...] * ao
    o = _dot(mixed.astype(_bf16), wo_ref[...])
    h_ref[...] = x_ref[...] + _rms(o, g_ref[...])


def _mix(attn, conv_out, gates, x, w_attn, w_out, g_post, tm=256):
    m, d = x.shape
    row = lambda i: (i, 0)
    const = lambda i: (0, 0)
    once = pl.Buffered(1)
    return pl.pallas_call(
        _mix_kernel,
        grid=(m // tm,),
        in_specs=[
            pl.BlockSpec((tm, d), row),
            pl.BlockSpec((tm, d), row),
            pl.BlockSpec((tm, d), lambda i: (i, 0)),
            pl.BlockSpec((tm, d), lambda i: (i, 1)),
            pl.BlockSpec((tm, d), row),
            pl.BlockSpec((d, d), const, pipeline_mode=once),
            pl.BlockSpec((d, d), const, pipeline_mode=once),
            pl.BlockSpec((1, d), const),
        ],
        out_specs=pl.BlockSpec((tm, d), row),
        out_shape=jax.ShapeDtypeStruct((m, d), _f32),
        compiler_params=_params("arbitrary"),
        name="mix",
    )(attn, conv_out, gates, gates, x, w_attn, w_out, g_post)


def _gelu_tanh(x):
    return 0.5 * x * (1.0 + jnp.tanh(math.sqrt(2.0 / math.pi) * (x + 0.044715 * (x * x * x))))


MXU_COLUMNS = 256
FFN_CHUNK = 1024


def _chunks(n, width=MXU_COLUMNS):
    return [slice(a, a + width) for a in range(0, n, width)]


def _ffn_finish(c, nc, acc_ref, h_ref, gp_ref, y_ref):
    @pl.when(c == nc - 1)
    def _():
        y_ref[...] = h_ref[...] + _rms(acc_ref[...], gp_ref[...])


def _ffn_prompt_kernel(h_ref, g_ref, w_ref, kg_ref, kv_ref, bg_ref, bv_ref, wd_ref, gp_ref,
                       y_ref, tg_ref, tv_ref, hn_ref, acc_ref, act_ref, ug_ref, uv_ref, cg_ref, cv_ref, *, tm, nc):
    i, c = pl.program_id(1), pl.program_id(2)
    tc = w_ref.shape[2] // 2

    @pl.when(c == 0)
    def _():
        hn_ref[...] = _rms(h_ref[...], g_ref[...]).astype(_bf16)
        acc_ref[...] = jnp.zeros_like(acc_ref)

    @pl.when(i == 0)
    def _():
        ug_ref[0:8, :] = jnp.zeros((8, tc), _f32)
        uv_ref[0:8, :] = jnp.zeros((8, tc), _f32)

    @pl.when(i > 0)
    def _():
        ug_ref[0:8, :] = cg_ref[c]
        uv_ref[0:8, :] = cv_ref[c]

    def conv(u_ref, col0, k_ref, b_ref, cs):
        u_ref[8:8 + tm, cs] = _dot(hn_ref[...], w_ref[0, :, col0 + cs.start:col0 + cs.stop])
        return (k_ref[2:3, cs] * u_ref[8:8 + tm, cs] + k_ref[1:2, cs] * u_ref[7:7 + tm, cs]
                + k_ref[0:1, cs] * u_ref[6:6 + tm, cs] + b_ref[:, cs])

    for cs in _chunks(tc):
        gate = conv(ug_ref, 0, kg_ref, bg_ref, cs)
        val = conv(uv_ref, tc, kv_ref, bv_ref, cs)
        act_ref[:, cs] = (_gelu_tanh(gate) * val).astype(_bf16)
    acc_ref[...] += _dot(act_ref[...], wd_ref[...])
    cg_ref[c] = ug_ref[tm:tm + 8, :]
    cv_ref[c] = uv_ref[tm:tm + 8, :]
    tg_ref[0, 0] = ug_ref[tm + 6:tm + 8, :]
    tv_ref[0, 0] = uv_ref[tm + 6:tm + 8, :]
    _ffn_finish(c, nc, acc_ref, h_ref, gp_ref, y_ref)


def _ffn_prompt(h, g_pre, w_up, ffn_k, ffn_b, w_down, g_post, nseq, tm=512):
    m, d = h.shape
    dff = w_down.shape[0]
    nc, tc = w_up.shape[0], w_up.shape[2] // 2
    nt = m // nseq // tm
    width = ffn_k.shape[0]
    assert width == 3 and nc * tc == dff
    row = lambda b, i, c: (b * nt + i, 0)
    const = lambda b, i, c: (0, 0)
    lo = lambda b, i, c: (0, c)
    hi = lambda b, i, c: (0, c + nc)
    y, tg, tv = pl.pallas_call(
        functools.partial(_ffn_prompt_kernel, tm=tm, nc=nc),
        grid=(nseq, nt, nc),
        in_specs=[
            pl.BlockSpec((tm, d), row),
            pl.BlockSpec((1, d), const),
            pl.BlockSpec((1, d, 2 * tc), lambda b, i, c: (c, 0, 0)),
            pl.BlockSpec((width, tc), lo),
            pl.BlockSpec((width, tc), hi),
            pl.BlockSpec((1, tc), lo),
            pl.BlockSpec((1, tc), hi),
            pl.BlockSpec((tc, d), lambda b, i, c: (c, 0)),
            pl.BlockSpec((1, d), const),
        ],
        out_specs=[
            pl.BlockSpec((tm, d), row, pipeline_mode=pl.Buffered(1)),
            pl.BlockSpec((1, 1, width - 1, tc), lambda b, i, c: (b, i, 0, c)),
            pl.BlockSpec((1, 1, width - 1, tc), lambda b, i, c: (b, i, 0, c)),
        ],
        out_shape=[
            jax.ShapeDtypeStruct((m, d), _f32),
            jax.ShapeDtypeStruct((nseq, nt, width - 1, dff), _f32),
            jax.ShapeDtypeStruct((nseq, nt, width - 1, dff), _f32),
        ],
        scratch_shapes=[
            pltpu.VMEM((tm, d), _bf16),
            pltpu.VMEM((tm, d), _f32),
            pltpu.VMEM((tm, tc), _bf16),
            pltpu.VMEM((tm + 8, tc), _f32),
            pltpu.VMEM((tm + 8, tc), _f32),
            pltpu.VMEM((nc, 8, tc), _f32),
            pltpu.VMEM((nc, 8, tc), _f32),
        ],
        compiler_params=_params("arbitrary", "arbitrary", "arbitrary"),
        name="ffn_prompt",
    )(h, g_pre, w_up, ffn_k, ffn_k, ffn_b, ffn_b, w_down, g_post)
    return y, jnp.concatenate([tg[:, -1], tv[:, -1]], axis=-1)


def _ffn_sample_kernel(h_ref, g_ref, w_ref, kg_ref, kv_ref, bg_ref, bv_ref, sg_ref, sv_ref, wd_ref, gp_ref,
                       y_ref, tg_ref, tv_ref, hn_ref, acc_ref, act_ref, ug_ref, uv_ref, *, nb, t, nc):
    c = pl.program_id(1)
    tc = w_ref.shape[2] // 2

    @pl.when(c == 0)
    def _():
        hn_ref[...] = _rms(h_ref[...], g_ref[...]).astype(_bf16)
        acc_ref[...] = jnp.zeros_like(acc_ref)

    ug_ref[:, 6:8, :] = sg_ref[...]
    uv_ref[:, 6:8, :] = sv_ref[...]

    def conv(u_ref, col0, k_ref, b_ref, cs):
        width = cs.stop - cs.start
        u_ref[:, 8:8 + t, cs] = _dot(hn_ref[...], w_ref[0, :, col0 + cs.start:col0 + cs.stop]).reshape(nb, t, width)
        cv = (k_ref[2:3, cs] * u_ref[:, 8:8 + t, cs] + k_ref[1:2, cs] * u_ref[:, 7:7 + t, cs]
              + k_ref[0:1, cs] * u_ref[:, 6:6 + t, cs] + b_ref[:, cs])
        return cv.reshape(nb * t, width)

    for cs in _chunks(tc):
        gate = conv(ug_ref, 0, kg_ref, bg_ref, cs)
        val = conv(uv_ref, tc, kv_ref, bv_ref, cs)
        act_ref[:, cs] = (_gelu_tanh(gate) * val).astype(_bf16)
    acc_ref[...] += _dot(act_ref[...], wd_ref[...])
    tg_ref[...] = ug_ref[:, 6 + t:8 + t, :]
    tv_ref[...] = uv_ref[:, 6 + t:8 + t, :]
    _ffn_finish(c, nc, acc_ref, h_ref, gp_ref, y_ref)


def _ffn_sample(h, state, g_pre, w_up, ffn_k, ffn_b, w_down, g_post, t, nb=64):
    m, d = h.shape
    n = m // t
    dff = w_down.shape[0]
    nc, tc = w_up.shape[0], w_up.shape[2] // 2
    width = ffn_k.shape[0]
    assert width == 3 and t >= width - 1 and nc * tc == dff
    tm = nb * t
    row = lambda i, c: (i, 0)
    const = lambda i, c: (0, 0)
    lo = lambda i, c: (0, c)
    hi = lambda i, c: (0, c + nc)
    y, tg, tv = pl.pallas_call(
        functools.partial(_ffn_sample_kernel, nb=nb, t=t, nc=nc),
        grid=(n // nb, nc),
        in_specs=[
            pl.BlockSpec((tm, d), row, pipeline_mode=pl.Buffered(1)),
            pl.BlockSpec((1, d), const),
            pl.BlockSpec((1, d, 2 * tc), lambda i, c: (c, 0, 0)),
            pl.BlockSpec((width, tc), lo),
            pl.BlockSpec((width, tc), hi),
            pl.BlockSpec((1, tc), lo),
            pl.BlockSpec((1, tc), hi),
            pl.BlockSpec((nb, width - 1, tc), lambda i, c: (i, 0, c)),
            pl.BlockSpec((nb, width - 1, tc), lambda i, c: (i, 0, c + nc)),
            pl.BlockSpec((tc, d), lambda i, c: (c, 0)),
            pl.BlockSpec((1, d), const),
        ],
        out_specs=[
            pl.BlockSpec((tm, d), row, pipeline_mode=pl.Buffered(1)),
            pl.BlockSpec((nb, width - 1, tc), lambda i, c: (i, 0, c)),
            pl.BlockSpec((nb, width - 1, tc), lambda i, c: (i, 0, c)),
        ],
        out_shape=[
            jax.ShapeDtypeStruct((m, d), _f32),
            jax.ShapeDtypeStruct((n, width - 1, dff), _f32),
            jax.ShapeDtypeStruct((n, width - 1, dff), _f32),
        ],
        scratch_shapes=[
            pltpu.VMEM((tm, d), _bf16),
            pltpu.VMEM((tm, d), _f32),
            pltpu.VMEM((tm, tc), _bf16),
            pltpu.VMEM((nb, 8 + t, tc), _f32),
            pltpu.VMEM((nb, 8 + t, tc), _f32),
        ],
        compiler_params=_params("arbitrary", "arbitrary"),
        name="ffn_sample",
    )(h, g_pre, w_up, ffn_k, ffn_k, ffn_b, ffn_b, state, state, w_down, g_post)
    return y, jnp.concatenate([tg, tv], axis=-1)


def _rel_bucket(dist):
    max_exact = N_BUCKETS // 2
    dd = jnp.maximum(dist, 1).astype(_f32)
    large = max_exact + (jnp.log(dd / max_exact) / math.log(MAX_DISTANCE / max_exact)
                         * (N_BUCKETS - max_exact)).astype(jnp.int32)
    large = jnp.minimum(large, N_BUCKETS - 1)
    return jnp.where(dist < max_exact, dist, large)


def _bias_table(rel_bias):
    n = WINDOW
    by_dist = rel_bias[_rel_bucket(jnp.arange(n))].astype(_f32).T
    g = jnp.concatenate([jnp.full_like(by_dist, MASKED), by_dist[:, ::-1],
                         jnp.full_like(by_dist, MASKED)], axis=1)
    heads, length = g.shape
    skew = jnp.broadcast_to(g[:, None, :], (heads, n, length)).reshape(heads, n * length)
    skew = skew[:, :n * (length - 1)].reshape(heads, n, length - 1)
    return skew[:, :, n - 1:3 * n - 1]


def _row(v):
    return v.reshape(1, -1)


def _layer(xp, xs, cache_k, cache_v, state_conv, layer, st_ffn, p):
    (g_pre, w_in, b_in, conv_k, conv_b, ln_g, ln_b, w_conv, sinks, rel_bias, w_attn, w_out, g_post,
     g_ffn_pre, w_up, ffn_k, ffn_b, w_down, g_ffn_post) = p
    nseq, seq, d = xp.shape
    ns, t, _ = xs.shape
    n_heads = sinks.shape[0]
    keep, kv_heads, hd = cache_k.shape[1:]
    group = n_heads // kv_heads
    dq, dkv = n_heads * hd, kv_heads * hd
    d_conv = conv_k.shape[1]
    assert d_conv == d and dq == d and keep == WINDOW and seq % WINDOW == 0

    w_conv_b, w_attn_b, w_out_b, w_down_b = (w.astype(_bf16) for w in (w_conv, w_attn, w_out, w_down))
    w_in_b = _column_tiles(w_in, 512)
    w_up_b = _column_tiles(w_up, min(FFN_CHUNK, w_down.shape[0]), groups=2)
    g_pre, b_in, conv_b, ln_g, ln_b, g_post, g_ffn_pre, ffn_b, g_ffn_post = map(
        _row, (g_pre, b_in, conv_b, ln_g, ln_b, g_post, g_ffn_pre, ffn_b, g_ffn_post))

    bias = _bias_table(rel_bias)

    outs = []
    for x3, is_prompt in ((xp, True), (xs, False)):
        n = x3.shape[0]
        x = x3.reshape(-1, d)
        glu, qkv, gates = _inproj(x, g_pre, w_in_b, b_in, d_conv, dq + 2 * dkv, 2 * d)
        if is_prompt:
            conv_out = _conv_prompt(glu, conv_k, conv_b, ln_g, ln_b, w_conv_b, nseq)
            attn = _attn_prompt(qkv, bias, sinks.astype(_f32), nseq, n_heads, kv_heads, hd)
            new_conv = glu.reshape(n, seq, d)[:, seq - (conv_k.shape[0] - 1):]
            last = qkv.reshape(n, seq, -1)[:, seq - keep:]
            new_k = last[:, :, dq:dq + dkv].reshape(n, keep, kv_heads, hd)
            new_v = last[:, :, dq + dkv:].reshape(n, keep, kv_heads, hd)
        else:
            conv_out, new_conv = _conv_sample(state_conv, layer, glu, conv_k, conv_b, ln_g, ln_b, w_conv_b)
            new_conv = new_conv[0]
            attn, new_k, new_v = _attn_sample(qkv, cache_k.reshape(n, keep, dkv), cache_v.reshape(n, keep, dkv), bias,
                                              sinks.astype(_f32), t, n_heads, kv_heads, hd)
            new_k, new_v = (a.reshape(n, keep, kv_heads, hd) for a in (new_k, new_v))
        h = _mix(attn, conv_out, gates, x, w_attn_b, w_out_b, g_post)
        if is_prompt:
            y, new_ffn = _ffn_prompt(h, g_ffn_pre, w_up_b, ffn_k, ffn_b, w_down_b, g_ffn_post, nseq)
        else:
            y, new_ffn = _ffn_sample(h, st_ffn, g_ffn_pre, w_up_b, ffn_k, ffn_b, w_down_b, g_ffn_post, t)
        outs.append((y.reshape(x3.shape), new_k, new_v, new_conv, new_ffn))
    return outs


def kernel(x_prompt, x_sample, cache_k, cache_v, state_conv, state_ffn_conv, norm_mix_pre, w_in, b_in, conv_dw_k, conv_dw_b, conv_ln_g, conv_ln_b, w_conv_proj, attn_sinks, rel_bias, w_attn_proj, w_out, norm_mix_post, norm_ffn_pre, w_up, ffn_dw_k, ffn_dw_b, w_down, norm_ffn_post):
    y_p, y_s = x_prompt, x_sample
    per_layer = []
    for l in range(w_in.shape[0]):
        p = (norm_mix_pre[l], w_in[l], b_in[l], conv_dw_k[l], conv_dw_b[l], conv_ln_g[l], conv_ln_b[l], w_conv_proj[l],
             attn_sinks[l], rel_bias, w_attn_proj[l], w_out[l], norm_mix_post[l],
             norm_ffn_pre[l], w_up[l], ffn_dw_k[l], ffn_dw_b[l], w_down[l], norm_ffn_post[l])
        (y_p, *rest_p), (y_s, *rest_s) = _layer(y_p, y_s, cache_k[l], cache_v[l], state_conv, l, state_ffn_conv[l], p)
        per_layer.append(rest_p + rest_s)
    stacked = [jnp.stack(leaves) for leaves in zip(*per_layer)]
    return (y_p, y_s, *stacked)
```

```python
import functools
import math

import jax
import jax.numpy as jnp
from jax import lax
from jax.experimental import pallas as pl
from jax.experimental.pallas import tpu as pltpu

EPS = 1e-6
WINDOW = 128
N_BUCKETS = 32
MAX_DISTANCE = 128
MASKED = -1e30
VMEM_LIMIT_BYTES = 60 * 1024 * 1024
HALO = 32

_bf16 = jnp.bfloat16
_f32 = jnp.float32


def _params(*sem):
    return pltpu.CompilerParams(dimension_semantics=sem, vmem_limit_bytes=VMEM_LIMIT_BYTES)


def _dot(a, b):
    return jnp.dot(a, b, preferred_element_type=_f32)


def _sigmoid(x):
    return 0.5 * (1.0 + jnp.tanh(0.5 * x))


def _rms(x, g):
    return x * lax.rsqrt(jnp.mean(x * x, axis=-1, keepdims=True) + EPS) * g


def _inproj_kernel(x_ref, g_ref, wa_ref, wb_ref, ba_ref, bb_ref, glu_ref, qkv_ref, gate_ref, xn_ref, *, n_glu, n_qkv):
    j = pl.program_id(1)

    @pl.when(j == 0)
    def _():
        xn_ref[...] = _rms(x_ref[...], g_ref[...]).astype(_bf16)

    half = wb_ref.shape[2] // 2
    halves = [slice(0, half), slice(half, 2 * half)]

    def u(w_ref, b_ref, cs):
        return _dot(xn_ref[...], w_ref[0, :, cs]) + b_ref[:, cs]

    @pl.when(j < n_glu)
    def _():
        for cs in halves:
            glu_ref[:, cs] = u(wa_ref, ba_ref, cs) * _sigmoid(u(wb_ref, bb_ref, cs))

    @pl.when((j >= n_glu) & (j < n_glu + n_qkv))
    def _():
        for cs in halves:
            qkv_ref[:, cs] = u(wb_ref, bb_ref, cs)

    @pl.when(j >= n_glu + n_qkv)
    def _():
        for cs in halves:
            gate_ref[:, cs] = _sigmoid(u(wb_ref, bb_ref, cs))


def _cast_tile_kernel(w_ref, o_ref):
    o_ref[0] = w_ref[...].astype(_bf16)


def _column_tiles(w, tn, groups=1):
    d, n = w.shape
    nt = n // groups // tn
    return pl.pallas_call(
        _cast_tile_kernel,
        grid=(nt, groups),
        in_specs=[pl.BlockSpec((d, tn), lambda c, g: (0, g * nt + c))],
        out_specs=pl.BlockSpec((1, d, tn), lambda c, g: (c, 0, g)),
        out_shape=jax.ShapeDtypeStruct((nt, d, groups * tn), _bf16),
        compiler_params=_params("arbitrary", "arbitrary"),
        name="weight_tiles",
    )(w)


def _inproj(x, g, w, b, d_glu, d_qkv, d_gate, tm=1024):
    m, d = x.shape
    tm = min(tm, m)
    tn = w.shape[2]
    n_glu, n_qkv, n_gate = d_glu // tn, d_qkv // tn, d_gate // tn
    assert w.shape[0] == 2 * n_glu + n_qkv + n_gate
    first_w = lambda i, j: (jnp.minimum(j, n_glu - 1), 0, 0)
    rest_w = lambda i, j: (j + n_glu, 0, 0)
    first = lambda i, j: (0, jnp.minimum(j, n_glu - 1))
    rest = lambda i, j: (0, j + n_glu)
    return pl.pallas_call(
        functools.partial(_inproj_kernel, n_glu=n_glu, n_qkv=n_qkv),
        grid=(m // tm, n_glu + n_qkv + n_gate),
        in_specs=[
            pl.BlockSpec((tm, d), lambda i, j: (i, 0)),
            pl.BlockSpec((1, d), lambda i, j: (0, 0)),
            pl.BlockSpec((1, d, tn), first_w),
            pl.BlockSpec((1, d, tn), rest_w),
            pl.BlockSpec((1, tn), first),
            pl.BlockSpec((1, tn), rest),
        ],
        out_specs=[
            pl.BlockSpec((tm, tn), lambda i, j: (i, jnp.minimum(j, n_glu - 1))),
            pl.BlockSpec((tm, tn), lambda i, j: (i, jnp.clip(j - n_glu, 0, n_qkv - 1))),
            pl.BlockSpec((tm, tn), lambda i, j: (i, jnp.maximum(j - n_glu - n_qkv, 0))),
        ],
        out_shape=[
            jax.ShapeDtypeStruct((m, d_glu), _f32),
            jax.ShapeDtypeStruct((m, d_qkv), _f32),
            jax.ShapeDtypeStruct((m, d_gate), _f32),
        ],
        scratch_shapes=[pltpu.VMEM((tm, d), _bf16)],
        compiler_params=_params("arbitrary", "arbitrary"),
        name="inproj",
    )(x, g, w, w, b, b)


def _ln_silu_proj(conv_ref, act_ref, lg_ref, lb_ref, w_ref, o_ref, rows):
    rt = 16

    def body(r, carry):
        r0 = pl.multiple_of(r * rt, rt)
        if len(conv_ref.shape) == 2:
            c = conv_ref[pl.ds(r0, rt), :]
        else:
            per = rt // conv_ref.shape[1]
            c = conv_ref[pl.ds(pl.multiple_of(r * per, per), per)].reshape(rt, conv_ref.shape[2])
        mu = jnp.mean(c, axis=-1, keepdims=True)
        cc = c - mu
        var = jnp.mean(cc * cc, axis=-1, keepdims=True)
        y = cc * lax.rsqrt(var + EPS) * lg_ref[...] + lb_ref[...]
        act_ref[pl.ds(r0, rt), :] = (y * _sigmoid(y)).astype(_bf16)
        return carry

    lax.fori_loop(0, rows // rt, body, 0, unroll=4)
    o_ref[...] = _dot(act_ref[...], w_ref[...])


def _conv_prompt_kernel(glu_ref, k_ref, cb_ref, lg_ref, lb_ref, w_ref, o_ref, ext_ref, conv_ref, act_ref, *, tm, width):
    d = glu_ref.shape[1]
    i = pl.program_id(1)

    @pl.when(i == 0)
    def _():
        ext_ref[0:HALO, :] = jnp.zeros((HALO, d), _f32)

    @pl.when(i > 0)
    def _():
        ext_ref[0:HALO, :] = ext_ref[tm:tm + HALO, :]

    ext_ref[HALO:HALO + tm, :] = glu_ref[...]

    rt, ct, sub = 64, 128, 8
    first = HALO - (width - 1)

    def body(r, carry):
        r0 = pl.multiple_of(r * rt, rt)
        for c in range(d // ct):
            cs = slice(c * ct, (c + 1) * ct)
            blk = ext_ref[pl.ds(r0, rt + HALO), cs]
            acc = jnp.broadcast_to(cb_ref[:, cs], (rt, ct))
            for phase in range(sub):
                taps = [w for w in range(width) if (first + w) % sub == phase]
                rows = rt if phase == 0 else rt + sub
                part = None
                for w in taps:
                    lo = first + w - phase
                    term = k_ref[w:w + 1, cs] * blk[lo:lo + rows, :]
                    part = term if part is None else part + term
                acc = acc + part[phase:phase + rt, :]
            conv_ref[pl.ds(r0, rt), cs] = acc
        return carry

    lax.fori_loop(0, tm // rt, body, 0)
    _ln_silu_proj(conv_ref, act_ref, lg_ref, lb_ref, w_ref, o_ref, tm)


def _conv_prompt(glu, conv_k, conv_b, ln_g, ln_b, w, nseq, tm=256):
    m, d = glu.shape
    nt = m // nseq // tm
    width = conv_k.shape[0]
    const = lambda b, i: (0, 0)
    return pl.pallas_call(
        functools.partial(_conv_prompt_kernel, tm=tm, width=width),
        grid=(nseq, nt),
        in_specs=[
            pl.BlockSpec((tm, d), lambda b, i: (b * nt + i, 0)),
            pl.BlockSpec((width, d), const),
            pl.BlockSpec((1, d), const),
            pl.BlockSpec((1, d), const),
            pl.BlockSpec((1, d), const),
            pl.BlockSpec((d, d), const),
        ],
        out_specs=pl.BlockSpec((tm, d), lambda b, i: (b * nt + i, 0)),
        out_shape=jax.ShapeDtypeStruct((m, d), _f32),
        scratch_shapes=[pltpu.VMEM((tm + HALO, d), _f32), pltpu.VMEM((tm, d), _f32), pltpu.VMEM((tm, d), _bf16)],
        compiler_params=_params("arbitrary", "arbitrary"),
        name="conv_prompt",
    )(glu, conv_k, conv_b, ln_g, ln_b, w)


def _conv_sample_kernel(st_ref, glu_ref, k_ref, cb_ref, lg_ref, lb_ref, w_ref, o_ref, ns_ref,
                        ext_ref, conv_ref, act_ref, *, nb, t, width):
    d = glu_ref.shape[2]
    hist = width - 1
    ct, sub = 128, 8

    def history(n, carry):
        ext_ref[0:hist, :] = st_ref[0, n]
        ext_ref[hist:hist + t, :] = glu_ref[n]
        ns_ref[0, n] = ext_ref[t:t + hist, :]
        return carry

    lax.fori_loop(0, nb, history, 0, unroll=2)

    def group(gi, carry):
        seqs = pl.ds(pl.multiple_of(gi * sub, sub), sub)
        for c in range(d // ct):
            cs = slice(c * ct, (c + 1) * ct)
            x = [st_ref[0, seqs, j, cs] for j in range(hist)] + [glu_ref[seqs, j, cs] for j in range(t)]
            for tt in range(t):
                acc = jnp.broadcast_to(cb_ref[:, cs], (sub, ct))
                for w in range(width):
                    acc = acc + k_ref[w:w + 1, cs] * x[tt + w]
                conv_ref[seqs, tt, cs] = acc
        return carry

    lax.fori_loop(0, nb // sub, group, 0)
    _ln_silu_proj(conv_ref, act_ref, lg_ref, lb_ref, w_ref, o_ref, nb * t)


def _conv_sample(state, layer, glu, conv_k, conv_b, ln_g, ln_b, w, nb=32):
    _, n, hist, d = state.shape
    width = conv_k.shape[0]
    t = glu.shape[0] // n
    assert hist == width - 1 and t <= hist
    const = lambda i: (0, 0)
    return pl.pallas_call(
        functools.partial(_conv_sample_kernel, nb=nb, t=t, width=width),
        grid=(n // nb,),
        in_specs=[
            pl.BlockSpec((1, nb, hist, d), lambda i: (layer, i, 0, 0)),
            pl.BlockSpec((nb, t, d), lambda i: (i, 0, 0)),
            pl.BlockSpec((width, d), const),
            pl.BlockSpec((1, d), const),
            pl.BlockSpec((1, d), const),
            pl.BlockSpec((1, d), const),
            pl.BlockSpec((d, d), const),
        ],
        out_specs=[
            pl.BlockSpec((nb * t, d), lambda i: (i, 0)),
            pl.BlockSpec((1, nb, hist, d), lambda i: (0, i, 0, 0)),
        ],
        out_shape=[
            jax.ShapeDtypeStruct((n * t, d), _f32),
            jax.ShapeDtypeStruct((1, n, hist, d), _f32),
        ],
        scratch_shapes=[pltpu.VMEM((hist + t + (-(hist + t)) % 8, d), _f32), pltpu.VMEM((nb, t, d), _f32),
                        pltpu.VMEM((nb * t, d), _bf16)],
        compiler_params=_params("arbitrary"),
        name="conv_sample",
    )(state, glu.reshape(n, t, d), conv_k, conv_b, ln_g, ln_b, w)


def _attn_prompt_kernel(q_ref, kc_ref, kp_ref, vc_ref, vp_ref, bias_ref, sink_ref, o_ref, *, kv_heads, hd):
    blk = q_ref.shape[0]
    grp = 2 * hd
    pairs = q_ref.shape[1] // (kv_heads * grp)
    cols = pairs * blk
    scale = hd ** -0.5
    nt = (((1,), (1,)), ((), ()))
    low = lax.broadcasted_iota(jnp.int32, (2 * blk, grp), 1) < hd
    qi = lax.broadcasted_iota(jnp.int32, (blk, cols), 1) & (blk - 1)
    cur_visible = lax.broadcasted_iota(jnp.int32, (blk, cols), 0) <= qi

    def halves(p_ref, c_ref, h):
        g0 = (h // 2) * grp
        x = jnp.concatenate([p_ref[:, g0:g0 + grp], c_ref[:, g0:g0 + grp]], axis=0)
        swapped = pltpu.roll(x, hd, axis=1)
        in_low, in_high = (x, swapped) if h % 2 == 0 else (swapped, x)
        return jnp.where(low, in_low, 0.0), jnp.where(low, 0.0, in_high)

    for h in range(kv_heads):
        ks = [x.astype(_bf16) for x in halves(kp_ref, kc_ref, h)]
        vts = [x.T.astype(_bf16) for x in halves(vp_ref, vc_ref, h)]
        lanes = [slice((h * pairs + p) * grp, (h * pairs + p + 1) * grp) for p in range(pairs)]
        qs = (jnp.concatenate([q_ref[:, ls] for ls in lanes], axis=0) * scale).astype(_bf16)
        out_t = None
        for parity in range(2):
            s2 = lax.dot_general(ks[parity], qs, nt, preferred_element_type=_f32)
            s = jnp.where(cur_visible, s2[blk:, :], s2[:blk, :]) + bias_ref[0, h, parity]
            sink = sink_ref[h, parity]
            m = jnp.maximum(jnp.max(s, axis=0, keepdims=True), sink)
            p = jnp.exp(s - m)
            den = jnp.sum(p, axis=0, keepdims=True) + jnp.exp(sink - m)
            pn = p * (1.0 / den)
            p2 = jnp.concatenate([jnp.where(cur_visible, 0.0, pn), jnp.where(cur_visible, pn, 0.0)], axis=0)
            o_t = _dot(vts[parity], p2.astype(_bf16))
            out_t = o_t if out_t is None else out_t + o_t
        out = out_t.T
        for p, ls in enumerate(lanes):
            o_ref[:, ls] = out[p * blk:(p + 1) * blk, :]


def _attn_prompt(qkv, bias, sinks, nseq, n_heads, kv_heads, hd):
    m = qkv.shape[0]
    dq, dkv = n_heads * hd, kv_heads * hd
    nb = m // nseq // WINDOW
    pairs = n_heads // kv_heads // 2
    rows = pairs * WINDOW
    assert WINDOW & (WINDOW - 1) == 0 and 2 * hd == 128 and kv_heads % 2 == 0
    qi = jnp.arange(WINDOW)[:, None]
    cur_visible = jnp.arange(WINDOW)[None, :] <= qi
    folded = jnp.stack([jnp.where(cur_visible, bias[:, :, WINDOW:], MASKED),
                        jnp.where(cur_visible, bias[:, :, WINDOW:], bias[:, :, :WINDOW])])
    folded = folded.reshape(2, kv_heads, pairs, 2, WINDOW, WINDOW).transpose(0, 1, 3, 5, 2, 4)
    folded = folded.reshape(2, kv_heads, 2, WINDOW, rows)
    sink_rows = jnp.repeat(sinks.reshape(kv_heads, pairs, 2).transpose(0, 2, 1), WINDOW, axis=-1)
    sink_rows = sink_rows.reshape(kv_heads, 2, 1, rows)
    kcol, vcol = dq // dkv, dq // dkv + 1
    cur = lambda b, i: b * nb + i
    prev = lambda b, i: b * nb + jnp.maximum(i - 1, 0)
    return pl.pallas_call(
        functools.partial(_attn_prompt_kernel, kv_heads=kv_heads, hd=hd),
        grid=(nseq, nb),
        in_specs=[
            pl.BlockSpec((WINDOW, dq), lambda b, i: (cur(b, i), 0)),
            pl.BlockSpec((WINDOW, dkv), lambda b, i: (cur(b, i), kcol)),
            pl.BlockSpec((WINDOW, dkv), lambda b, i: (prev(b, i), kcol)),
            pl.BlockSpec((WINDOW, dkv), lambda b, i: (cur(b, i), vcol)),
            pl.BlockSpec((WINDOW, dkv), lambda b, i: (prev(b, i), vcol)),
            pl.BlockSpec((1, kv_heads, 2, WINDOW, rows), lambda b, i: (jnp.minimum(i, 1), 0, 0, 0, 0)),
            pl.BlockSpec((kv_heads, 2, 1, rows), lambda b, i: (0, 0, 0, 0)),
        ],
        out_specs=pl.BlockSpec((WINDOW, dq), lambda b, i: (cur(b, i), 0)),
        out_shape=jax.ShapeDtypeStruct((m, dq), _f32),
        compiler_params=_params("arbitrary", "arbitrary"),
        name="attn_prompt",
    )(qkv, qkv, qkv, qkv, qkv, folded, sink_rows)


def _attn_sample_kernel(qkv_ref, ck_ref, cv_ref, bias_ref, sink_ref, o_ref, ko_ref, vo_ref, *,
                        nb, t, n_heads, kv_heads, hd):
    dq, dkv = n_heads * hd, kv_heads * hd
    group = n_heads // kv_heads
    keep = ck_ref.shape[1]
    keys = bias_ref.shape[0]
    grp = 2 * hd
    scale = hd ** -0.5
    nt = (((1,), (1,)), ((), ()))
    low = lax.broadcasted_iota(jnp.int32, (t, grp), 1) < hd
    blank = jnp.zeros((t, grp), _f32)
    pad = jnp.zeros((keys - keep - t, dkv), _f32)

    def body(n, carry):
        r0 = pl.multiple_of(n * t, t)
        q = qkv_ref[pl.ds(r0, t), 0:dq]
        kn = qkv_ref[pl.ds(r0, t), dq:dq + dkv]
        vn = qkv_ref[pl.ds(r0, t), dq + dkv:dq + 2 * dkv]
        ck, cv = ck_ref[n], cv_ref[n]
        ko_ref[n, 0:keep - t, :] = ck[t:, :]
        ko_ref[n, keep - t:keep, :] = kn
        vo_ref[n, 0:keep - t, :] = cv[t:, :]
        vo_ref[n, keep - t:keep, :] = vn
        kp = jnp.concatenate([ck, kn, pad], axis=0).astype(_bf16)
        vp = jnp.concatenate([cv, vn, pad], axis=0).astype(_bf16)

        rows = []
        for h in range(kv_heads):
            for pair in range(group // 2):
                qv = q[:, (h * (group // 2) + pair) * grp:(h * (group // 2) + pair + 1) * grp]
                swapped = pltpu.roll(qv, hd, axis=1)
                for parity in range(2):
                    x = qv if parity == h % 2 else swapped
                    x = jnp.where(low, x, 0.0) if h % 2 == 0 else jnp.where(low, 0.0, x)
                    pieces = [blank] * (dkv // grp)
                    pieces[h // 2] = x
                    rows.append(jnp.concatenate(pieces, axis=1))
        wt = (jnp.concatenate(rows, axis=0) * scale).astype(_bf16)

        st = lax.dot_general(kp, wt, nt, preferred_element_type=_f32) + bias_ref[...]
        sink = sink_ref[...]
        m = jnp.maximum(jnp.max(st, axis=0, keepdims=True), sink)
        p = jnp.exp(st - m)
        den = jnp.sum(p, axis=0, keepdims=True) + jnp.exp(sink - m)
        pn = p * (1.0 / den)
        of = _dot(pn.T.astype(_bf16), vp)

        outs = []
        for h in range(kv_heads):
            ls = slice((h // 2) * grp, (h // 2 + 1) * grp)
            for pair in range(group // 2):
                r = (h * group + 2 * pair) * t
                even, odd = of[r:r + t, ls], of[r + t:r + 2 * t, ls]
                if h % 2 == 0:
                    outs.append(jnp.where(low, even, pltpu.roll(odd, hd, axis=1)))
                else:
                    outs.append(jnp.where(low, pltpu.roll(even, hd, axis=1), odd))
        o_ref[pl.ds(r0, t), :] = jnp.concatenate(outs, axis=1)
        return carry

    lax.fori_loop(0, nb, body, 0, unroll=2)


def _attn_sample(qkv, cache_k, cache_v, bias, sinks, t, n_heads, kv_heads, hd, nb=8):
    n, keep, dkv = cache_k.shape
    dq = n_heads * hd
    keys = bias.shape[2]
    assert keep == WINDOW and keep + t <= keys and 2 * hd == 128 and kv_heads % 2 == 0
    bias_t = bias[:, :t].transpose(2, 0, 1).reshape(keys, n_heads * t)
    sink_row = jnp.repeat(sinks, t).reshape(1, n_heads * t)
    const = lambda i: (0, 0)
    return pl.pallas_call(
        functools.partial(_attn_sample_kernel, nb=nb, t=t, n_heads=n_heads, kv_heads=kv_heads, hd=hd),
        grid=(n // nb,),
        in_specs=[
            pl.BlockSpec((nb * t, qkv.shape[1]), lambda i: (i, 0)),
            pl.BlockSpec((nb, keep, dkv), lambda i: (i, 0, 0)),
            pl.BlockSpec((nb, keep, dkv), lambda i: (i, 0, 0)),
            pl.BlockSpec((keys, n_heads * t), const),
            pl.BlockSpec((1, n_heads * t), const),
        ],
        out_specs=[
            pl.BlockSpec((nb * t, dq), lambda i: (i, 0)),
            pl.BlockSpec((nb, keep, dkv), lambda i: (i, 0, 0)),
            pl.BlockSpec((nb, keep, dkv), lambda i: (i, 0, 0)),
        ],
        out_shape=[
            jax.ShapeDtypeStruct((n * t, dq), _f32),
            jax.ShapeDtypeStruct((n, keep, dkv), _f32),
            jax.ShapeDtypeStruct((n, keep, dkv), _f32),
        ],
        compiler_params=_params("arbitrary"),
        name="attn_sample",
    )(qkv, cache_k, cache_v, bias_t, sink_row)


def _mix_kernel(attn_ref, conv_ref, gc_ref, ga_ref, x_ref, wa_ref, wo_ref, g_ref, h_ref):
    ao = _dot(attn_ref[...].astype(_bf16), wa_ref[...])
    mixed = gc_ref[...] * conv_ref[...] + ga_ref[...] * ao
    o = _dot(mixed.astype(_bf16), wo_ref[...])
    h_ref[...] = x_ref[...] + _rms(o, g_ref[...])


def _mix(attn, conv_out, gates, x, w_attn, w_out, g_post, tm=256):
    m, d = x.shape
    row = lambda i: (i, 0)
    const = lambda i: (0, 0)
    once = pl.Buffered(1)
    return pl.pallas_call(
        _mix_kernel,
        grid=(m // tm,),
        in_specs=[
            pl.BlockSpec((tm, d), row),
            pl.BlockSpec((tm, d), row),
            pl.BlockSpec((tm, d), lambda i: (i, 0)),
            pl.BlockSpec((tm, d), lambda i: (i, 1)),
            pl.BlockSpec((tm, d), row),
            pl.BlockSpec((d, d), const, pipeline_mode=once),
            pl.BlockSpec((d, d), const, pipeline_mode=once),
            pl.BlockSpec((1, d), const),
        ],
        out_specs=pl.BlockSpec((tm, d), row),
        out_shape=jax.ShapeDtypeStruct((m, d), _f32),
        compiler_params=_params("arbitrary"),
        name="mix",
    )(attn, conv_out, gates, gates, x, w_attn, w_out, g_post)


def _gelu_tanh(x):
    return 0.5 * x * (1.0 + jnp.tanh(math.sqrt(2.0 / math.pi) * (x + 0.044715 * (x * x * x))))


MXU_COLUMNS = 256
FFN_CHUNK = 1024


def _chunks(n, width=MXU_COLUMNS):
    return [slice(a, a + width) for a in range(0, n, width)]


def _ffn_chunks(tc, act_ref, wd_ref, acc_ref, gate_val):
    for cs in _chunks(tc):
        gate, val = gate_val(cs)
        act_ref[:, cs] = (_gelu_tanh(gate) * val).astype(_bf16)
    acc_ref[...] += _dot(act_ref[...], wd_ref[...])


def _ffn_finish(c, nc, acc_ref, h_ref, gp_ref, y_ref):
    @pl.when(c == nc - 1)
    def _():
        y_ref[...] = h_ref[...] + _rms(acc_ref[...], gp_ref[...])


def _ffn_prompt_kernel(h_ref, g_ref, w_ref, kg_ref, kv_ref, bg_ref, bv_ref, wd_ref, gp_ref,
                       y_ref, tg_ref, tv_ref, hn_ref, acc_ref, act_ref, ug_ref, uv_ref, cg_ref, cv_ref, *, tm, nc):
    i, c = pl.program_id(1), pl.program_id(2)
    tc = w_ref.shape[2] // 2

    @pl.when(c == 0)
    def _():
        hn_ref[...] = _rms(h_ref[...], g_ref[...]).astype(_bf16)
        acc_ref[...] = jnp.zeros_like(acc_ref)

    @pl.when(i == 0)
    def _():
        ug_ref[0:8, :] = jnp.zeros((8, tc), _f32)
        uv_ref[0:8, :] = jnp.zeros((8, tc), _f32)

    @pl.when(i > 0)
    def _():
        ug_ref[0:8, :] = cg_ref[c]
        uv_ref[0:8, :] = cv_ref[c]

    def conv(u_ref, col0, k_ref, b_ref, cs):
        u_ref[8:8 + tm, cs] = _dot(hn_ref[...], w_ref[0, :, col0 + cs.start:col0 + cs.stop])
        return (k_ref[2:3, cs] * u_ref[8:8 + tm, cs] + k_ref[1:2, cs] * u_ref[7:7 + tm, cs]
                + k_ref[0:1, cs] * u_ref[6:6 + tm, cs] + b_ref[:, cs])

    _ffn_chunks(tc, act_ref, wd_ref, acc_ref,
                lambda cs: (conv(ug_ref, 0, kg_ref, bg_ref, cs), conv(uv_ref, tc, kv_ref, bv_ref, cs)))
    cg_ref[c] = ug_ref[tm:tm + 8, :]
    cv_ref[c] = uv_ref[tm:tm + 8, :]
    tg_ref[0, 0] = ug_ref[tm + 6:tm + 8, :]
    tv_ref[0, 0] = uv_ref[tm + 6:tm + 8, :]
    _ffn_finish(c, nc, acc_ref, h_ref, gp_ref, y_ref)


def _ffn_prompt(h, g_pre, w_up, ffn_k, ffn_b, w_down, g_post, nseq, tm=512):
    m, d = h.shape
    dff = w_down.shape[0]
    nc, tc = w_up.shape[0], w_up.shape[2] // 2
    nt = m // nseq // tm
    width = ffn_k.shape[0]
    assert width == 3 and nc * tc == dff
    row = lambda b, i, c: (b * nt + i, 0)
    const = lambda b, i, c: (0, 0)
    lo = lambda b, i, c: (0, c)
    hi = lambda b, i, c: (0, c + nc)
    y, tg, tv = pl.pallas_call(
        functools.partial(_ffn_prompt_kernel, tm=tm, nc=nc),
        grid=(nseq, nt, nc),
        in_specs=[
            pl.BlockSpec((tm, d), row),
            pl.BlockSpec((1, d), const),
            pl.BlockSpec((1, d, 2 * tc), lambda b, i, c: (c, 0, 0)),
            pl.BlockSpec((width, tc), lo),
            pl.BlockSpec((width, tc), hi),
            pl.BlockSpec((1, tc), lo),
            pl.BlockSpec((1, tc), hi),
            pl.BlockSpec((tc, d), lambda b, i, c: (c, 0)),
            pl.BlockSpec((1, d), const),
        ],
        out_specs=[
            pl.BlockSpec((tm, d), row, pipeline_mode=pl.Buffered(1)),
            pl.BlockSpec((1, 1, width - 1, tc), lambda b, i, c: (b, i, 0, c)),
            pl.BlockSpec((1, 1, width - 1, tc), lambda b, i, c: (b, i, 0, c)),
        ],
        out_shape=[
            jax.ShapeDtypeStruct((m, d), _f32),
            jax.ShapeDtypeStruct((nseq, nt, width - 1, dff), _f32),
            jax.ShapeDtypeStruct((nseq, nt, width - 1, dff), _f32),
        ],
        scratch_shapes=[
            pltpu.VMEM((tm, d), _bf16),
            pltpu.VMEM((tm, d), _f32),
            pltpu.VMEM((tm, tc), _bf16),
            pltpu.VMEM((tm + 8, tc), _f32),
            pltpu.VMEM((tm + 8, tc), _f32),
            pltpu.VMEM((nc, 8, tc), _f32),
            pltpu.VMEM((nc, 8, tc), _f32),
        ],
        compiler_params=_params("arbitrary", "arbitrary", "arbitrary"),
        name="ffn_prompt",
    )(h, g_pre, w_up, ffn_k, ffn_k, ffn_b, ffn_b, w_down, g_post)
    return y, jnp.concatenate([tg[:, -1], tv[:, -1]], axis=-1)


def _ffn_sample_kernel(h_ref, g_ref, w_ref, kg_ref, kv_ref, bg_ref, bv_ref, sg_ref, sv_ref, wd_ref, gp_ref,
                       y_ref, tg_ref, tv_ref, hn_ref, acc_ref, act_ref, ug_ref, uv_ref, *, nb, t, nc):
    c = pl.program_id(1)
    tc = w_ref.shape[2] // 2

    @pl.when(c == 0)
    def _():
        hn_ref[...] = _rms(h_ref[...], g_ref[...]).astype(_bf16)
        acc_ref[...] = jnp.zeros_like(acc_ref)

    ug_ref[:, 6:8, :] = sg_ref[...]
    uv_ref[:, 6:8, :] = sv_ref[...]

    def conv(u_ref, col0, k_ref, b_ref, cs):
        width = cs.stop - cs.start
        u_ref[:, 8:8 + t, cs] = _dot(hn_ref[...], w_ref[0, :, col0 + cs.start:col0 + cs.stop]).reshape(nb, t, width)
        cv = (k_ref[2:3, cs] * u_ref[:, 8:8 + t, cs] + k_ref[1:2, cs] * u_ref[:, 7:7 + t, cs]
              + k_ref[0:1, cs] * u_ref[:, 6:6 + t, cs] + b_ref[:, cs])
        return cv.reshape(nb * t, width)

    _ffn_chunks(tc, act_ref, wd_ref, acc_ref,
                lambda cs: (conv(ug_ref, 0, kg_ref, bg_ref, cs), conv(uv_ref, tc, kv_ref, bv_ref, cs)))
    tg_ref[...] = ug_ref[:, 6 + t:8 + t, :]
    tv_ref[...] = uv_ref[:, 6 + t:8 + t, :]
    _ffn_finish(c, nc, acc_ref, h_ref, gp_ref, y_ref)


def _ffn_sample(h, state, g_pre, w_up, ffn_k, ffn_b, w_down, g_post, t, nb=64):
    m, d = h.shape
    n = m // t
    dff = w_down.shape[0]
    nc, tc = w_up.shape[0], w_up.shape[2] // 2
    width = ffn_k.shape[0]
    assert width == 3 and t >= width - 1 and nc * tc == dff
    tm = nb * t
    row = lambda i, c: (i, 0)
    const = lambda i, c: (0, 0)
    lo = lambda i, c: (0, c)
    hi = lambda i, c: (0, c + nc)
    y, tg, tv = pl.pallas_call(
        functools.partial(_ffn_sample_kernel, nb=nb, t=t, nc=nc),
        grid=(n // nb, nc),
        in_specs=[
            pl.BlockSpec((tm, d), row, pipeline_mode=pl.Buffered(1)),
            pl.BlockSpec((1, d), const),
            pl.BlockSpec((1, d, 2 * tc), lambda i, c: (c, 0, 0)),
            pl.BlockSpec((width, tc), lo),
            pl.BlockSpec((width, tc), hi),
            pl.BlockSpec((1, tc), lo),
            pl.BlockSpec((1, tc), hi),
            pl.BlockSpec((nb, width - 1, tc), lambda i, c: (i, 0, c)),
            pl.BlockSpec((nb, width - 1, tc), lambda i, c: (i, 0, c + nc)),
            pl.BlockSpec((tc, d), lambda i, c: (c, 0)),
            pl.BlockSpec((1, d), const),
        ],
        out_specs=[
            pl.BlockSpec((tm, d), row, pipeline_mode=pl.Buffered(1)),
            pl.BlockSpec((nb, width - 1, tc), lambda i, c: (i, 0, c)),
            pl.BlockSpec((nb, width - 1, tc), lambda i, c: (i, 0, c)),
        ],
        out_shape=[
            jax.ShapeDtypeStruct((m, d), _f32),
            jax.ShapeDtypeStruct((n, width - 1, dff), _f32),
            jax.ShapeDtypeStruct((n, width - 1, dff), _f32),
        ],
        scratch_shapes=[
            pltpu.VMEM((tm, d), _bf16),
            pltpu.VMEM((tm, d), _f32),
            pltpu.VMEM((tm, tc), _bf16),
            pltpu.VMEM((nb, 8 + t, tc), _f32),
            pltpu.VMEM((nb, 8 + t, tc), _f32),
        ],
        compiler_params=_params("arbitrary", "arbitrary"),
        name="ffn_sample",
    )(h, g_pre, w_up, ffn_k, ffn_k, ffn_b, ffn_b, state, state, w_down, g_post)
    return y, jnp.concatenate([tg, tv], axis=-1)


def _rel_bucket(dist):
    max_exact = N_BUCKETS // 2
    dd = jnp.maximum(dist, 1).astype(_f32)
    large = max_exact + (jnp.log(dd / max_exact) / math.log(MAX_DISTANCE / max_exact)
                         * (N_BUCKETS - max_exact)).astype(jnp.int32)
    large = jnp.minimum(large, N_BUCKETS - 1)
    return jnp.where(dist < max_exact, dist, large)


def _bias_table(rel_bias):
    n = WINDOW
    by_dist = rel_bias[_rel_bucket(jnp.arange(n))].astype(_f32).T
    g = jnp.concatenate([jnp.full_like(by_dist, MASKED), by_dist[:, ::-1],
                         jnp.full_like(by_dist, MASKED)], axis=1)
    heads, length = g.shape
    skew = jnp.broadcast_to(g[:, None, :], (heads, n, length)).reshape(heads, n * length)
    skew = skew[:, :n * (length - 1)].reshape(heads, n, length - 1)
    return skew[:, :, n - 1:3 * n - 1]


def _row(v):
    return v.reshape(1, -1)


def _layer(xp, xs, cache_k, cache_v, state_conv, layer, st_ffn, p):
    (g_pre, w_in, b_in, conv_k, conv_b, ln_g, ln_b, w_conv, sinks, rel_bias, w_attn, w_out, g_post,
     g_ffn_pre, w_up, ffn_k, ffn_b, w_down, g_ffn_post) = p
    nseq, seq, d = xp.shape
    ns, t, _ = xs.shape
    n_heads = sinks.shape[0]
    keep, kv_heads, hd = cache_k.shape[1:]
    group = n_heads // kv_heads
    dq, dkv = n_heads * hd, kv_heads * hd
    d_conv = conv_k.shape[1]
    assert d_conv == d and dq == d and keep == WINDOW and seq % WINDOW == 0

    w_conv_b, w_attn_b, w_out_b, w_down_b = (w.astype(_bf16) for w in (w_conv, w_attn, w_out, w_down))
    w_in_b = _column_tiles(w_in, 512)
    w_up_b = _column_tiles(w_up, min(FFN_CHUNK, w_down.shape[0]), groups=2)
    g_pre, b_in, conv_b, ln_g, ln_b, g_post, g_ffn_pre, ffn_b, g_ffn_post = map(
        _row, (g_pre, b_in, conv_b, ln_g, ln_b, g_post, g_ffn_pre, ffn_b, g_ffn_post))

    bias = _bias_table(rel_bias)

    outs = []
    for x3, is_prompt in ((xp, True), (xs, False)):
        n = x3.shape[0]
        x = x3.reshape(-1, d)
        glu, qkv, gates = _inproj(x, g_pre, w_in_b, b_in, d_conv, dq + 2 * dkv, 2 * d)
        if is_prompt:
            conv_out = _conv_prompt(glu, conv_k, conv_b, ln_g, ln_b, w_conv_b, nseq)
            attn = _attn_prompt(qkv, bias, sinks.astype(_f32), nseq, n_heads, kv_heads, hd)
            new_conv = glu.reshape(n, seq, d)[:, seq - (conv_k.shape[0] - 1):]
            last = qkv.reshape(n, seq, -1)[:, seq - keep:]
            new_k = last[:, :, dq:dq + dkv].reshape(n, keep, kv_heads, hd)
            new_v = last[:, :, dq + dkv:].reshape(n, keep, kv_heads, hd)
        else:
            conv_out, new_conv = _conv_sample(state_conv, layer, glu, conv_k, conv_b, ln_g, ln_b, w_conv_b)
            new_conv = new_conv[0]
            attn, new_k, new_v = _attn_sample(qkv, cache_k.reshape(n, keep, dkv), cache_v.reshape(n, keep, dkv), bias,
                                              sinks.astype(_f32), t, n_heads, kv_heads, hd)
            new_k, new_v = (a.reshape(n, keep, kv_heads, hd) for a in (new_k, new_v))
        h = _mix(attn, conv_out, gates, x, w_attn_b, w_out_b, g_post)
        if is_prompt:
            y, new_ffn = _ffn_prompt(h, g_ffn_pre, w_up_b, ffn_k, ffn_b, w_down_b, g_ffn_post, nseq)
        else:
            y, new_ffn = _ffn_sample(h, st_ffn, g_ffn_pre, w_up_b, ffn_k, ffn_b, w_down_b, g_ffn_post, t)
        outs.append((y.reshape(x3.shape), new_k, new_v, new_conv, new_ffn))
    return outs


def kernel(x_prompt, x_sample, cache_k, cache_v, state_conv, state_ffn_conv, norm_mix_pre, w_in, b_in, conv_dw_k, conv_dw_b, conv_ln_g, conv_ln_b, w_conv_proj, attn_sinks, rel_bias, w_attn_proj, w_out, norm_mix_post, norm_ffn_pre, w_up, ffn_dw_k, ffn_dw_b, w_down, norm_ffn_post):
    y_p, y_s = x_prompt, x_sample
    per_layer = []
    for l in range(w_in.shape[0]):
        p = (norm_mix_pre[l], w_in[l], b_in[l], conv_dw_k[l], conv_dw_b[l], conv_ln_g[l], conv_ln_b[l], w_conv_proj[l],
             attn_sinks[l], rel_bias, w_attn_proj[l], w_out[l], norm_mix_post[l],
             norm_ffn_pre[l], w_up[l], ffn_dw_k[l], ffn_dw_b[l], w_down[l], norm_ffn_post[l])
        (y_p, *rest_p), (y_s, *rest_s) = _layer(y_p, y_s, cache_k[l], cache_v[l], state_conv, l, state_ffn_conv[l], p)
        per_layer.append(rest_p + rest_s)
    stacked = [jnp.stack(leaves) for leaves in zip(*per_layer)]
    return (y_p, y_s, *stacked)
```

```python
import functools
import math

import jax
import jax.numpy as jnp
from jax import lax
from jax.experimental import pallas as pl
from jax.experimental.pallas import tpu as pltpu

EPS = 1e-6
WINDOW = 128
N_BUCKETS = 32
MAX_DISTANCE = 128
MASKED = -1e30
VMEM_LIMIT_BYTES = 60 * 1024 * 1024
HALO = 32
_bf16 = jnp.bfloat16
_f32 = jnp.float32


def _params(*sem):
    return pltpu.CompilerParams(dimension_semantics=sem, vmem_limit_bytes=VMEM_LIMIT_BYTES)


def _dot(a, b):
    return jnp.dot(a, b, preferred_element_type=_f32)


def _sigmoid(x):
    return 0.5 * (1.0 + jnp.tanh(0.5 * x))


def _rms(x, g):
    return x * lax.rsqrt(jnp.mean(x * x, axis=-1, keepdims=True) + EPS) * g


def _inproj_kernel(*refs, n_glu, n_qkv, side):
    if side:
        (x_ref, g_ref, wa_ref, wb_ref, ba_ref, bb_ref, side_ref,
         glu_ref, qkv_ref, gate_ref, side_out_ref, xn_ref) = refs
        side_out_ref[0] = side_ref[...].astype(_bf16)
    else:
        x_ref, g_ref, wa_ref, wb_ref, ba_ref, bb_ref, glu_ref, qkv_ref, gate_ref, xn_ref = refs
    j = pl.program_id(1)

    @pl.when(j == 0)
    def _():
        xn_ref[...] = _rms(x_ref[...], g_ref[...]).astype(_bf16)

    half = wb_ref.shape[2] // 2
    halves = [slice(0, half), slice(half, 2 * half)]

    def u(w_ref, b_ref, cs):
        return _dot(xn_ref[...], w_ref[0, :, cs]) + b_ref[:, cs]

    @pl.when(j < n_glu)
    def _():
        for cs in halves:
            glu_ref[:, cs] = u(wa_ref, ba_ref, cs) * _sigmoid(u(wb_ref, bb_ref, cs))

    @pl.when((j >= n_glu) & (j < n_glu + n_qkv))
    def _():
        for cs in halves:
            qkv_ref[:, cs] = u(wb_ref, bb_ref, cs)

    @pl.when(j >= n_glu + n_qkv)
    def _():
        for cs in halves:
            gate_ref[:, cs] = _sigmoid(u(wb_ref, bb_ref, cs))


def _cast_tile_kernel(w_ref, o_ref):
    o_ref[0] = w_ref[...].astype(_bf16)


def _column_tiles(w, tn, groups=1):
    d, n = w.shape
    nt = n // groups // tn
    return pl.pallas_call(
        _cast_tile_kernel,
        grid=(nt, groups),
        in_specs=[pl.BlockSpec((d, tn), lambda c, g: (0, g * nt + c))],
        out_specs=pl.BlockSpec((1, d, tn), lambda c, g: (c, 0, g)),
        out_shape=jax.ShapeDtypeStruct((nt, d, groups * tn), _bf16),
        compiler_params=_params("arbitrary", "arbitrary"),
        name="weight_tiles",
    )(w)


def _side_block_width(n_cols, tile, steps):
    bw = 128
    while bw < tile and n_cols // bw > steps:
        bw *= 2
    return bw if tile % bw == 0 and n_cols // bw <= steps else None


def _inproj(x, g, w, b, d_glu, d_qkv, d_gate, tm=1024, side=None, side_tile=None):
    m, d = x.shape
    tm = min(tm, m)
    tn = w.shape[2]
    n_glu, n_qkv, n_gate = d_glu // tn, d_qkv // tn, d_gate // tn
    n_col = n_glu + n_qkv + n_gate
    assert w.shape[0] == n_col + n_glu
    first_w = lambda i, j: (jnp.minimum(j, n_glu - 1), 0, 0)
    rest_w = lambda i, j: (j + n_glu, 0, 0)
    first = lambda i, j: (0, jnp.minimum(j, n_glu - 1))
    rest = lambda i, j: (0, j + n_glu)
    in_specs = [
        pl.BlockSpec((tm, d), lambda i, j: (i, 0)),
        pl.BlockSpec((1, d), lambda i, j: (0, 0)),
        pl.BlockSpec((1, d, tn), first_w),
        pl.BlockSpec((1, d, tn), rest_w),
        pl.BlockSpec((1, tn), first),
        pl.BlockSpec((1, tn), rest),
    ]
    out_specs = [
        pl.BlockSpec((tm, tn), lambda i, j: (i, jnp.minimum(j, n_glu - 1))),
        pl.BlockSpec((tm, tn), lambda i, j: (i, jnp.clip(j - n_glu, 0, n_qkv - 1))),
        pl.BlockSpec((tm, tn), lambda i, j: (i, jnp.maximum(j - n_glu - n_qkv, 0))),
    ]
    out_shape = [
        jax.ShapeDtypeStruct((m, d_glu), _f32),
        jax.ShapeDtypeStruct((m, d_qkv), _f32),
        jax.ShapeDtypeStruct((m, d_gate), _f32),
    ]
    operands = [x, g, w, w, b, b]
    bw = None if side is None else _side_block_width(side.shape[1], side_tile, (m // tm) * n_col)
    if bw is not None:
        d2, n2 = side.shape
        nblk, per_tile = n2 // bw, side_tile // bw
        blk = lambda i, j: jnp.minimum(i * n_col + j, nblk - 1)
        in_specs.append(pl.BlockSpec((d2, bw), lambda i, j: (0, blk(i, j))))
        out_specs.append(pl.BlockSpec(
            (1, d2, bw), lambda i, j: ((blk(i, j) % (nblk // 2)) // per_tile, 0,
                                       (blk(i, j) // (nblk // 2)) * per_tile + blk(i, j) % per_tile)))
        out_shape.append(jax.ShapeDtypeStruct((n2 // 2 // side_tile, d2, 2 * side_tile), _bf16))
        operands.append(side)
    outs = pl.pallas_call(
        functools.partial(_inproj_kernel, n_glu=n_glu, n_qkv=n_qkv, side=bw is not None),
        grid=(m // tm, n_col),
        in_specs=in_specs,
        out_specs=out_specs,
        out_shape=out_shape,
        scratch_shapes=[pltpu.VMEM((tm, d), _bf16)],
        compiler_params=_params("arbitrary", "arbitrary"),
        name="inproj",
    )(*operands)
    return outs if bw is not None else (*outs, None)


def _ln_silu_proj(conv_ref, act_ref, lg_ref, lb_ref, w_ref, o_ref, rows):
    rt = 16

    def body(r, carry):
        r0 = pl.multiple_of(r * rt, rt)
        if len(conv_ref.shape) == 2:
            c = conv_ref[pl.ds(r0, rt), :]
        else:
            per = rt // conv_ref.shape[1]
            c = conv_ref[pl.ds(pl.multiple_of(r * per, per), per)].reshape(rt, conv_ref.shape[2])
        mu = jnp.mean(c, axis=-1, keepdims=True)
        cc = c - mu
        var = jnp.mean(cc * cc, axis=-1, keepdims=True)
        y = cc * lax.rsqrt(var + EPS) * lg_ref[...] + lb_ref[...]
        act_ref[pl.ds(r0, rt), :] = (y * _sigmoid(y)).astype(_bf16)
        return carry

    lax.fori_loop(0, rows // rt, body, 0, unroll=4)
    o_ref[...] = _dot(act_ref[...], w_ref[...])


def _conv_prompt_kernel(glu_ref, k_ref, cb_ref, lg_ref, lb_ref, w_ref, o_ref, ext_ref, conv_ref, act_ref, *, tm, width):
    d = glu_ref.shape[1]
    i = pl.program_id(1)

    @pl.when(i == 0)
    def _():
        ext_ref[0:HALO, :] = jnp.zeros((HALO, d), _f32)

    @pl.when(i > 0)
    def _():
        ext_ref[0:HALO, :] = ext_ref[tm:tm + HALO, :]

    ext_ref[HALO:HALO + tm, :] = glu_ref[...]

    rt, ct, sub = 64, 128, 8
    first = HALO - (width - 1)

    def body(r, carry):
        r0 = pl.multiple_of(r * rt, rt)
        for c in range(d // ct):
            cs = slice(c * ct, (c + 1) * ct)
            blk = ext_ref[pl.ds(r0, rt + HALO), cs]
            acc = jnp.broadcast_to(cb_ref[:, cs], (rt, ct))
            for phase in range(sub):
                taps = [w for w in range(width) if (first + w) % sub == phase]
                rows = rt if phase == 0 else rt + sub
                part = None
                for w in taps:
                    lo = first + w - phase
                    term = k_ref[w:w + 1, cs] * blk[lo:lo + rows, :]
                    part = term if part is None else part + term
                acc = acc + part[phase:phase + rt, :]
            conv_ref[pl.ds(r0, rt), cs] = acc
        return carry

    lax.fori_loop(0, tm // rt, body, 0)
    _ln_silu_proj(conv_ref, act_ref, lg_ref, lb_ref, w_ref, o_ref, tm)


def _conv_prompt(glu, conv_k, conv_b, ln_g, ln_b, w, nseq, tm=256):
    m, d = glu.shape
    nt = m // nseq // tm
    width = conv_k.shape[0]
    const = lambda b, i: (0, 0)
    return pl.pallas_call(
        functools.partial(_conv_prompt_kernel, tm=tm, width=width),
        grid=(nseq, nt),
        in_specs=[
            pl.BlockSpec((tm, d), lambda b, i: (b * nt + i, 0)),
            pl.BlockSpec((width, d), const),
            pl.BlockSpec((1, d), const),
            pl.BlockSpec((1, d), const),
            pl.BlockSpec((1, d), const),
            pl.BlockSpec((d, d), const),
        ],
        out_specs=pl.BlockSpec((tm, d), lambda b, i: (b * nt + i, 0)),
        out_shape=jax.ShapeDtypeStruct((m, d), _f32),
        scratch_shapes=[pltpu.VMEM((tm + HALO, d), _f32), pltpu.VMEM((tm, d), _f32), pltpu.VMEM((tm, d), _bf16)],
        compiler_params=_params("arbitrary", "arbitrary"),
        name="conv_prompt",
    )(glu, conv_k, conv_b, ln_g, ln_b, w)


def _conv_sample_kernel(st_ref, glu_ref, k_ref, cb_ref, lg_ref, lb_ref, w_ref, o_ref, ns_ref,
                        ext_ref, conv_ref, act_ref, *, nb, t, width):
    d = glu_ref.shape[2]
    hist = width - 1
    ct, sub = 128, 8

    def history(n, carry):
        ext_ref[0:hist, :] = st_ref[0, n]
        ext_ref[hist:hist + t, :] = glu_ref[n]
        ns_ref[0, n] = ext_ref[t:t + hist, :]
        return carry

    lax.fori_loop(0, nb, history, 0, unroll=2)

    def group(gi, carry):
        seqs = pl.ds(pl.multiple_of(gi * sub, sub), sub)
        for c in range(d // ct):
            cs = slice(c * ct, (c + 1) * ct)
            x = [st_ref[0, seqs, j, cs] for j in range(hist)] + [glu_ref[seqs, j, cs] for j in range(t)]
            for tt in range(t):
                acc = jnp.broadcast_to(cb_ref[:, cs], (sub, ct))
                for w in range(width):
                    acc = acc + k_ref[w:w + 1, cs] * x[tt + w]
                conv_ref[seqs, tt, cs] = acc
        return carry

    lax.fori_loop(0, nb // sub, group, 0)
    _ln_silu_proj(conv_ref, act_ref, lg_ref, lb_ref, w_ref, o_ref, nb * t)


def _conv_sample(state, layer, glu, conv_k, conv_b, ln_g, ln_b, w, nb=32):
    _, n, hist, d = state.shape
    width = conv_k.shape[0]
    t = glu.shape[0] // n
    assert hist == width - 1 and t <= hist
    const = lambda i: (0, 0)
    return pl.pallas_call(
        functools.partial(_conv_sample_kernel, nb=nb, t=t, width=width),
        grid=(n // nb,),
        in_specs=[
            pl.BlockSpec((1, nb, hist, d), lambda i: (layer, i, 0, 0)),
            pl.BlockSpec((nb, t, d), lambda i: (i, 0, 0)),
            pl.BlockSpec((width, d), const),
            pl.BlockSpec((1, d), const),
            pl.BlockSpec((1, d), const),
            pl.BlockSpec((1, d), const),
            pl.BlockSpec((d, d), const),
        ],
        out_specs=[
            pl.BlockSpec((nb * t, d), lambda i: (i, 0)),
            pl.BlockSpec((1, nb, hist, d), lambda i: (0, i, 0, 0)),
        ],
        out_shape=[
            jax.ShapeDtypeStruct((n * t, d), _f32),
            jax.ShapeDtypeStruct((1, n, hist, d), _f32),
        ],
        scratch_shapes=[pltpu.VMEM((hist + t + (-(hist + t)) % 8, d), _f32), pltpu.VMEM((nb, t, d), _f32),
                        pltpu.VMEM((nb * t, d), _bf16)],
        compiler_params=_params("arbitrary"),
        name="conv_sample",
    )(state, glu.reshape(n, t, d), conv_k, conv_b, ln_g, ln_b, w)


def _attn_prompt_kernel(q_ref, kc_ref, kp_ref, vc_ref, vp_ref, bias_ref, sink_ref, *rest, kv_heads, hd, side):
    if side:
        side_ref, o_ref, side_out_ref = rest
        side_out_ref[...] = side_ref[...].astype(_bf16)
    else:
        (o_ref,) = rest
    blk = q_ref.shape[0]
    grp = 2 * hd
    pairs = q_ref.shape[1] // (kv_heads * grp)
    cols = pairs * blk
    scale = hd ** -0.5
    nt = (((1,), (1,)), ((), ()))
    low = lax.broadcasted_iota(jnp.int32, (2 * blk, grp), 1) < hd
    qi = lax.broadcasted_iota(jnp.int32, (blk, cols), 1) & (blk - 1)
    cur_visible = lax.broadcasted_iota(jnp.int32, (blk, cols), 0) <= qi

    def halves(p_ref, c_ref, h):
        g0 = (h // 2) * grp
        x = jnp.concatenate([p_ref[:, g0:g0 + grp], c_ref[:, g0:g0 + grp]], axis=0)
        swapped = pltpu.roll(x, hd, axis=1)
        in_low, in_high = (x, swapped) if h % 2 == 0 else (swapped, x)
        return jnp.where(low, in_low, 0.0), jnp.where(low, 0.0, in_high)

    for h in range(kv_heads):
        ks = [x.astype(_bf16) for x in halves(kp_ref, kc_ref, h)]
        vts = [x.T.astype(_bf16) for x in halves(vp_ref, vc_ref, h)]
        lanes = [slice((h * pairs + p) * grp, (h * pairs + p + 1) * grp) for p in range(pairs)]
        qs = (jnp.concatenate([q_ref[:, ls] for ls in lanes], axis=0) * scale).astype(_bf16)
        out_t = None
        for parity in range(2):
            s2 = lax.dot_general(ks[parity], qs, nt, preferred_element_type=_f32)
            s = jnp.where(cur_visible, s2[blk:, :], s2[:blk, :]) + bias_ref[0, h, parity]
            sink = sink_ref[h, parity]
            m = jnp.maximum(jnp.max(s, axis=0, keepdims=True), sink)
            p = jnp.exp(s - m)
            den = jnp.sum(p, axis=0, keepdims=True) + jnp.exp(sink - m)
            pn = p * (1.0 / den)
            p2 = jnp.concatenate([jnp.where(cur_visible, 0.0, pn), jnp.where(cur_visible, pn, 0.0)], axis=0)
            o_t = _dot(vts[parity], p2.astype(_bf16))
            out_t = o_t if out_t is None else out_t + o_t
        out = out_t.T
        for p, ls in enumerate(lanes):
            o_ref[:, ls] = out[p * blk:(p + 1) * blk, :]


def _attn_prompt(qkv, bias, sinks, nseq, n_heads, kv_heads, hd, side=None):
    m = qkv.shape[0]
    dq, dkv = n_heads * hd, kv_heads * hd
    nb = m // nseq // WINDOW
    pairs = n_heads // kv_heads // 2
    rows = pairs * WINDOW
    assert WINDOW & (WINDOW - 1) == 0 and 2 * hd == 128 and kv_heads % 2 == 0
    qi = jnp.arange(WINDOW)[:, None]
    cur_visible = jnp.arange(WINDOW)[None, :] <= qi
    folded = jnp.stack([jnp.where(cur_visible, bias[:, :, WINDOW:], MASKED),
                        jnp.where(cur_visible, bias[:, :, WINDOW:], bias[:, :, :WINDOW])])
    folded = folded.reshape(2, kv_heads, pairs, 2, WINDOW, WINDOW).transpose(0, 1, 3, 5, 2, 4)
    folded = folded.reshape(2, kv_heads, 2, WINDOW, rows)
    sink_rows = jnp.repeat(sinks.reshape(kv_heads, pairs, 2).transpose(0, 2, 1), WINDOW, axis=-1)
    sink_rows = sink_rows.reshape(kv_heads, 2, 1, rows)
    kcol, vcol = dq // dkv, dq // dkv + 1
    cur = lambda b, i: b * nb + i
    prev = lambda b, i: b * nb + jnp.maximum(i - 1, 0)
    in_specs = [
        pl.BlockSpec((WINDOW, dq), lambda b, i: (cur(b, i), 0)),
        pl.BlockSpec((WINDOW, dkv), lambda b, i: (cur(b, i), kcol)),
        pl.BlockSpec((WINDOW, dkv), lambda b, i: (prev(b, i), kcol)),
        pl.BlockSpec((WINDOW, dkv), lambda b, i: (cur(b, i), vcol)),
        pl.BlockSpec((WINDOW, dkv), lambda b, i: (prev(b, i), vcol)),
        pl.BlockSpec((1, kv_heads, 2, WINDOW, rows), lambda b, i: (jnp.minimum(i, 1), 0, 0, 0, 0)),
        pl.BlockSpec((kv_heads, 2, 1, rows), lambda b, i: (0, 0, 0, 0)),
    ]
    out_specs = [pl.BlockSpec((WINDOW, dq), lambda b, i: (cur(b, i), 0))]
    out_shape = [jax.ShapeDtypeStruct((m, dq), _f32)]
    operands = [qkv, qkv, qkv, qkv, qkv, folded, sink_rows]
    steps = nseq * nb
    ride = side is not None and side.shape[0] % (16 * steps) == 0
    if ride:
        block = (side.shape[0] // steps, side.shape[1])
        in_specs.append(pl.BlockSpec(block, lambda b, i: (cur(b, i), 0)))
        out_specs.append(pl.BlockSpec(block, lambda b, i: (cur(b, i), 0)))
        out_shape.append(jax.ShapeDtypeStruct(side.shape, _bf16))
        operands.append(side)
    outs = pl.pallas_call(
        functools.partial(_attn_prompt_kernel, kv_heads=kv_heads, hd=hd, side=ride),
        grid=(nseq, nb),
        in_specs=in_specs,
        out_specs=out_specs,
        out_shape=out_shape,
        compiler_params=_params("arbitrary", "arbitrary"),
        name="attn_prompt",
    )(*operands)
    return (outs[0], outs[1]) if ride else (outs[0], None if side is None else side.astype(_bf16))


def _attn_sample_kernel(qkv_ref, ck_ref, cv_ref, bias_ref, sink_ref, o_ref, ko_ref, vo_ref, *,
                        nb, t, n_heads, kv_heads, hd):
    dq, dkv = n_heads * hd, kv_heads * hd
    group = n_heads // kv_heads
    keep = ck_ref.shape[1]
    keys = bias_ref.shape[0]
    grp = 2 * hd
    scale = hd ** -0.5
    nt = (((1,), (1,)), ((), ()))
    low = lax.broadcasted_iota(jnp.int32, (t, grp), 1) < hd
    blank = jnp.zeros((t, grp), _f32)
    pad = jnp.zeros((keys - keep - t, dkv), _f32)

    def body(n, carry):
        r0 = pl.multiple_of(n * t, t)
        q = qkv_ref[pl.ds(r0, t), 0:dq]
        kn = qkv_ref[pl.ds(r0, t), dq:dq + dkv]
        vn = qkv_ref[pl.ds(r0, t), dq + dkv:dq + 2 * dkv]
        ck, cv = ck_ref[n], cv_ref[n]
        ko_ref[n, 0:keep - t, :] = ck[t:, :]
        ko_ref[n, keep - t:keep, :] = kn
        vo_ref[n, 0:keep - t, :] = cv[t:, :]
        vo_ref[n, keep - t:keep, :] = vn
        kp = jnp.concatenate([ck, kn, pad], axis=0).astype(_bf16)
        vp = jnp.concatenate([cv, vn, pad], axis=0).astype(_bf16)

        rows = []
        for h in range(kv_heads):
            for pair in range(group // 2):
                qv = q[:, (h * (group // 2) + pair) * grp:(h * (group // 2) + pair + 1) * grp]
                swapped = pltpu.roll(qv, hd, axis=1)
                for parity in range(2):
                    x = qv if parity == h % 2 else swapped
                    x = jnp.where(low, x, 0.0) if h % 2 == 0 else jnp.where(low, 0.0, x)
                    pieces = [blank] * (dkv // grp)
                    pieces[h // 2] = x
                    rows.append(jnp.concatenate(pieces, axis=1))
        wt = (jnp.concatenate(rows, axis=0) * scale).astype(_bf16)

        st = lax.dot_general(kp, wt, nt, preferred_element_type=_f32) + bias_ref[...]
        sink = sink_ref[...]
        m = jnp.maximum(jnp.max(st, axis=0, keepdims=True), sink)
        p = jnp.exp(st - m)
        den = jnp.sum(p, axis=0, keepdims=True) + jnp.exp(sink - m)
        pn = p * (1.0 / den)
        of = _dot(pn.T.astype(_bf16), vp)

        outs = []
        for h in range(kv_heads):
            ls = slice((h // 2) * grp, (h // 2 + 1) * grp)
            for pair in range(group // 2):
                r = (h * group + 2 * pair) * t
                even, odd = of[r:r + t, ls], of[r + t:r + 2 * t, ls]
                if h % 2 == 0:
                    outs.append(jnp.where(low, even, pltpu.roll(odd, hd, axis=1)))
                else:
                    outs.append(jnp.where(low, pltpu.roll(even, hd, axis=1), odd))
        o_ref[pl.ds(r0, t), :] = jnp.concatenate(outs, axis=1)
        return carry

    lax.fori_loop(0, nb, body, 0, unroll=2)


def _attn_sample(qkv, cache_k, cache_v, bias, sinks, t, n_heads, kv_heads, hd, nb=8):
    n, keep, dkv = cache_k.shape
    dq = n_heads * hd
    keys = bias.shape[2]
    assert keep == WINDOW and keep + t <= keys and 2 * hd == 128 and kv_heads % 2 == 0
    bias_t = bias[:, :t].transpose(2, 0, 1).reshape(keys, n_heads * t)
    sink_row = jnp.repeat(sinks, t).reshape(1, n_heads * t)
    const = lambda i: (0, 0)
    return pl.pallas_call(
        functools.partial(_attn_sample_kernel, nb=nb, t=t, n_heads=n_heads, kv_heads=kv_heads, hd=hd),
        grid=(n // nb,),
        in_specs=[
            pl.BlockSpec((nb * t, qkv.shape[1]), lambda i: (i, 0)),
            pl.BlockSpec((nb, keep, dkv), lambda i: (i, 0, 0)),
            pl.BlockSpec((nb, keep, dkv), lambda i: (i, 0, 0)),
            pl.BlockSpec((keys, n_heads * t), const),
            pl.BlockSpec((1, n_heads * t), const),
        ],
        out_specs=[
            pl.BlockSpec((nb * t, dq), lambda i: (i, 0)),
            pl.BlockSpec((nb, keep, dkv), lambda i: (i, 0, 0)),
            pl.BlockSpec((nb, keep, dkv), lambda i: (i, 0, 0)),
        ],
        out_shape=[
            jax.ShapeDtypeStruct((n * t, dq), _f32),
            jax.ShapeDtypeStruct((n, keep, dkv), _f32),
            jax.ShapeDtypeStruct((n, keep, dkv), _f32),
        ],
        compiler_params=_params("arbitrary"),
        name="attn_sample",
    )(qkv, cache_k, cache_v, bias_t, sink_row)


def _mix_kernel(attn_ref, conv_ref, gc_ref, ga_ref, x_ref, wa_ref, wo_ref, g_ref, h_ref):
    ao = _dot(attn_ref[...].astype(_bf16), wa_ref[...])
    mixed = gc_ref[...] * conv_ref[...] + ga_ref[...] * ao
    o = _dot(mixed.astype(_bf16), wo_ref[...])
    h_ref[...] = x_ref[...] + _rms(o, g_ref[...])


def _mix(attn, conv_out, gates, x, w_attn, w_out, g_post, tm=256):
    m, d = x.shape
    row = lambda i: (i, 0)
    const = lambda i: (0, 0)
    once = pl.Buffered(1)
    return pl.pallas_call(
        _mix_kernel,
        grid=(m // tm,),
        in_specs=[
            pl.BlockSpec((tm, d), row),
            pl.BlockSpec((tm, d), row),
            pl.BlockSpec((tm, d), lambda i: (i, 0)),
            pl.BlockSpec((tm, d), lambda i: (i, 1)),
            pl.BlockSpec((tm, d), row),
            pl.BlockSpec((d, d), const, pipeline_mode=once),
            pl.BlockSpec((d, d), const, pipeline_mode=once),
            pl.BlockSpec((1, d), const),
        ],
        out_specs=pl.BlockSpec((tm, d), row),
        out_shape=jax.ShapeDtypeStruct((m, d), _f32),
        compiler_params=_params("arbitrary"),
        name="mix",
    )(attn, conv_out, gates, gates, x, w_attn, w_out, g_post)


def _gelu_tanh(x):
    return 0.5 * x * (1.0 + jnp.tanh(math.sqrt(2.0 / math.pi) * (x + 0.044715 * (x * x * x))))


MXU_COLUMNS = 256
FFN_CHUNK = 1024


def _chunks(n, width=MXU_COLUMNS):
    return [slice(a, a + width) for a in range(0, n, width)]


def _ffn_chunks(tc, act_ref, wd_ref, acc_ref, gate_val):
    for cs in _chunks(tc):
        gate, val = gate_val(cs)
        act_ref[:, cs] = (_gelu_tanh(gate) * val).astype(_bf16)
    acc_ref[...] += _dot(act_ref[...], wd_ref[...])


def _ffn_finish(c, nc, acc_ref, h_ref, gp_ref, y_ref):
    @pl.when(c == nc - 1)
    def _():
        y_ref[...] = h_ref[...] + _rms(acc_ref[...], gp_ref[...])


def _ffn_prompt_kernel(h_ref, g_ref, w_ref, kg_ref, kv_ref, bg_ref, bv_ref, wd_ref, gp_ref,
                       y_ref, tg_ref, tv_ref, hn_ref, acc_ref, act_ref, ug_ref, uv_ref, cg_ref, cv_ref, *, tm, nc):
    i, c = pl.program_id(1), pl.program_id(2)
    tc = w_ref.shape[2] // 2

    @pl.when(c == 0)
    def _():
        hn_ref[...] = _rms(h_ref[...], g_ref[...]).astype(_bf16)
        acc_ref[...] = jnp.zeros_like(acc_ref)

    @pl.when(i == 0)
    def _():
        ug_ref[0:8, :] = jnp.zeros((8, tc), _f32)
        uv_ref[0:8, :] = jnp.zeros((8, tc), _f32)

    @pl.when(i > 0)
    def _():
        ug_ref[0:8, :] = cg_ref[c]
        uv_ref[0:8, :] = cv_ref[c]

    def conv(u_ref, col0, k_ref, b_ref, cs):
        u_ref[8:8 + tm, cs] = _dot(hn_ref[...], w_ref[0, :, col0 + cs.start:col0 + cs.stop])
        return (k_ref[2:3, cs] * u_ref[8:8 + tm, cs] + k_ref[1:2, cs] * u_ref[7:7 + tm, cs]
                + k_ref[0:1, cs] * u_ref[6:6 + tm, cs] + b_ref[:, cs])

    _ffn_chunks(tc, act_ref, wd_ref, acc_ref,
                lambda cs: (conv(ug_ref, 0, kg_ref, bg_ref, cs), conv(uv_ref, tc, kv_ref, bv_ref, cs)))
    cg_ref[c] = ug_ref[tm:tm + 8, :]
    cv_ref[c] = uv_ref[tm:tm + 8, :]
    tg_ref[0, 0] = ug_ref[tm + 6:tm + 8, :]
    tv_ref[0, 0] = uv_ref[tm + 6:tm + 8, :]
    _ffn_finish(c, nc, acc_ref, h_ref, gp_ref, y_ref)


def _ffn_prompt(h, g_pre, w_up, ffn_k, ffn_b, w_down, g_post, nseq, tm=512):
    m, d = h.shape
    dff = w_down.shape[0]
    nc, tc = w_up.shape[0], w_up.shape[2] // 2
    nt = m // nseq // tm
    width = ffn_k.shape[0]
    assert width == 3 and nc * tc == dff
    row = lambda b, i, c: (b * nt + i, 0)
    const = lambda b, i, c: (0, 0)
    lo = lambda b, i, c: (0, c)
    hi = lambda b, i, c: (0, c + nc)
    y, tg, tv = pl.pallas_call(
        functools.partial(_ffn_prompt_kernel, tm=tm, nc=nc),
        grid=(nseq, nt, nc),
        in_specs=[
            pl.BlockSpec((tm, d), row),
            pl.BlockSpec((1, d), const),
            pl.BlockSpec((1, d, 2 * tc), lambda b, i, c: (c, 0, 0)),
            pl.BlockSpec((width, tc), lo),
            pl.BlockSpec((width, tc), hi),
            pl.BlockSpec((1, tc), lo),
            pl.BlockSpec((1, tc), hi),
            pl.BlockSpec((tc, d), lambda b, i, c: (c, 0)),
            pl.BlockSpec((1, d), const),
        ],
        out_specs=[
            pl.BlockSpec((tm, d), row, pipeline_mode=pl.Buffered(1)),
            pl.BlockSpec((1, 1, width - 1, tc), lambda b, i, c: (b, i, 0, c)),
            pl.BlockSpec((1, 1, width - 1, tc), lambda b, i, c: (b, i, 0, c)),
        ],
        out_shape=[
            jax.ShapeDtypeStruct((m, d), _f32),
            jax.ShapeDtypeStruct((nseq, nt, width - 1, dff), _f32),
            jax.ShapeDtypeStruct((nseq, nt, width - 1, dff), _f32),
        ],
        scratch_shapes=[
            pltpu.VMEM((tm, d), _bf16),
            pltpu.VMEM((tm, d), _f32),
            pltpu.VMEM((tm, tc), _bf16),
            pltpu.VMEM((tm + 8, tc), _f32),
            pltpu.VMEM((tm + 8, tc), _f32),
            pltpu.VMEM((nc, 8, tc), _f32),
            pltpu.VMEM((nc, 8, tc), _f32),
        ],
        compiler_params=_params("arbitrary", "arbitrary", "arbitrary"),
        name="ffn_prompt",
    )(h, g_pre, w_up, ffn_k, ffn_k, ffn_b, ffn_b, w_down, g_post)
    return y, jnp.concatenate([tg[:, -1], tv[:, -1]], axis=-1)


def _ffn_sample_kernel(h_ref, g_ref, w_ref, kg_ref, kv_ref, bg_ref, bv_ref, sg_ref, sv_ref, wd_ref, gp_ref,
                       y_ref, tg_ref, tv_ref, hn_ref, acc_ref, act_ref, ug_ref, uv_ref, *, nb, t, nc):
    c = pl.program_id(1)
    tc = w_ref.shape[2] // 2

    @pl.when(c == 0)
    def _():
        hn_ref[...] = _rms(h_ref[...], g_ref[...]).astype(_bf16)
        acc_ref[...] = jnp.zeros_like(acc_ref)

    ug_ref[:, 6:8, :] = sg_ref[...]
    uv_ref[:, 6:8, :] = sv_ref[...]

    def conv(u_ref, col0, k_ref, b_ref, cs):
        width = cs.stop - cs.start
        u_ref[:, 8:8 + t, cs] = _dot(hn_ref[...], w_ref[0, :, col0 + cs.start:col0 + cs.stop]).reshape(nb, t, width)
        cv = (k_ref[2:3, cs] * u_ref[:, 8:8 + t, cs] + k_ref[1:2, cs] * u_ref[:, 7:7 + t, cs]
              + k_ref[0:1, cs] * u_ref[:, 6:6 + t, cs] + b_ref[:, cs])
        return cv.reshape(nb * t, width)

    _ffn_chunks(tc, act_ref, wd_ref, acc_ref,
                lambda cs: (conv(ug_ref, 0, kg_ref, bg_ref, cs), conv(uv_ref, tc, kv_ref, bv_ref, cs)))
    tg_ref[...] = ug_ref[:, 6 + t:8 + t, :]
    tv_ref[...] = uv_ref[:, 6 + t:8 + t, :]
    _ffn_finish(c, nc, acc_ref, h_ref, gp_ref, y_ref)


def _ffn_sample(h, state, g_pre, w_up, ffn_k, ffn_b, w_down, g_post, t, nb=64):
    m, d = h.shape
    n = m // t
    dff = w_down.shape[0]
    nc, tc = w_up.shape[0], w_up.shape[2] // 2
    width = ffn_k.shape[0]
    assert width == 3 and t >= width - 1 and nc * tc == dff
    tm = nb * t
    row = lambda i, c: (i, 0)
    const = lambda i, c: (0, 0)
    lo = lambda i, c: (0, c)
    hi = lambda i, c: (0, c + nc)
    y, tg, tv = pl.pallas_call(
        functools.partial(_ffn_sample_kernel, nb=nb, t=t, nc=nc),
        grid=(n // nb, nc),
        in_specs=[
            pl.BlockSpec((tm, d), row, pipeline_mode=pl.Buffered(1)),
            pl.BlockSpec((1, d), const),
            pl.BlockSpec((1, d, 2 * tc), lambda i, c: (c, 0, 0)),
            pl.BlockSpec((width, tc), lo),
            pl.BlockSpec((width, tc), hi),
            pl.BlockSpec((1, tc), lo),
            pl.BlockSpec((1, tc), hi),
            pl.BlockSpec((nb, width - 1, tc), lambda i, c: (i, 0, c)),
            pl.BlockSpec((nb, width - 1, tc), lambda i, c: (i, 0, c + nc)),
            pl.BlockSpec((tc, d), lambda i, c: (c, 0)),
            pl.BlockSpec((1, d), const),
        ],
        out_specs=[
            pl.BlockSpec((tm, d), row, pipeline_mode=pl.Buffered(1)),
            pl.BlockSpec((nb, width - 1, tc), lambda i, c: (i, 0, c)),
            pl.BlockSpec((nb, width - 1, tc), lambda i, c: (i, 0, c)),
        ],
        out_shape=[
            jax.ShapeDtypeStruct((m, d), _f32),
            jax.ShapeDtypeStruct((n, width - 1, dff), _f32),
            jax.ShapeDtypeStruct((n, width - 1, dff), _f32),
        ],
        scratch_shapes=[
            pltpu.VMEM((tm, d), _bf16),
            pltpu.VMEM((tm, d), _f32),
            pltpu.VMEM((tm, tc), _bf16),
            pltpu.VMEM((nb, 8 + t, tc), _f32),
            pltpu.VMEM((nb, 8 + t, tc), _f32),
        ],
        compiler_params=_params("arbitrary", "arbitrary"),
        name="ffn_sample",
    )(h, g_pre, w_up, ffn_k, ffn_k, ffn_b, ffn_b, state, state, w_down, g_post)
    return y, jnp.concatenate([tg, tv], axis=-1)


def _rel_bucket(dist):
    max_exact = N_BUCKETS // 2
    dd = jnp.maximum(dist, 1).astype(_f32)
    large = max_exact + (jnp.log(dd / max_exact) / math.log(MAX_DISTANCE / max_exact)
                         * (N_BUCKETS - max_exact)).astype(jnp.int32)
    large = jnp.minimum(large, N_BUCKETS - 1)
    return jnp.where(dist < max_exact, dist, large)


def _bias_table(rel_bias):
    n = WINDOW
    by_dist = rel_bias[_rel_bucket(jnp.arange(n))].astype(_f32).T
    g = jnp.concatenate([jnp.full_like(by_dist, MASKED), by_dist[:, ::-1],
                         jnp.full_like(by_dist, MASKED)], axis=1)
    heads, length = g.shape
    skew = jnp.broadcast_to(g[:, None, :], (heads, n, length)).reshape(heads, n * length)
    skew = skew[:, :n * (length - 1)].reshape(heads, n, length - 1)
    return skew[:, :, n - 1:3 * n - 1]


def _row(v):
    return v.reshape(1, -1)


def _layer(xp, xs, cache_k, cache_v, state_conv, layer, st_ffn, p):
    (g_pre, w_in, b_in, conv_k, conv_b, ln_g, ln_b, w_conv, sinks, rel_bias, w_attn, w_out, g_post,
     g_ffn_pre, w_up, ffn_k, ffn_b, w_down, g_ffn_post) = p
    nseq, seq, d = xp.shape
    ns, t, _ = xs.shape
    n_heads = sinks.shape[0]
    keep, kv_heads, hd = cache_k.shape[1:]
    group = n_heads // kv_heads
    dq, dkv = n_heads * hd, kv_heads * hd
    d_conv = conv_k.shape[1]
    assert d_conv == d and dq == d and keep == WINDOW and seq % WINDOW == 0

    w_conv_b, w_attn_b, w_out_b = (w.astype(_bf16) for w in (w_conv, w_attn, w_out))
    w_in_b = _column_tiles(w_in, 512)
    ffn_tile = min(FFN_CHUNK, w_down.shape[0])
    w_up_b = None
    g_pre, b_in, conv_b, ln_g, ln_b, g_post, g_ffn_pre, ffn_b, g_ffn_post = map(
        _row, (g_pre, b_in, conv_b, ln_g, ln_b, g_post, g_ffn_pre, ffn_b, g_ffn_post))

    bias = _bias_table(rel_bias)

    outs = []
    for x3, is_prompt in ((xp, True), (xs, False)):
        n = x3.shape[0]
        x = x3.reshape(-1, d)
        glu, qkv, gates, tiles = _inproj(x, g_pre, w_in_b, b_in, d_conv, dq + 2 * dkv, 2 * d,
                                         side=w_up if w_up_b is None else None, side_tile=ffn_tile)
        if w_up_b is None:
            w_up_b = tiles if tiles is not None else _column_tiles(w_up, ffn_tile, groups=2)
        if is_prompt:
            conv_out = _conv_prompt(glu, conv_k, conv_b, ln_g, ln_b, w_conv_b, nseq)
            attn, w_down_b = _attn_prompt(qkv, bias, sinks.astype(_f32), nseq, n_heads, kv_heads, hd, side=w_down)
            new_conv = glu.reshape(n, seq, d)[:, seq - (conv_k.shape[0] - 1):]
            last = qkv.reshape(n, seq, -1)[:, seq - keep:]
            new_k = last[:, :, dq:dq + dkv].reshape(n, keep, kv_heads, hd)
            new_v = last[:, :, dq + dkv:].reshape(n, keep, kv_heads, hd)
        else:
            conv_out, new_conv = _conv_sample(state_conv, layer, glu, conv_k, conv_b, ln_g, ln_b, w_conv_b)
            new_conv = new_conv[0]
            attn, new_k, new_v = _attn_sample(qkv, cache_k.reshape(n, keep, dkv), cache_v.reshape(n, keep, dkv), bias,
                                              sinks.astype(_f32), t, n_heads, kv_heads, hd)
            new_k, new_v = (a.reshape(n, keep, kv_heads, hd) for a in (new_k, new_v))
        h = _mix(attn, conv_out, gates, x, w_attn_b, w_out_b, g_post)
        if is_prompt:
            y, new_ffn = _ffn_prompt(h, g_ffn_pre, w_up_b, ffn_k, ffn_b, w_down_b, g_ffn_post, nseq)
        else:
            y, new_ffn = _ffn_sample(h, st_ffn, g_ffn_pre, w_up_b, ffn_k, ffn_b, w_down_b, g_ffn_post, t)
        outs.append((y.reshape(x3.shape), new_k, new_v, new_conv, new_ffn))
    return outs


def kernel(x_prompt, x_sample, cache_k, cache_v, state_conv, state_ffn_conv, norm_mix_pre, w_in, b_in, conv_dw_k, conv_dw_b, conv_ln_g, conv_ln_b, w_conv_proj, attn_sinks, rel_bias, w_attn_proj, w_out, norm_mix_post, norm_ffn_pre, w_up, ffn_dw_k, ffn_dw_b, w_down, norm_ffn_post):
    y_p, y_s = x_prompt, x_sample
    per_layer = []
    for l in range(w_in.shape[0]):
        p = (norm_mix_pre[l], w_in[l], b_in[l], conv_dw_k[l], conv_dw_b[l], conv_ln_g[l], conv_ln_b[l], w_conv_proj[l],
             attn_sinks[l], rel_bias, w_attn_proj[l], w_out[l], norm_mix_post[l],
             norm_ffn_pre[l], w_up[l], ffn_dw_k[l], ffn_dw_b[l], w_down[l], norm_ffn_post[l])
        (y_p, *rest_p), (y_s, *rest_s) = _layer(y_p, y_s, cache_k[l], cache_v[l], state_conv, l, state_ffn_conv[l], p)
        per_layer.append(rest_p + rest_s)
    stacked = [jnp.stack(leaves) for leaves in zip(*per_layer)]
    return (y_p, y_s, *stacked)
```

```python
import functools
import math

import jax
import jax.numpy as jnp
from jax import lax
from jax.experimental import pallas as pl
from jax.experimental.pallas import tpu as pltpu

EPS = 1e-6
WINDOW = 128
N_BUCKETS = 32
MAX_DISTANCE = 128
MASKED = -1e30
VMEM_LIMIT_BYTES = 60 * 1024 * 1024
HALO = 32
_bf16 = jnp.bfloat16
_f32 = jnp.float32


def _params(*sem):
    return pltpu.CompilerParams(dimension_semantics=sem, vmem_limit_bytes=VMEM_LIMIT_BYTES)


def _dot(a, b):
    return jnp.dot(a, b, preferred_element_type=_f32)


def _sigmoid(x):
    return 0.5 * (1.0 + jnp.tanh(0.5 * x))


def _rms(x, g):
    return x * lax.rsqrt(jnp.mean(x * x, axis=-1, keepdims=True) + EPS) * g


def _inproj_kernel(*refs, n_glu, n_qkv, side):
    if side:
        (x_ref, g_ref, wa_ref, wb_ref, ba_ref, bb_ref, side_ref,
         glu_ref, qkv_ref, gate_ref, side_out_ref, xn_ref) = refs
        side_out_ref[0] = side_ref[...].astype(_bf16)
    else:
        x_ref, g_ref, wa_ref, wb_ref, ba_ref, bb_ref, glu_ref, qkv_ref, gate_ref, xn_ref = refs
    j = pl.program_id(1)

    @pl.when(j == 0)
    def _():
        xn_ref[...] = _rms(x_ref[...], g_ref[...]).astype(_bf16)

    half = wb_ref.shape[2] // 2
    halves = [slice(0, half), slice(half, 2 * half)]

    def u(w_ref, b_ref, cs):
        return _dot(xn_ref[...], w_ref[0, :, cs]) + b_ref[:, cs]

    @pl.when(j < n_glu)
    def _():
        for cs in halves:
            glu_ref[:, cs] = u(wa_ref, ba_ref, cs) * _sigmoid(u(wb_ref, bb_ref, cs))

    @pl.when((j >= n_glu) & (j < n_glu + n_qkv))
    def _():
        for cs in halves:
            qkv_ref[:, cs] = u(wb_ref, bb_ref, cs)

    @pl.when(j >= n_glu + n_qkv)
    def _():
        for cs in halves:
            gate_ref[:, cs] = _sigmoid(u(wb_ref, bb_ref, cs))


def _cast_tile_kernel(w_ref, o_ref):
    o_ref[0] = w_ref[...].astype(_bf16)


def _column_tiles(w, tn, groups=1):
    d, n = w.shape
    nt = n // groups // tn
    return pl.pallas_call(
        _cast_tile_kernel,
        grid=(nt, groups),
        in_specs=[pl.BlockSpec((d, tn), lambda c, g: (0, g * nt + c))],
        out_specs=pl.BlockSpec((1, d, tn), lambda c, g: (c, 0, g)),
        out_shape=jax.ShapeDtypeStruct((nt, d, groups * tn), _bf16),
        compiler_params=_params("arbitrary", "arbitrary"),
        name="weight_tiles",
    )(w)


def _side_block_width(n_cols, tile, steps):
    bw = 128
    while bw < tile and n_cols // bw > steps:
        bw *= 2
    return bw if tile % bw == 0 and n_cols // bw <= steps else None


def _inproj(x, g, w, b, d_glu, d_qkv, d_gate, tm=1024, side=None, side_tile=None):
    m, d = x.shape
    tm = min(tm, m)
    tn = w.shape[2]
    n_glu, n_qkv, n_gate = d_glu // tn, d_qkv // tn, d_gate // tn
    n_col = n_glu + n_qkv + n_gate
    assert w.shape[0] == n_col + n_glu
    first_w = lambda i, j: (jnp.minimum(j, n_glu - 1), 0, 0)
    rest_w = lambda i, j: (j + n_glu, 0, 0)
    first = lambda i, j: (0, jnp.minimum(j, n_glu - 1))
    rest = lambda i, j: (0, j + n_glu)
    in_specs = [
        pl.BlockSpec((tm, d), lambda i, j: (i, 0)),
        pl.BlockSpec((1, d), lambda i, j: (0, 0)),
        pl.BlockSpec((1, d, tn), first_w),
        pl.BlockSpec((1, d, tn), rest_w),
        pl.BlockSpec((1, tn), first),
        pl.BlockSpec((1, tn), rest),
    ]
    out_specs = [
        pl.BlockSpec((tm, tn), lambda i, j: (i, jnp.minimum(j, n_glu - 1))),
        pl.BlockSpec((tm, tn), lambda i, j: (i, jnp.clip(j - n_glu, 0, n_qkv - 1))),
        pl.BlockSpec((tm, tn), lambda i, j: (i, jnp.maximum(j - n_glu - n_qkv, 0))),
    ]
    out_shape = [
        jax.ShapeDtypeStruct((m, d_glu), _f32),
        jax.ShapeDtypeStruct((m, d_qkv), _f32),
        jax.ShapeDtypeStruct((m, d_gate), _f32),
    ]
    operands = [x, g, w, w, b, b]
    bw = None if side is None else _side_block_width(side.shape[1], side_tile, (m // tm) * n_col)
    if bw is not None:
        d2, n2 = side.shape
        nblk, per_tile = n2 // bw, side_tile // bw
        blk = lambda i, j: jnp.minimum(i * n_col + j, nblk - 1)
        in_specs.append(pl.BlockSpec((d2, bw), lambda i, j: (0, blk(i, j))))
        out_specs.append(pl.BlockSpec(
            (1, d2, bw), lambda i, j: ((blk(i, j) % (nblk // 2)) // per_tile, 0,
                                       (blk(i, j) // (nblk // 2)) * per_tile + blk(i, j) % per_tile)))
        out_shape.append(jax.ShapeDtypeStruct((n2 // 2 // side_tile, d2, 2 * side_tile), _bf16))
        operands.append(side)
    outs = pl.pallas_call(
        functools.partial(_inproj_kernel, n_glu=n_glu, n_qkv=n_qkv, side=bw is not None),
        grid=(m // tm, n_col),
        in_specs=in_specs,
        out_specs=out_specs,
        out_shape=out_shape,
        scratch_shapes=[pltpu.VMEM((tm, d), _bf16)],
        compiler_params=_params("arbitrary", "arbitrary"),
        name="inproj",
    )(*operands)
    return outs if bw is not None else (*outs, None)


def _ln_silu_proj(conv_ref, act_ref, lg_ref, lb_ref, w_ref, o_ref, rows):
    rt = 16

    def body(r, carry):
        r0 = pl.multiple_of(r * rt, rt)
        if len(conv_ref.shape) == 2:
            c = conv_ref[pl.ds(r0, rt), :]
        else:
            per = rt // conv_ref.shape[1]
            c = conv_ref[pl.ds(pl.multiple_of(r * per, per), per)].reshape(rt, conv_ref.shape[2])
        mu = jnp.mean(c, axis=-1, keepdims=True)
        cc = c - mu
        var = jnp.mean(cc * cc, axis=-1, keepdims=True)
        y = cc * lax.rsqrt(var + EPS) * lg_ref[...] + lb_ref[...]
        act_ref[pl.ds(r0, rt), :] = (y * _sigmoid(y)).astype(_bf16)
        return carry

    lax.fori_loop(0, rows // rt, body, 0, unroll=4)
    o_ref[...] = _dot(act_ref[...], w_ref[...])


def _conv_prompt_kernel(glu_ref, k_ref, cb_ref, lg_ref, lb_ref, w_ref, o_ref, ext_ref, conv_ref, act_ref, *, tm, width):
    d = glu_ref.shape[1]
    i = pl.program_id(1)

    @pl.when(i == 0)
    def _():
        ext_ref[0:HALO, :] = jnp.zeros((HALO, d), _f32)

    @pl.when(i > 0)
    def _():
        ext_ref[0:HALO, :] = ext_ref[tm:tm + HALO, :]

    ext_ref[HALO:HALO + tm, :] = glu_ref[...]

    rt, ct, sub = 64, 128, 8
    first = HALO - (width - 1)

    def body(r, carry):
        r0 = pl.multiple_of(r * rt, rt)
        for c in range(d // ct):
            cs = slice(c * ct, (c + 1) * ct)
            blk = ext_ref[pl.ds(r0, rt + HALO), cs]
            acc = jnp.broadcast_to(cb_ref[:, cs], (rt, ct))
            for phase in range(sub):
                taps = [w for w in range(width) if (first + w) % sub == phase]
                rows = rt if phase == 0 else rt + sub
                part = None
                for w in taps:
                    lo = first + w - phase
                    term = k_ref[w:w + 1, cs] * blk[lo:lo + rows, :]
                    part = term if part is None else part + term
                acc = acc + part[phase:phase + rt, :]
            conv_ref[pl.ds(r0, rt), cs] = acc
        return carry

    lax.fori_loop(0, tm // rt, body, 0)
    _ln_silu_proj(conv_ref, act_ref, lg_ref, lb_ref, w_ref, o_ref, tm)


def _conv_prompt(glu, conv_k, conv_b, ln_g, ln_b, w, nseq, tm=256):
    m, d = glu.shape
    nt = m // nseq // tm
    width = conv_k.shape[0]
    const = lambda b, i: (0, 0)
    return pl.pallas_call(
        functools.partial(_conv_prompt_kernel, tm=tm, width=width),
        grid=(nseq, nt),
        in_specs=[
            pl.BlockSpec((tm, d), lambda b, i: (b * nt + i, 0)),
            pl.BlockSpec((width, d), const),
            pl.BlockSpec((1, d), const),
            pl.BlockSpec((1, d), const),
            pl.BlockSpec((1, d), const),
            pl.BlockSpec((d, d), const),
        ],
        out_specs=pl.BlockSpec((tm, d), lambda b, i: (b * nt + i, 0)),
        out_shape=jax.ShapeDtypeStruct((m, d), _f32),
        scratch_shapes=[pltpu.VMEM((tm + HALO, d), _f32), pltpu.VMEM((tm, d), _f32), pltpu.VMEM((tm, d), _bf16)],
        compiler_params=_params("arbitrary", "arbitrary"),
        name="conv_prompt",
    )(glu, conv_k, conv_b, ln_g, ln_b, w)


def _conv_sample_kernel(st_ref, glu_ref, k_ref, cb_ref, lg_ref, lb_ref, w_ref, o_ref, ns_ref,
                        ext_ref, conv_ref, act_ref, *, nb, t, width):
    d = glu_ref.shape[2]
    hist = width - 1
    ct, sub = 128, 8

    def history(n, carry):
        ext_ref[0:hist, :] = st_ref[0, n]
        ext_ref[hist:hist + t, :] = glu_ref[n]
        ns_ref[0, n] = ext_ref[t:t + hist, :]
        return carry

    lax.fori_loop(0, nb, history, 0, unroll=2)

    def group(gi, carry):
        seqs = pl.ds(pl.multiple_of(gi * sub, sub), sub)
        for c in range(d // ct):
            cs = slice(c * ct, (c + 1) * ct)
            x = [st_ref[0, seqs, j, cs] for j in range(hist)] + [glu_ref[seqs, j, cs] for j in range(t)]
            for tt in range(t):
                acc = jnp.broadcast_to(cb_ref[:, cs], (sub, ct))
                for w in range(width):
                    acc = acc + k_ref[w:w + 1, cs] * x[tt + w]
                conv_ref[seqs, tt, cs] = acc
        return carry

    lax.fori_loop(0, nb // sub, group, 0)
    _ln_silu_proj(conv_ref, act_ref, lg_ref, lb_ref, w_ref, o_ref, nb * t)


def _conv_sample(state, layer, glu, conv_k, conv_b, ln_g, ln_b, w, nb=32):
    _, n, hist, d = state.shape
    width = conv_k.shape[0]
    t = glu.shape[0] // n
    assert hist == width - 1 and t <= hist
    const = lambda i: (0, 0)
    return pl.pallas_call(
        functools.partial(_conv_sample_kernel, nb=nb, t=t, width=width),
        grid=(n // nb,),
        in_specs=[
            pl.BlockSpec((1, nb, hist, d), lambda i: (layer, i, 0, 0)),
            pl.BlockSpec((nb, t, d), lambda i: (i, 0, 0)),
            pl.BlockSpec((width, d), const),
            pl.BlockSpec((1, d), const),
            pl.BlockSpec((1, d), const),
            pl.BlockSpec((1, d), const),
            pl.BlockSpec((d, d), const),
        ],
        out_specs=[
            pl.BlockSpec((nb * t, d), lambda i: (i, 0)),
            pl.BlockSpec((1, nb, hist, d), lambda i: (0, i, 0, 0)),
        ],
        out_shape=[
            jax.ShapeDtypeStruct((n * t, d), _f32),
            jax.ShapeDtypeStruct((1, n, hist, d), _f32),
        ],
        scratch_shapes=[pltpu.VMEM((hist + t + (-(hist + t)) % 8, d), _f32), pltpu.VMEM((nb, t, d), _f32),
                        pltpu.VMEM((nb * t, d), _bf16)],
        compiler_params=_params("arbitrary"),
        name="conv_sample",
    )(state, glu.reshape(n, t, d), conv_k, conv_b, ln_g, ln_b, w)


def _attn_prompt_kernel(q_ref, kc_ref, kp_ref, vc_ref, vp_ref, bias_ref, sink_ref, *rest, kv_heads, hd, side):
    o_ref = rest[side]
    for side_ref, side_out_ref in zip(rest[:side], rest[side + 1:]):
        side_out_ref[...] = side_ref[...].astype(_bf16)
    blk = q_ref.shape[0]
    grp = 2 * hd
    pairs = q_ref.shape[1] // (kv_heads * grp)
    cols = pairs * blk
    scale = hd ** -0.5
    nt = (((1,), (1,)), ((), ()))
    low = lax.broadcasted_iota(jnp.int32, (2 * blk, grp), 1) < hd
    qi = lax.broadcasted_iota(jnp.int32, (blk, cols), 1) & (blk - 1)
    cur_visible = lax.broadcasted_iota(jnp.int32, (blk, cols), 0) <= qi

    def halves(p_ref, c_ref, h):
        g0 = (h // 2) * grp
        x = jnp.concatenate([p_ref[:, g0:g0 + grp], c_ref[:, g0:g0 + grp]], axis=0)
        swapped = pltpu.roll(x, hd, axis=1)
        in_low, in_high = (x, swapped) if h % 2 == 0 else (swapped, x)
        return jnp.where(low, in_low, 0.0), jnp.where(low, 0.0, in_high)

    for h in range(kv_heads):
        ks = [x.astype(_bf16) for x in halves(kp_ref, kc_ref, h)]
        vts = [x.T.astype(_bf16) for x in halves(vp_ref, vc_ref, h)]
        lanes = [slice((h * pairs + p) * grp, (h * pairs + p + 1) * grp) for p in range(pairs)]
        qs = (jnp.concatenate([q_ref[:, ls] for ls in lanes], axis=0) * scale).astype(_bf16)
        out_t = None
        for parity in range(2):
            s2 = lax.dot_general(ks[parity], qs, nt, preferred_element_type=_f32)
            s = jnp.where(cur_visible, s2[blk:, :], s2[:blk, :]) + bias_ref[0, h, parity]
            sink = sink_ref[h, parity]
            m = jnp.maximum(jnp.max(s, axis=0, keepdims=True), sink)
            p = jnp.exp(s - m)
            den = jnp.sum(p, axis=0, keepdims=True) + jnp.exp(sink - m)
            pn = p * (1.0 / den)
            p2 = jnp.concatenate([jnp.where(cur_visible, 0.0, pn), jnp.where(cur_visible, pn, 0.0)], axis=0)
            o_t = _dot(vts[parity], p2.astype(_bf16))
            out_t = o_t if out_t is None else out_t + o_t
        out = out_t.T
        for p, ls in enumerate(lanes):
            o_ref[:, ls] = out[p * blk:(p + 1) * blk, :]


def _attn_prompt(qkv, bias, sinks, nseq, n_heads, kv_heads, hd, sides=()):
    m = qkv.shape[0]
    dq, dkv = n_heads * hd, kv_heads * hd
    nb = m // nseq // WINDOW
    pairs = n_heads // kv_heads // 2
    rows = pairs * WINDOW
    assert WINDOW & (WINDOW - 1) == 0 and 2 * hd == 128 and kv_heads % 2 == 0
    qi = jnp.arange(WINDOW)[:, None]
    cur_visible = jnp.arange(WINDOW)[None, :] <= qi
    folded = jnp.stack([jnp.where(cur_visible, bias[:, :, WINDOW:], MASKED),
                        jnp.where(cur_visible, bias[:, :, WINDOW:], bias[:, :, :WINDOW])])
    folded = folded.reshape(2, kv_heads, pairs, 2, WINDOW, WINDOW).transpose(0, 1, 3, 5, 2, 4)
    folded = folded.reshape(2, kv_heads, 2, WINDOW, rows)
    sink_rows = jnp.repeat(sinks.reshape(kv_heads, pairs, 2).transpose(0, 2, 1), WINDOW, axis=-1)
    sink_rows = sink_rows.reshape(kv_heads, 2, 1, rows)
    kcol, vcol = dq // dkv, dq // dkv + 1
    cur = lambda b, i: b * nb + i
    prev = lambda b, i: b * nb + jnp.maximum(i - 1, 0)
    in_specs = [
        pl.BlockSpec((WINDOW, dq), lambda b, i: (cur(b, i), 0)),
        pl.BlockSpec((WINDOW, dkv), lambda b, i: (cur(b, i), kcol)),
        pl.BlockSpec((WINDOW, dkv), lambda b, i: (prev(b, i), kcol)),
        pl.BlockSpec((WINDOW, dkv), lambda b, i: (cur(b, i), vcol)),
        pl.BlockSpec((WINDOW, dkv), lambda b, i: (prev(b, i), vcol)),
        pl.BlockSpec((1, kv_heads, 2, WINDOW, rows), lambda b, i: (jnp.minimum(i, 1), 0, 0, 0, 0)),
        pl.BlockSpec((kv_heads, 2, 1, rows), lambda b, i: (0, 0, 0, 0)),
    ]
    out_specs = [pl.BlockSpec((WINDOW, dq), lambda b, i: (cur(b, i), 0))]
    out_shape = [jax.ShapeDtypeStruct((m, dq), _f32)]
    operands = [qkv, qkv, qkv, qkv, qkv, folded, sink_rows]
    steps = nseq * nb
    ride = [w for w in sides if w.shape[0] % (16 * steps) == 0]
    for w in ride:
        block = (w.shape[0] // steps, w.shape[1])
        in_specs.append(pl.BlockSpec(block, lambda b, i: (cur(b, i), 0)))
        out_specs.append(pl.BlockSpec(block, lambda b, i: (cur(b, i), 0)))
        out_shape.append(jax.ShapeDtypeStruct(w.shape, _bf16))
        operands.append(w)
    outs = pl.pallas_call(
        functools.partial(_attn_prompt_kernel, kv_heads=kv_heads, hd=hd, side=len(ride)),
        grid=(nseq, nb),
        in_specs=in_specs,
        out_specs=out_specs,
        out_shape=out_shape,
        compiler_params=_params("arbitrary", "arbitrary"),
        name="attn_prompt",
    )(*operands)
    cast = iter(outs[1:])
    return outs[0], [next(cast) if any(w is r for r in ride) else w.astype(_bf16) for w in sides]


def _attn_sample_kernel(qkv_ref, ck_ref, cv_ref, bias_ref, sink_ref, o_ref, ko_ref, vo_ref, *,
                        nb, t, n_heads, kv_heads, hd):
    dq, dkv = n_heads * hd, kv_heads * hd
    group = n_heads // kv_heads
    keep = ck_ref.shape[1]
    keys = bias_ref.shape[0]
    grp = 2 * hd
    scale = hd ** -0.5
    nt = (((1,), (1,)), ((), ()))
    low = lax.broadcasted_iota(jnp.int32, (t, grp), 1) < hd
    blank = jnp.zeros((t, grp), _f32)
    pad = jnp.zeros((keys - keep - t, dkv), _f32)

    def body(n, carry):
        r0 = pl.multiple_of(n * t, t)
        q = qkv_ref[pl.ds(r0, t), 0:dq]
        kn = qkv_ref[pl.ds(r0, t), dq:dq + dkv]
        vn = qkv_ref[pl.ds(r0, t), dq + dkv:dq + 2 * dkv]
        ck, cv = ck_ref[n], cv_ref[n]
        ko_ref[n, 0:keep - t, :] = ck[t:, :]
        ko_ref[n, keep - t:keep, :] = kn
        vo_ref[n, 0:keep - t, :] = cv[t:, :]
        vo_ref[n, keep - t:keep, :] = vn
        kp = jnp.concatenate([ck, kn, pad], axis=0).astype(_bf16)
        vp = jnp.concatenate([cv, vn, pad], axis=0).astype(_bf16)

        rows = []
        for h in range(kv_heads):
            for pair in range(group // 2):
                qv = q[:, (h * (group // 2) + pair) * grp:(h * (group // 2) + pair + 1) * grp]
                swapped = pltpu.roll(qv, hd, axis=1)
                for parity in range(2):
                    x = qv if parity == h % 2 else swapped
                    x = jnp.where(low, x, 0.0) if h % 2 == 0 else jnp.where(low, 0.0, x)
                    pieces = [blank] * (dkv // grp)
                    pieces[h // 2] = x
                    rows.append(jnp.concatenate(pieces, axis=1))
        wt = (jnp.concatenate(rows, axis=0) * scale).astype(_bf16)

        st = lax.dot_general(kp, wt, nt, preferred_element_type=_f32) + bias_ref[...]
        sink = sink_ref[...]
        m = jnp.maximum(jnp.max(st, axis=0, keepdims=True), sink)
        p = jnp.exp(st - m)
        den = jnp.sum(p, axis=0, keepdims=True) + jnp.exp(sink - m)
        pn = p * (1.0 / den)
        of = _dot(pn.T.astype(_bf16), vp)

        outs = []
        for h in range(kv_heads):
            ls = slice((h // 2) * grp, (h // 2 + 1) * grp)
            for pair in range(group // 2):
                r = (h * group + 2 * pair) * t
                even, odd = of[r:r + t, ls], of[r + t:r + 2 * t, ls]
                if h % 2 == 0:
                    outs.append(jnp.where(low, even, pltpu.roll(odd, hd, axis=1)))
                else:
                    outs.append(jnp.where(low, pltpu.roll(even, hd, axis=1), odd))
        o_ref[pl.ds(r0, t), :] = jnp.concatenate(outs, axis=1)
        return carry

    lax.fori_loop(0, nb, body, 0, unroll=2)


def _attn_sample(qkv, cache_k, cache_v, bias, sinks, t, n_heads, kv_heads, hd, nb=8):
    n, keep, dkv = cache_k.shape
    dq = n_heads * hd
    keys = bias.shape[2]
    assert keep == WINDOW and keep + t <= keys and 2 * hd == 128 and kv_heads % 2 == 0
    bias_t = bias[:, :t].transpose(2, 0, 1).reshape(keys, n_heads * t)
    sink_row = jnp.repeat(sinks, t).reshape(1, n_heads * t)
    const = lambda i: (0, 0)
    return pl.pallas_call(
        functools.partial(_attn_sample_kernel, nb=nb, t=t, n_heads=n_heads, kv_heads=kv_heads, hd=hd),
        grid=(n // nb,),
        in_specs=[
            pl.BlockSpec((nb * t, qkv.shape[1]), lambda i: (i, 0)),
            pl.BlockSpec((nb, keep, dkv), lambda i: (i, 0, 0)),
            pl.BlockSpec((nb, keep, dkv), lambda i: (i, 0, 0)),
            pl.BlockSpec((keys, n_heads * t), const),
            pl.BlockSpec((1, n_heads * t), const),
        ],
        out_specs=[
            pl.BlockSpec((nb * t, dq), lambda i: (i, 0)),
            pl.BlockSpec((nb, keep, dkv), lambda i: (i, 0, 0)),
            pl.BlockSpec((nb, keep, dkv), lambda i: (i, 0, 0)),
        ],
        out_shape=[
            jax.ShapeDtypeStruct((n * t, dq), _f32),
            jax.ShapeDtypeStruct((n, keep, dkv), _f32),
            jax.ShapeDtypeStruct((n, keep, dkv), _f32),
        ],
        compiler_params=_params("arbitrary"),
        name="attn_sample",
    )(qkv, cache_k, cache_v, bias_t, sink_row)


def _mix_kernel(attn_ref, conv_ref, gc_ref, ga_ref, x_ref, wa_ref, wo_ref, g_ref, h_ref):
    ao = _dot(attn_ref[...].astype(_bf16), wa_ref[...])
    mixed = gc_ref[...] * conv_ref[...] + ga_ref[...] * ao
    o = _dot(mixed.astype(_bf16), wo_ref[...])
    h_ref[...] = x_ref[...] + _rms(o, g_ref[...])


def _mix(attn, conv_out, gates, x, w_attn, w_out, g_post, tm=256):
    m, d = x.shape
    row = lambda i: (i, 0)
    const = lambda i: (0, 0)
    once = pl.Buffered(1)
    return pl.pallas_call(
        _mix_kernel,
        grid=(m // tm,),
        in_specs=[
            pl.BlockSpec((tm, d), row),
            pl.BlockSpec((tm, d), row),
            pl.BlockSpec((tm, d), lambda i: (i, 0)),
            pl.BlockSpec((tm, d), lambda i: (i, 1)),
            pl.BlockSpec((tm, d), row),
            pl.BlockSpec((d, d), const, pipeline_mode=once),
            pl.BlockSpec((d, d), const, pipeline_mode=once),
            pl.BlockSpec((1, d), const),
        ],
        out_specs=pl.BlockSpec((tm, d), row),
        out_shape=jax.ShapeDtypeStruct((m, d), _f32),
        compiler_params=_params("arbitrary"),
        name="mix",
    )(attn, conv_out, gates, gates, x, w_attn, w_out, g_post)


def _gelu_tanh(x):
    return 0.5 * x * (1.0 + jnp.tanh(math.sqrt(2.0 / math.pi) * (x + 0.044715 * (x * x * x))))


MXU_COLUMNS = 256
FFN_CHUNK = 1024


def _chunks(n, width=MXU_COLUMNS):
    return [slice(a, a + width) for a in range(0, n, width)]


def _ffn_chunks(tc, act_ref, wd_ref, acc_ref, gate_val):
    for cs in _chunks(tc):
        gate, val = gate_val(cs)
        act_ref[:, cs] = (_gelu_tanh(gate) * val).astype(_bf16)
    acc_ref[...] += _dot(act_ref[...], wd_ref[...])


def _ffn_finish(c, nc, acc_ref, h_ref, gp_ref, y_ref):
    @pl.when(c == nc - 1)
    def _():
        y_ref[...] = h_ref[...] + _rms(acc_ref[...], gp_ref[...])


def _ffn_prompt_kernel(h_ref, g_ref, w_ref, kg_ref, kv_ref, bg_ref, bv_ref, wd_ref, gp_ref,
                       y_ref, tg_ref, tv_ref, hn_ref, acc_ref, act_ref, ug_ref, uv_ref, cg_ref, cv_ref, *, tm, nc):
    i, c = pl.program_id(1), pl.program_id(2)
    tc = w_ref.shape[2] // 2

    @pl.when(c == 0)
    def _():
        hn_ref[...] = _rms(h_ref[...], g_ref[...]).astype(_bf16)
        acc_ref[...] = jnp.zeros_like(acc_ref)

    @pl.when(i == 0)
    def _():
        ug_ref[0:8, :] = jnp.zeros((8, tc), _f32)
        uv_ref[0:8, :] = jnp.zeros((8, tc), _f32)

    @pl.when(i > 0)
    def _():
        ug_ref[0:8, :] = cg_ref[c]
        uv_ref[0:8, :] = cv_ref[c]

    def conv(u_ref, col0, k_ref, b_ref, cs):
        u_ref[8:8 + tm, cs] = _dot(hn_ref[...], w_ref[0, :, col0 + cs.start:col0 + cs.stop])
        return (k_ref[2:3, cs] * u_ref[8:8 + tm, cs] + k_ref[1:2, cs] * u_ref[7:7 + tm, cs]
                + k_ref[0:1, cs] * u_ref[6:6 + tm, cs] + b_ref[:, cs])

    _ffn_chunks(tc, act_ref, wd_ref, acc_ref,
                lambda cs: (conv(ug_ref, 0, kg_ref, bg_ref, cs), conv(uv_ref, tc, kv_ref, bv_ref, cs)))
    cg_ref[c] = ug_ref[tm:tm + 8, :]
    cv_ref[c] = uv_ref[tm:tm + 8, :]
    tg_ref[0, 0] = ug_ref[tm + 6:tm + 8, :]
    tv_ref[0, 0] = uv_ref[tm + 6:tm + 8, :]
    _ffn_finish(c, nc, acc_ref, h_ref, gp_ref, y_ref)


def _ffn_prompt(h, g_pre, w_up, ffn_k, ffn_b, w_down, g_post, nseq, tm=512):
    m, d = h.shape
    dff = w_down.shape[0]
    nc, tc = w_up.shape[0], w_up.shape[2] // 2
    nt = m // nseq // tm
    width = ffn_k.shape[0]
    assert width == 3 and nc * tc == dff
    row = lambda b, i, c: (b * nt + i, 0)
    const = lambda b, i, c: (0, 0)
    lo = lambda b, i, c: (0, c)
    hi = lambda b, i, c: (0, c + nc)
    y, tg, tv = pl.pallas_call(
        functools.partial(_ffn_prompt_kernel, tm=tm, nc=nc),
        grid=(nseq, nt, nc),
        in_specs=[
            pl.BlockSpec((tm, d), row),
            pl.BlockSpec((1, d), const),
            pl.BlockSpec((1, d, 2 * tc), lambda b, i, c: (c, 0, 0)),
            pl.BlockSpec((width, tc), lo),
            pl.BlockSpec((width, tc), hi),
            pl.BlockSpec((1, tc), lo),
            pl.BlockSpec((1, tc), hi),
            pl.BlockSpec((tc, d), lambda b, i, c: (c, 0)),
            pl.BlockSpec((1, d), const),
        ],
        out_specs=[
            pl.BlockSpec((tm, d), row, pipeline_mode=pl.Buffered(1)),
            pl.BlockSpec((1, 1, width - 1, tc), lambda b, i, c: (b, i, 0, c)),
            pl.BlockSpec((1, 1, width - 1, tc), lambda b, i, c: (b, i, 0, c)),
        ],
        out_shape=[
            jax.ShapeDtypeStruct((m, d), _f32),
            jax.ShapeDtypeStruct((nseq, nt, width - 1, dff), _f32),
            jax.ShapeDtypeStruct((nseq, nt, width - 1, dff), _f32),
        ],
        scratch_shapes=[
            pltpu.VMEM((tm, d), _bf16),
            pltpu.VMEM((tm, d), _f32),
            pltpu.VMEM((tm, tc), _bf16),
            pltpu.VMEM((tm + 8, tc), _f32),
            pltpu.VMEM((tm + 8, tc), _f32),
            pltpu.VMEM((nc, 8, tc), _f32),
            pltpu.VMEM((nc, 8, tc), _f32),
        ],
        compiler_params=_params("arbitrary", "arbitrary", "arbitrary"),
        name="ffn_prompt",
    )(h, g_pre, w_up, ffn_k, ffn_k, ffn_b, ffn_b, w_down, g_post)
    return y, jnp.concatenate([tg[:, -1], tv[:, -1]], axis=-1)


def _ffn_sample_kernel(h_ref, g_ref, w_ref, kg_ref, kv_ref, bg_ref, bv_ref, sg_ref, sv_ref, wd_ref, gp_ref,
                       y_ref, tg_ref, tv_ref, hn_ref, acc_ref, act_ref, ug_ref, uv_ref, *, nb, t, nc):
    c = pl.program_id(1)
    tc = w_ref.shape[2] // 2

    @pl.when(c == 0)
    def _():
        hn_ref[...] = _rms(h_ref[...], g_ref[...]).astype(_bf16)
        acc_ref[...] = jnp.zeros_like(acc_ref)

    ug_ref[:, 6:8, :] = sg_ref[...]
    uv_ref[:, 6:8, :] = sv_ref[...]

    def conv(u_ref, col0, k_ref, b_ref, cs):
        width = cs.stop - cs.start
        u_ref[:, 8:8 + t, cs] = _dot(hn_ref[...], w_ref[0, :, col0 + cs.start:col0 + cs.stop]).reshape(nb, t, width)
        cv = (k_ref[2:3, cs] * u_ref[:, 8:8 + t, cs] + k_ref[1:2, cs] * u_ref[:, 7:7 + t, cs]
              + k_ref[0:1, cs] * u_ref[:, 6:6 + t, cs] + b_ref[:, cs])
        return cv.reshape(nb * t, width)

    _ffn_chunks(tc, act_ref, wd_ref, acc_ref,
                lambda cs: (conv(ug_ref, 0, kg_ref, bg_ref, cs), conv(uv_ref, tc, kv_ref, bv_ref, cs)))
    tg_ref[...] = ug_ref[:, 6 + t:8 + t, :]
    tv_ref[...] = uv_ref[:, 6 + t:8 + t, :]
    _ffn_finish(c, nc, acc_ref, h_ref, gp_ref, y_ref)


def _ffn_sample(h, state, g_pre, w_up, ffn_k, ffn_b, w_down, g_post, t, nb=64):
    m, d = h.shape
    n = m // t
    dff = w_down.shape[0]
    nc, tc = w_up.shape[0], w_up.shape[2] // 2
    width = ffn_k.shape[0]
    assert width == 3 and t >= width - 1 and nc * tc == dff
    tm = nb * t
    row = lambda i, c: (i, 0)
    const = lambda i, c: (0, 0)
    lo = lambda i, c: (0, c)
    hi = lambda i, c: (0, c + nc)
    y, tg, tv = pl.pallas_call(
        functools.partial(_ffn_sample_kernel, nb=nb, t=t, nc=nc),
        grid=(n // nb, nc),
        in_specs=[
            pl.BlockSpec((tm, d), row, pipeline_mode=pl.Buffered(1)),
            pl.BlockSpec((1, d), const),
            pl.BlockSpec((1, d, 2 * tc), lambda i, c: (c, 0, 0)),
            pl.BlockSpec((width, tc), lo),
            pl.BlockSpec((width, tc), hi),
            pl.BlockSpec((1, tc), lo),
            pl.BlockSpec((1, tc), hi),
            pl.BlockSpec((nb, width - 1, tc), lambda i, c: (i, 0, c)),
            pl.BlockSpec((nb, width - 1, tc), lambda i, c: (i, 0, c + nc)),
            pl.BlockSpec((tc, d), lambda i, c: (c, 0)),
            pl.BlockSpec((1, d), const),
        ],
        out_specs=[
            pl.BlockSpec((tm, d), row, pipeline_mode=pl.Buffered(1)),
            pl.BlockSpec((nb, width - 1, tc), lambda i, c: (i, 0, c)),
            pl.BlockSpec((nb, width - 1, tc), lambda i, c: (i, 0, c)),
        ],
        out_shape=[
            jax.ShapeDtypeStruct((m, d), _f32),
            jax.ShapeDtypeStruct((n, width - 1, dff), _f32),
            jax.ShapeDtypeStruct((n, width - 1, dff), _f32),
        ],
        scratch_shapes=[
            pltpu.VMEM((tm, d), _bf16),
            pltpu.VMEM((tm, d), _f32),
            pltpu.VMEM((tm, tc), _bf16),
            pltpu.VMEM((nb, 8 + t, tc), _f32),
            pltpu.VMEM((nb, 8 + t, tc), _f32),
        ],
        compiler_params=_params("arbitrary", "arbitrary"),
        name="ffn_sample",
    )(h, g_pre, w_up, ffn_k, ffn_k, ffn_b, ffn_b, state, state, w_down, g_post)
    return y, jnp.concatenate([tg, tv], axis=-1)


def _rel_bucket(dist):
    max_exact = N_BUCKETS // 2
    dd = jnp.maximum(dist, 1).astype(_f32)
    large = max_exact + (jnp.log(dd / max_exact) / math.log(MAX_DISTANCE / max_exact)
                         * (N_BUCKETS - max_exact)).astype(jnp.int32)
    large = jnp.minimum(large, N_BUCKETS - 1)
    return jnp.where(dist < max_exact, dist, large)


def _bias_table(rel_bias):
    n = WINDOW
    by_dist = rel_bias[_rel_bucket(jnp.arange(n))].astype(_f32).T
    g = jnp.concatenate([jnp.full_like(by_dist, MASKED), by_dist[:, ::-1],
                         jnp.full_like(by_dist, MASKED)], axis=1)
    heads, length = g.shape
    skew = jnp.broadcast_to(g[:, None, :], (heads, n, length)).reshape(heads, n * length)
    skew = skew[:, :n * (length - 1)].reshape(heads, n, length - 1)
    return skew[:, :, n - 1:3 * n - 1]


def _row(v):
    return v.reshape(1, -1)


def _layer(xp, xs, cache_k, cache_v, state_conv, layer, st_ffn, p):
    (g_pre, w_in, b_in, conv_k, conv_b, ln_g, ln_b, w_conv, sinks, rel_bias, w_attn, w_out, g_post,
     g_ffn_pre, w_up, ffn_k, ffn_b, w_down, g_ffn_post) = p
    nseq, seq, d = xp.shape
    t = xs.shape[1]
    n_heads = sinks.shape[0]
    keep, kv_heads, hd = cache_k.shape[1:]
    dq, dkv = n_heads * hd, kv_heads * hd
    d_conv = conv_k.shape[1]
    assert d_conv == d and dq == d and keep == WINDOW and seq % WINDOW == 0

    w_in_b = _column_tiles(w_in, 512)
    ffn_tile = min(FFN_CHUNK, w_down.shape[0])
    w_up_b = None
    g_pre, b_in, conv_b, ln_g, ln_b, g_post, g_ffn_pre, ffn_b, g_ffn_post = map(
        _row, (g_pre, b_in, conv_b, ln_g, ln_b, g_post, g_ffn_pre, ffn_b, g_ffn_post))

    bias = _bias_table(rel_bias)

    outs = []
    for x3, is_prompt in ((xp, True), (xs, False)):
        n = x3.shape[0]
        x = x3.reshape(-1, d)
        glu, qkv, gates, tiles = _inproj(x, g_pre, w_in_b, b_in, d_conv, dq + 2 * dkv, 2 * d,
                                         side=w_up if w_up_b is None else None, side_tile=ffn_tile)
        if w_up_b is None:
            w_up_b = tiles if tiles is not None else _column_tiles(w_up, ffn_tile, groups=2)
        if is_prompt:
            attn, (w_down_b, w_conv_b, w_attn_b, w_out_b) = _attn_prompt(
                qkv, bias, sinks.astype(_f32), nseq, n_heads, kv_heads, hd, sides=(w_down, w_conv, w_attn, w_out))
            conv_out = _conv_prompt(glu, conv_k, conv_b, ln_g, ln_b, w_conv_b, nseq)
            new_conv = glu.reshape(n, seq, d)[:, seq - (conv_k.shape[0] - 1):]
            last = qkv.reshape(n, seq, -1)[:, seq - keep:]
            new_k = last[:, :, dq:dq + dkv].reshape(n, keep, kv_heads, hd)
            new_v = last[:, :, dq + dkv:].reshape(n, keep, kv_heads, hd)
        else:
            conv_out, new_conv = _conv_sample(state_conv, layer, glu, conv_k, conv_b, ln_g, ln_b, w_conv_b)
            new_conv = new_conv[0]
            attn, new_k, new_v = _attn_sample(qkv, cache_k.reshape(n, keep, dkv), cache_v.reshape(n, keep, dkv), bias,
                                              sinks.astype(_f32), t, n_heads, kv_heads, hd)
            new_k, new_v = (a.reshape(n, keep, kv_heads, hd) for a in (new_k, new_v))
        h = _mix(attn, conv_out, gates, x, w_attn_b, w_out_b, g_post)
        if is_prompt:
            y, new_ffn = _ffn_prompt(h, g_ffn_pre, w_up_b, ffn_k, ffn_b, w_down_b, g_ffn_post, nseq)
        else:
            y, new_ffn = _ffn_sample(h, st_ffn, g_ffn_pre, w_up_b, ffn_k, ffn_b, w_down_b, g_ffn_post, t)
        outs.append((y.reshape(x3.shape), new_k, new_v, new_conv, new_ffn))
    return outs


def kernel(x_prompt, x_sample, cache_k, cache_v, state_conv, state_ffn_conv, norm_mix_pre, w_in, b_in, conv_dw_k, conv_dw_b, conv_ln_g, conv_ln_b, w_conv_proj, attn_sinks, rel_bias, w_attn_proj, w_out, norm_mix_post, norm_ffn_pre, w_up, ffn_dw_k, ffn_dw_b, w_down, norm_ffn_post):
    y_p, y_s = x_prompt, x_sample
    per_layer = []
    for l in range(w_in.shape[0]):
        p = (norm_mix_pre[l], w_in[l], b_in[l], conv_dw_k[l], conv_dw_b[l], conv_ln_g[l], conv_ln_b[l], w_conv_proj[l],
             attn_sinks[l], rel_bias, w_attn_proj[l], w_out[l], norm_mix_post[l],
             norm_ffn_pre[l], w_up[l], ffn_dw_k[l], ffn_dw_b[l], w_down[l], norm_ffn_post[l])
        (y_p, *rest_p), (y_s, *rest_s) = _layer(y_p, y_s, cache_k[l], cache_v[l], state_conv, l, state_ffn_conv[l], p)
        per_layer.append(rest_p + rest_s)
    stacked = [jnp.stack(leaves) for leaves in zip(*per_layer)]
    return (y_p, y_s, *stacked)
```

```python
import functools
import math

import jax
import jax.numpy as jnp
from jax import lax
from jax.experimental import pallas as pl
from jax.experimental.pallas import tpu as pltpu

EPS = 1e-6
WINDOW = 128
N_BUCKETS = 32
MAX_DISTANCE = 128
MASKED = -1e30
VMEM_LIMIT_BYTES = 60 * 1024 * 1024
HALO = 32
_bf16 = jnp.bfloat16
_f32 = jnp.float32


def _params(*sem):
    return pltpu.CompilerParams(dimension_semantics=sem, vmem_limit_bytes=VMEM_LIMIT_BYTES)


def _dot(a, b):
    return jnp.dot(a, b, preferred_element_type=_f32)


def _sigmoid(x):
    return 0.5 * (1.0 + jnp.tanh(0.5 * x))


def _rms(x, g):
    return x * lax.rsqrt(jnp.mean(x * x, axis=-1, keepdims=True) + EPS) * g


def _inproj_kernel(*refs, n_glu, n_qkv, side):
    if side:
        (x_ref, g_ref, wa_ref, wb_ref, ba_ref, bb_ref, side_ref,
         glu_ref, qkv_ref, gate_ref, side_out_ref, xn_ref) = refs
        side_out_ref[0] = side_ref[...].astype(_bf16)
    else:
        x_ref, g_ref, wa_ref, wb_ref, ba_ref, bb_ref, glu_ref, qkv_ref, gate_ref, xn_ref = refs
    j = pl.program_id(1)

    @pl.when(j == 0)
    def _():
        xn_ref[...] = _rms(x_ref[...], g_ref[...]).astype(_bf16)

    half = wb_ref.shape[2] // 2
    halves = [slice(0, half), slice(half, 2 * half)]

    def u(w_ref, b_ref, cs):
        return _dot(xn_ref[...], w_ref[0, :, cs]) + b_ref[:, cs]

    @pl.when(j < n_glu)
    def _():
        for cs in halves:
            glu_ref[:, cs] = u(wa_ref, ba_ref, cs) * _sigmoid(u(wb_ref, bb_ref, cs))

    @pl.when((j >= n_glu) & (j < n_glu + n_qkv))
    def _():
        for cs in halves:
            qkv_ref[:, cs] = u(wb_ref, bb_ref, cs)

    @pl.when(j >= n_glu + n_qkv)
    def _():
        for cs in halves:
            gate_ref[:, cs] = _sigmoid(u(wb_ref, bb_ref, cs))


def _cast_tile_kernel(w_ref, o_ref):
    o_ref[0] = w_ref[...].astype(_bf16)


def _column_tiles(w, tn, groups=1):
    d, n = w.shape
    nt = n // groups // tn
    return pl.pallas_call(
        _cast_tile_kernel,
        grid=(nt, groups),
        in_specs=[pl.BlockSpec((d, tn), lambda c, g: (0, g * nt + c))],
        out_specs=pl.BlockSpec((1, d, tn), lambda c, g: (c, 0, g)),
        out_shape=jax.ShapeDtypeStruct((nt, d, groups * tn), _bf16),
        compiler_params=_params("arbitrary", "arbitrary"),
        name="weight_tiles",
    )(w)


def _side_block_width(n_cols, tile, steps):
    bw = 128
    while bw < tile and n_cols // bw > steps:
        bw *= 2
    return bw if tile % bw == 0 and n_cols // bw <= steps else None


def _inproj(x, g, w, b, d_glu, d_qkv, d_gate, tm=1024, side=None, side_tile=None):
    m, d = x.shape
    tm = min(tm, m)
    tn = w.shape[2]
    n_glu, n_qkv, n_gate = d_glu // tn, d_qkv // tn, d_gate // tn
    n_col = n_glu + n_qkv + n_gate
    assert w.shape[0] == n_col + n_glu
    first_w = lambda i, j: (jnp.minimum(j, n_glu - 1), 0, 0)
    rest_w = lambda i, j: (j + n_glu, 0, 0)
    first = lambda i, j: (0, jnp.minimum(j, n_glu - 1))
    rest = lambda i, j: (0, j + n_glu)
    in_specs = [
        pl.BlockSpec((tm, d), lambda i, j: (i, 0)),
        pl.BlockSpec((1, d), lambda i, j: (0, 0)),
        pl.BlockSpec((1, d, tn), first_w),
        pl.BlockSpec((1, d, tn), rest_w),
        pl.BlockSpec((1, tn), first),
        pl.BlockSpec((1, tn), rest),
    ]
    out_specs = [
        pl.BlockSpec((tm, tn), lambda i, j: (i, jnp.minimum(j, n_glu - 1))),
        pl.BlockSpec((tm, tn), lambda i, j: (i, jnp.clip(j - n_glu, 0, n_qkv - 1))),
        pl.BlockSpec((tm, tn), lambda i, j: (i, jnp.maximum(j - n_glu - n_qkv, 0))),
    ]
    out_shape = [
        jax.ShapeDtypeStruct((m, d_glu), _f32),
        jax.ShapeDtypeStruct((m, d_qkv), _f32),
        jax.ShapeDtypeStruct((m, d_gate), _f32),
    ]
    operands = [x, g, w, w, b, b]
    bw = None if side is None else _side_block_width(side.shape[1], side_tile, (m // tm) * n_col)
    if bw is not None:
        d2, n2 = side.shape
        nblk, per_tile = n2 // bw, side_tile // bw
        blk = lambda i, j: jnp.minimum(i * n_col + j, nblk - 1)
        in_specs.append(pl.BlockSpec((d2, bw), lambda i, j: (0, blk(i, j))))
        out_specs.append(pl.BlockSpec(
            (1, d2, bw), lambda i, j: ((blk(i, j) % (nblk // 2)) // per_tile, 0,
                                       (blk(i, j) // (nblk // 2)) * per_tile + blk(i, j) % per_tile)))
        out_shape.append(jax.ShapeDtypeStruct((n2 // 2 // side_tile, d2, 2 * side_tile), _bf16))
        operands.append(side)
    outs = pl.pallas_call(
        functools.partial(_inproj_kernel, n_glu=n_glu, n_qkv=n_qkv, side=bw is not None),
        grid=(m // tm, n_col),
        in_specs=in_specs,
        out_specs=out_specs,
        out_shape=out_shape,
        scratch_shapes=[pltpu.VMEM((tm, d), _bf16)],
        compiler_params=_params("arbitrary", "arbitrary"),
        name="inproj",
    )(*operands)
    return outs if bw is not None else (*outs, None)


def _ln_silu_proj(conv_ref, act_ref, lg_ref, lb_ref, w_ref, o_ref, rows):
    rt = 64

    def body(r, carry):
        r0 = pl.multiple_of(r * rt, rt)
        if len(conv_ref.shape) == 2:
            c = conv_ref[pl.ds(r0, rt), :]
        else:
            per = rt // conv_ref.shape[1]
            c = conv_ref[pl.ds(pl.multiple_of(r * per, per), per)].reshape(rt, conv_ref.shape[2])
        mu = jnp.mean(c, axis=-1, keepdims=True)
        cc = c - mu
        var = jnp.mean(cc * cc, axis=-1, keepdims=True)
        y = cc * lax.rsqrt(var + EPS) * lg_ref[...] + lb_ref[...]
        act_ref[pl.ds(r0, rt), :] = (y * _sigmoid(y)).astype(_bf16)
        return carry

    lax.fori_loop(0, rows // rt, body, 0, unroll=4)
    o_ref[...] = _dot(act_ref[...], w_ref[...])


def _conv_prompt_kernel(glu_ref, k_ref, cb_ref, lg_ref, lb_ref, w_ref, o_ref, ext_ref, conv_ref, act_ref, *, tm, width):
    d = glu_ref.shape[1]
    i = pl.program_id(1)

    @pl.when(i == 0)
    def _():
        ext_ref[0:HALO, :] = jnp.zeros((HALO, d), _f32)

    @pl.when(i > 0)
    def _():
        ext_ref[0:HALO, :] = ext_ref[tm:tm + HALO, :]

    ext_ref[HALO:HALO + tm, :] = glu_ref[...]

    rt, ct, sub = 64, 128, 8
    first = HALO - (width - 1)

    def body(r, carry):
        r0 = pl.multiple_of(r * rt, rt)
        for c in range(d // ct):
            cs = slice(c * ct, (c + 1) * ct)
            blk = ext_ref[pl.ds(r0, rt + HALO), cs]
            acc = jnp.broadcast_to(cb_ref[:, cs], (rt, ct))
            for phase in range(sub):
                taps = [w for w in range(width) if (first + w) % sub == phase]
                rows = rt if phase == 0 else rt + sub
                part = None
                for w in taps:
                    lo = first + w - phase
                    term = k_ref[w:w + 1, cs] * blk[lo:lo + rows, :]
                    part = term if part is None else part + term
                acc = acc + part[phase:phase + rt, :]
            conv_ref[pl.ds(r0, rt), cs] = acc
        return carry

    lax.fori_loop(0, tm // rt, body, 0)
    _ln_silu_proj(conv_ref, act_ref, lg_ref, lb_ref, w_ref, o_ref, tm)


def _conv_prompt(glu, conv_k, conv_b, ln_g, ln_b, w, nseq, tm=256):
    m, d = glu.shape
    nt = m // nseq // tm
    width = conv_k.shape[0]
    const = lambda b, i: (0, 0)
    return pl.pallas_call(
        functools.partial(_conv_prompt_kernel, tm=tm, width=width),
        grid=(nseq, nt),
        in_specs=[
            pl.BlockSpec((tm, d), lambda b, i: (b * nt + i, 0)),
            pl.BlockSpec((width, d), const),
            pl.BlockSpec((1, d), const),
            pl.BlockSpec((1, d), const),
            pl.BlockSpec((1, d), const),
            pl.BlockSpec((d, d), const),
        ],
        out_specs=pl.BlockSpec((tm, d), lambda b, i: (b * nt + i, 0)),
        out_shape=jax.ShapeDtypeStruct((m, d), _f32),
        scratch_shapes=[pltpu.VMEM((tm + HALO, d), _f32), pltpu.VMEM((tm, d), _f32), pltpu.VMEM((tm, d), _bf16)],
        compiler_params=_params("arbitrary", "arbitrary"),
        name="conv_prompt",
    )(glu, conv_k, conv_b, ln_g, ln_b, w)


def _conv_sample_kernel(st_ref, glu_ref, k_ref, cb_ref, lg_ref, lb_ref, w_ref, o_ref, ns_ref,
                        ext_ref, conv_ref, act_ref, *, nb, t, width):
    d = glu_ref.shape[2]
    hist = width - 1
    ct, sub = 128, 8

    def history(n, carry):
        ext_ref[0:hist, :] = st_ref[0, n]
        ext_ref[hist:hist + t, :] = glu_ref[n]
        ns_ref[0, n] = ext_ref[t:t + hist, :]
        return carry

    lax.fori_loop(0, nb, history, 0, unroll=2)

    def group(gi, carry):
        seqs = pl.ds(pl.multiple_of(gi * sub, sub), sub)
        for c in range(d // ct):
            cs = slice(c * ct, (c + 1) * ct)
            x = [st_ref[0, seqs, j, cs] for j in range(hist)] + [glu_ref[seqs, j, cs] for j in range(t)]
            for tt in range(t):
                acc = jnp.broadcast_to(cb_ref[:, cs], (sub, ct))
                for w in range(width):
                    acc = acc + k_ref[w:w + 1, cs] * x[tt + w]
                conv_ref[seqs, tt, cs] = acc
        return carry

    lax.fori_loop(0, nb // sub, group, 0)
    _ln_silu_proj(conv_ref, act_ref, lg_ref, lb_ref, w_ref, o_ref, nb * t)


def _conv_sample(state, layer, glu, conv_k, conv_b, ln_g, ln_b, w, nb=32):
    _, n, hist, d = state.shape
    width = conv_k.shape[0]
    t = glu.shape[0] // n
    assert hist == width - 1 and t <= hist
    const = lambda i: (0, 0)
    return pl.pallas_call(
        functools.partial(_conv_sample_kernel, nb=nb, t=t, width=width),
        grid=(n // nb,),
        in_specs=[
            pl.BlockSpec((1, nb, hist, d), lambda i: (layer, i, 0, 0)),
            pl.BlockSpec((nb, t, d), lambda i: (i, 0, 0)),
            pl.BlockSpec((width, d), const),
            pl.BlockSpec((1, d), const),
            pl.BlockSpec((1, d), const),
            pl.BlockSpec((1, d), const),
            pl.BlockSpec((d, d), const),
        ],
        out_specs=[
            pl.BlockSpec((nb * t, d), lambda i: (i, 0)),
            pl.BlockSpec((1, nb, hist, d), lambda i: (0, i, 0, 0)),
        ],
        out_shape=[
            jax.ShapeDtypeStruct((n * t, d), _f32),
            jax.ShapeDtypeStruct((1, n, hist, d), _f32),
        ],
        scratch_shapes=[pltpu.VMEM((hist + t + (-(hist + t)) % 8, d), _f32), pltpu.VMEM((nb, t, d), _f32),
                        pltpu.VMEM((nb * t, d), _bf16)],
        compiler_params=_params("arbitrary"),
        name="conv_sample",
    )(state, glu.reshape(n, t, d), conv_k, conv_b, ln_g, ln_b, w)


def _attn_prompt_kernel(q_ref, kc_ref, kp_ref, vc_ref, vp_ref, bias_ref, sink_ref, *rest, kv_heads, hd, side):
    o_ref = rest[side]
    for side_ref, side_out_ref in zip(rest[:side], rest[side + 1:]):
        side_out_ref[...] = side_ref[...].astype(_bf16)
    blk = q_ref.shape[0]
    grp = 2 * hd
    pairs = q_ref.shape[1] // (kv_heads * grp)
    cols = pairs * blk
    scale = hd ** -0.5
    nt = (((1,), (1,)), ((), ()))
    low = lax.broadcasted_iota(jnp.int32, (2 * blk, grp), 1) < hd
    qi = lax.broadcasted_iota(jnp.int32, (blk, cols), 1) & (blk - 1)
    cur_visible = lax.broadcasted_iota(jnp.int32, (blk, cols), 0) <= qi

    def halves(p_ref, c_ref, h):
        g0 = (h // 2) * grp
        x = jnp.concatenate([p_ref[:, g0:g0 + grp], c_ref[:, g0:g0 + grp]], axis=0)
        swapped = pltpu.roll(x, hd, axis=1)
        in_low, in_high = (x, swapped) if h % 2 == 0 else (swapped, x)
        return jnp.where(low, in_low, 0.0), jnp.where(low, 0.0, in_high)

    for h in range(kv_heads):
        ks = [x.astype(_bf16) for x in halves(kp_ref, kc_ref, h)]
        vts = [x.T.astype(_bf16) for x in halves(vp_ref, vc_ref, h)]
        lanes = [slice((h * pairs + p) * grp, (h * pairs + p + 1) * grp) for p in range(pairs)]
        qs = (jnp.concatenate([q_ref[:, ls] for ls in lanes], axis=0) * scale).astype(_bf16)
        out_t = None
        for parity in range(2):
            s2 = lax.dot_general(ks[parity], qs, nt, preferred_element_type=_f32)
            s = jnp.where(cur_visible, s2[blk:, :], s2[:blk, :]) + bias_ref[0, h, parity]
            sink = sink_ref[h, parity]
            m = jnp.maximum(jnp.max(s, axis=0, keepdims=True), sink)
            p = jnp.exp(s - m)
            den = jnp.sum(p, axis=0, keepdims=True) + jnp.exp(sink - m)
            pn = p * (1.0 / den)
            p2 = jnp.concatenate([jnp.where(cur_visible, 0.0, pn), jnp.where(cur_visible, pn, 0.0)], axis=0)
            o_t = _dot(vts[parity], p2.astype(_bf16))
            out_t = o_t if out_t is None else out_t + o_t
        out = out_t.T
        for p, ls in enumerate(lanes):
            o_ref[:, ls] = out[p * blk:(p + 1) * blk, :]


def _attn_prompt(qkv, bias, sinks, nseq, n_heads, kv_heads, hd, sides=()):
    m = qkv.shape[0]
    dq, dkv = n_heads * hd, kv_heads * hd
    nb = m // nseq // WINDOW
    pairs = n_heads // kv_heads // 2
    rows = pairs * WINDOW
    assert WINDOW & (WINDOW - 1) == 0 and 2 * hd == 128 and kv_heads % 2 == 0
    qi = jnp.arange(WINDOW)[:, None]
    cur_visible = jnp.arange(WINDOW)[None, :] <= qi
    folded = jnp.stack([jnp.where(cur_visible, bias[:, :, WINDOW:], MASKED),
                        jnp.where(cur_visible, bias[:, :, WINDOW:], bias[:, :, :WINDOW])])
    folded = folded.reshape(2, kv_heads, pairs, 2, WINDOW, WINDOW).transpose(0, 1, 3, 5, 2, 4)
    folded = folded.reshape(2, kv_heads, 2, WINDOW, rows)
    sink_rows = jnp.repeat(sinks.reshape(kv_heads, pairs, 2).transpose(0, 2, 1), WINDOW, axis=-1)
    sink_rows = sink_rows.reshape(kv_heads, 2, 1, rows)
    kcol, vcol = dq // dkv, dq // dkv + 1
    cur = lambda b, i: b * nb + i
    prev = lambda b, i: b * nb + jnp.maximum(i - 1, 0)
    in_specs = [
        pl.BlockSpec((WINDOW, dq), lambda b, i: (cur(b, i), 0)),
        pl.BlockSpec((WINDOW, dkv), lambda b, i: (cur(b, i), kcol)),
        pl.BlockSpec((WINDOW, dkv), lambda b, i: (prev(b, i), kcol)),
        pl.BlockSpec((WINDOW, dkv), lambda b, i: (cur(b, i), vcol)),
        pl.BlockSpec((WINDOW, dkv), lambda b, i: (prev(b, i), vcol)),
        pl.BlockSpec((1, kv_heads, 2, WINDOW, rows), lambda b, i: (jnp.minimum(i, 1), 0, 0, 0, 0)),
        pl.BlockSpec((kv_heads, 2, 1, rows), lambda b, i: (0, 0, 0, 0)),
    ]
    out_specs = [pl.BlockSpec((WINDOW, dq), lambda b, i: (cur(b, i), 0))]
    out_shape = [jax.ShapeDtypeStruct((m, dq), _f32)]
    operands = [qkv, qkv, qkv, qkv, qkv, folded, sink_rows]
    steps = nseq * nb
    ride = [w for w in sides if w.shape[0] % (16 * steps) == 0]
    for w in ride:
        block = (w.shape[0] // steps, w.shape[1])
        in_specs.append(pl.BlockSpec(block, lambda b, i: (cur(b, i), 0)))
        out_specs.append(pl.BlockSpec(block, lambda b, i: (cur(b, i), 0)))
        out_shape.append(jax.ShapeDtypeStruct(w.shape, _bf16))
        operands.append(w)
    outs = pl.pallas_call(
        functools.partial(_attn_prompt_kernel, kv_heads=kv_heads, hd=hd, side=len(ride)),
        grid=(nseq, nb),
        in_specs=in_specs,
        out_specs=out_specs,
        out_shape=out_shape,
        compiler_params=_params("arbitrary", "arbitrary"),
        name="attn_prompt",
    )(*operands)
    cast = iter(outs[1:])
    return outs[0], [next(cast) if any(w is r for r in ride) else w.astype(_bf16) for w in sides]


def _attn_sample_kernel(qkv_ref, ck_ref, cv_ref, bias_ref, sink_ref, o_ref, ko_ref, vo_ref, *,
                        nb, t, n_heads, kv_heads, hd):
    dq, dkv = n_heads * hd, kv_heads * hd
    group = n_heads // kv_heads
    keep = ck_ref.shape[1]
    keys = bias_ref.shape[0]
    grp = 2 * hd
    scale = hd ** -0.5
    nt = (((1,), (1,)), ((), ()))
    low = lax.broadcasted_iota(jnp.int32, (t, grp), 1) < hd
    blank = jnp.zeros((t, grp), _f32)
    pad = jnp.zeros((keys - keep - t, dkv), _f32)

    def body(n, carry):
        r0 = pl.multiple_of(n * t, t)
        q = qkv_ref[pl.ds(r0, t), 0:dq]
        kn = qkv_ref[pl.ds(r0, t), dq:dq + dkv]
        vn = qkv_ref[pl.ds(r0, t), dq + dkv:dq + 2 * dkv]
        ck, cv = ck_ref[n], cv_ref[n]
        ko_ref[n, 0:keep - t, :] = ck[t:, :]
        ko_ref[n, keep - t:keep, :] = kn
        vo_ref[n, 0:keep - t, :] = cv[t:, :]
        vo_ref[n, keep - t:keep, :] = vn
        kp = jnp.concatenate([ck, kn, pad], axis=0).astype(_bf16)
        vp = jnp.concatenate([cv, vn, pad], axis=0).astype(_bf16)

        rows = []
        for h in range(kv_heads):
            for pair in range(group // 2):
                qv = q[:, (h * (group // 2) + pair) * grp:(h * (group // 2) + pair + 1) * grp]
                swapped = pltpu.roll(qv, hd, axis=1)
                for parity in range(2):
                    x = qv if parity == h % 2 else swapped
                    x = jnp.where(low, x, 0.0) if h % 2 == 0 else jnp.where(low, 0.0, x)
                    pieces = [blank] * (dkv // grp)
                    pieces[h // 2] = x
                    rows.append(jnp.concatenate(pieces, axis=1))
        wt = (jnp.concatenate(rows, axis=0) * scale).astype(_bf16)

        st = lax.dot_general(kp, wt, nt, preferred_element_type=_f32) + bias_ref[...]
        sink = sink_ref[...]
        m = jnp.maximum(jnp.max(st, axis=0, keepdims=True), sink)
        p = jnp.exp(st - m)
        den = jnp.sum(p, axis=0, keepdims=True) + jnp.exp(sink - m)
        pn = p * (1.0 / den)
        of = _dot(pn.T.astype(_bf16), vp)

        outs = []
        for h in range(kv_heads):
            ls = slice((h // 2) * grp, (h // 2 + 1) * grp)
            for pair in range(group // 2):
                r = (h * group + 2 * pair) * t
                even, odd = of[r:r + t, ls], of[r + t:r + 2 * t, ls]
                if h % 2 == 0:
                    outs.append(jnp.where(low, even, pltpu.roll(odd, hd, axis=1)))
                else:
                    outs.append(jnp.where(low, pltpu.roll(even, hd, axis=1), odd))
        o_ref[pl.ds(r0, t), :] = jnp.concatenate(outs, axis=1)
        return carry

    lax.fori_loop(0, nb, body, 0, unroll=2)


def _attn_sample(qkv, cache_k, cache_v, bias, sinks, t, n_heads, kv_heads, hd, nb=8):
    n, keep, dkv = cache_k.shape
    dq = n_heads * hd
    keys = bias.shape[2]
    assert keep == WINDOW and keep + t <= keys and 2 * hd == 128 and kv_heads % 2 == 0
    bias_t = bias[:, :t].transpose(2, 0, 1).reshape(keys, n_heads * t)
    sink_row = jnp.repeat(sinks, t).reshape(1, n_heads * t)
    const = lambda i: (0, 0)
    return pl.pallas_call(
        functools.partial(_attn_sample_kernel, nb=nb, t=t, n_heads=n_heads, kv_heads=kv_heads, hd=hd),
        grid=(n // nb,),
        in_specs=[
            pl.BlockSpec((nb * t, qkv.shape[1]), lambda i: (i, 0)),
            pl.BlockSpec((nb, keep, dkv), lambda i: (i, 0, 0)),
            pl.BlockSpec((nb, keep, dkv), lambda i: (i, 0, 0)),
            pl.BlockSpec((keys, n_heads * t), const),
            pl.BlockSpec((1, n_heads * t), const),
        ],
        out_specs=[
            pl.BlockSpec((nb * t, dq), lambda i: (i, 0)),
            pl.BlockSpec((nb, keep, dkv), lambda i: (i, 0, 0)),
            pl.BlockSpec((nb, keep, dkv), lambda i: (i, 0, 0)),
        ],
        out_shape=[
            jax.ShapeDtypeStruct((n * t, dq), _f32),
            jax.ShapeDtypeStruct((n, keep, dkv), _f32),
            jax.ShapeDtypeStruct((n, keep, dkv), _f32),
        ],
        compiler_params=_params("arbitrary"),
        name="attn_sample",
    )(qkv, cache_k, cache_v, bias_t, sink_row)


def _mix_kernel(attn_ref, conv_ref, gc_ref, ga_ref, x_ref, wa_ref, wo_ref, g_ref, h_ref):
    ao = _dot(attn_ref[...].astype(_bf16), wa_ref[...])
    mixed = gc_ref[...] * conv_ref[...] + ga_ref[...] * ao
    o = _dot(mixed.astype(_bf16), wo_ref[...])
    h_ref[...] = x_ref[...] + _rms(o, g_ref[...])


def _mix(attn, conv_out, gates, x, w_attn, w_out, g_post, tm=256):
    m, d = x.shape
    row = lambda i: (i, 0)
    const = lambda i: (0, 0)
    once = pl.Buffered(1)
    return pl.pallas_call(
        _mix_kernel,
        grid=(m // tm,),
        in_specs=[
            pl.BlockSpec((tm, d), row),
            pl.BlockSpec((tm, d), row),
            pl.BlockSpec((tm, d), lambda i: (i, 0)),
            pl.BlockSpec((tm, d), lambda i: (i, 1)),
            pl.BlockSpec((tm, d), row),
            pl.BlockSpec((d, d), const, pipeline_mode=once),
            pl.BlockSpec((d, d), const, pipeline_mode=once),
            pl.BlockSpec((1, d), const),
        ],
        out_specs=pl.BlockSpec((tm, d), row),
        out_shape=jax.ShapeDtypeStruct((m, d), _f32),
        compiler_params=_params("arbitrary"),
        name="mix",
    )(attn, conv_out, gates, gates, x, w_attn, w_out, g_post)


def _gelu_tanh(x):
    return 0.5 * x * (1.0 + jnp.tanh(math.sqrt(2.0 / math.pi) * (x + 0.044715 * (x * x * x))))


MXU_COLUMNS = 256
FFN_CHUNK = 1024


def _chunks(n, width=MXU_COLUMNS):
    return [slice(a, a + width) for a in range(0, n, width)]


def _ffn_chunks(tc, act_ref, wd_ref, acc_ref, gate_val):
    for cs in _chunks(tc):
        gate, val = gate_val(cs)
        act_ref[:, cs] = (_gelu_tanh(gate) * val).astype(_bf16)
    acc_ref[...] += _dot(act_ref[...], wd_ref[...])


def _ffn_finish(c, nc, acc_ref, h_ref, gp_ref, y_ref):
    @pl.when(c == nc - 1)
    def _():
        y_ref[...] = h_ref[...] + _rms(acc_ref[...], gp_ref[...])


def _ffn_prompt_kernel(h_ref, g_ref, w_ref, kg_ref, kv_ref, bg_ref, bv_ref, wd_ref, gp_ref,
                       y_ref, tg_ref, tv_ref, hn_ref, acc_ref, act_ref, ug_ref, uv_ref, cg_ref, cv_ref, *, tm, nc):
    i, c = pl.program_id(1), pl.program_id(2)
    tc = w_ref.shape[2] // 2

    @pl.when(c == 0)
    def _():
        hn_ref[...] = _rms(h_ref[...], g_ref[...]).astype(_bf16)
        acc_ref[...] = jnp.zeros_like(acc_ref)

    @pl.when(i == 0)
    def _():
        ug_ref[0:8, :] = jnp.zeros((8, tc), _f32)
        uv_ref[0:8, :] = jnp.zeros((8, tc), _f32)

    @pl.when(i > 0)
    def _():
        ug_ref[0:8, :] = cg_ref[c]
        uv_ref[0:8, :] = cv_ref[c]

    def conv(u_ref, col0, k_ref, b_ref, cs):
        u_ref[8:8 + tm, cs] = _dot(hn_ref[...], w_ref[0, :, col0 + cs.start:col0 + cs.stop])
        return (k_ref[2:3, cs] * u_ref[8:8 + tm, cs] + k_ref[1:2, cs] * u_ref[7:7 + tm, cs]
                + k_ref[0:1, cs] * u_ref[6:6 + tm, cs] + b_ref[:, cs])

    _ffn_chunks(tc, act_ref, wd_ref, acc_ref,
                lambda cs: (conv(ug_ref, 0, kg_ref, bg_ref, cs), conv(uv_ref, tc, kv_ref, bv_ref, cs)))
    cg_ref[c] = ug_ref[tm:tm + 8, :]
    cv_ref[c] = uv_ref[tm:tm + 8, :]
    tg_ref[0, 0] = ug_ref[tm + 6:tm + 8, :]
    tv_ref[0, 0] = uv_ref[tm + 6:tm + 8, :]
    _ffn_finish(c, nc, acc_ref, h_ref, gp_ref, y_ref)


def _ffn_prompt(h, g_pre, w_up, ffn_k, ffn_b, w_down, g_post, nseq, tm=512):
    m, d = h.shape
    dff = w_down.shape[0]
    nc, tc = w_up.shape[0], w_up.shape[2] // 2
    nt = m // nseq // tm
    width = ffn_k.shape[0]
    assert width == 3 and nc * tc == dff
    row = lambda b, i, c: (b * nt + i, 0)
    const = lambda b, i, c: (0, 0)
    lo = lambda b, i, c: (0, c)
    hi = lambda b, i, c: (0, c + nc)
    y, tg, tv = pl.pallas_call(
        functools.partial(_ffn_prompt_kernel, tm=tm, nc=nc),
        grid=(nseq, nt, nc),
        in_specs=[
            pl.BlockSpec((tm, d), row),
            pl.BlockSpec((1, d), const),
            pl.BlockSpec((1, d, 2 * tc), lambda b, i, c: (c, 0, 0)),
            pl.BlockSpec((width, tc), lo),
            pl.BlockSpec((width, tc), hi),
            pl.BlockSpec((1, tc), lo),
            pl.BlockSpec((1, tc), hi),
            pl.BlockSpec((tc, d), lambda b, i, c: (c, 0)),
            pl.BlockSpec((1, d), const),
        ],
        out_specs=[
            pl.BlockSpec((tm, d), row, pipeline_mode=pl.Buffered(1)),
            pl.BlockSpec((1, 1, width - 1, tc), lambda b, i, c: (b, i, 0, c)),
            pl.BlockSpec((1, 1, width - 1, tc), lambda b, i, c: (b, i, 0, c)),
        ],
        out_shape=[
            jax.ShapeDtypeStruct((m, d), _f32),
            jax.ShapeDtypeStruct((nseq, nt, width - 1, dff), _f32),
            jax.ShapeDtypeStruct((nseq, nt, width - 1, dff), _f32),
        ],
        scratch_shapes=[
            pltpu.VMEM((tm, d), _bf16),
            pltpu.VMEM((tm, d), _f32),
            pltpu.VMEM((tm, tc), _bf16),
            pltpu.VMEM((tm + 8, tc), _f32),
            pltpu.VMEM((tm + 8, tc), _f32),
            pltpu.VMEM((nc, 8, tc), _f32),
            pltpu.VMEM((nc, 8, tc), _f32),
        ],
        compiler_params=_params("arbitrary", "arbitrary", "arbitrary"),
        name="ffn_prompt",
    )(h, g_pre, w_up, ffn_k, ffn_k, ffn_b, ffn_b, w_down, g_post)
    return y, jnp.concatenate([tg[:, -1], tv[:, -1]], axis=-1)


def _ffn_sample_kernel(h_ref, g_ref, w_ref, kg_ref, kv_ref, bg_ref, bv_ref, sg_ref, sv_ref, wd_ref, gp_ref,
                       y_ref, tg_ref, tv_ref, hn_ref, acc_ref, act_ref, ug_ref, uv_ref, *, nb, t, nc):
    c = pl.program_id(1)
    tc = w_ref.shape[2] // 2

    @pl.when(c == 0)
    def _():
        hn_ref[...] = _rms(h_ref[...], g_ref[...]).astype(_bf16)
        acc_ref[...] = jnp.zeros_like(acc_ref)

    ug_ref[:, 6:8, :] = sg_ref[...]
    uv_ref[:, 6:8, :] = sv_ref[...]

    def conv(u_ref, col0, k_ref, b_ref, cs):
        width = cs.stop - cs.start
        u_ref[:, 8:8 + t, cs] = _dot(hn_ref[...], w_ref[0, :, col0 + cs.start:col0 + cs.stop]).reshape(nb, t, width)
        cv = (k_ref[2:3, cs] * u_ref[:, 8:8 + t, cs] + k_ref[1:2, cs] * u_ref[:, 7:7 + t, cs]
              + k_ref[0:1, cs] * u_ref[:, 6:6 + t, cs] + b_ref[:, cs])
        return cv.reshape(nb * t, width)

    _ffn_chunks(tc, act_ref, wd_ref, acc_ref,
                lambda cs: (conv(ug_ref, 0, kg_ref, bg_ref, cs), conv(uv_ref, tc, kv_ref, bv_ref, cs)))
    tg_ref[...] = ug_ref[:, 6 + t:8 + t, :]
    tv_ref[...] = uv_ref[:, 6 + t:8 + t, :]
    _ffn_finish(c, nc, acc_ref, h_ref, gp_ref, y_ref)


def _ffn_sample(h, state, g_pre, w_up, ffn_k, ffn_b, w_down, g_post, t, nb=64):
    m, d = h.shape
    n = m // t
    dff = w_down.shape[0]
    nc, tc = w_up.shape[0], w_up.shape[2] // 2
    width = ffn_k.shape[0]
    assert width == 3 and t >= width - 1 and nc * tc == dff
    tm = nb * t
    row = lambda i, c: (i, 0)
    const = lambda i, c: (0, 0)
    lo = lambda i, c: (0, c)
    hi = lambda i, c: (0, c + nc)
    y, tg, tv = pl.pallas_call(
        functools.partial(_ffn_sample_kernel, nb=nb, t=t, nc=nc),
        grid=(n // nb, nc),
        in_specs=[
            pl.BlockSpec((tm, d), row, pipeline_mode=pl.Buffered(1)),
            pl.BlockSpec((1, d), const),
            pl.BlockSpec((1, d, 2 * tc), lambda i, c: (c, 0, 0)),
            pl.BlockSpec((width, tc), lo),
            pl.BlockSpec((width, tc), hi),
            pl.BlockSpec((1, tc), lo),
            pl.BlockSpec((1, tc), hi),
            pl.BlockSpec((nb, width - 1, tc), lambda i, c: (i, 0, c)),
            pl.BlockSpec((nb, width - 1, tc), lambda i, c: (i, 0, c + nc)),
            pl.BlockSpec((tc, d), lambda i, c: (c, 0)),
            pl.BlockSpec((1, d), const),
        ],
        out_specs=[
            pl.BlockSpec((tm, d), row, pipeline_mode=pl.Buffered(1)),
            pl.BlockSpec((nb, width - 1, tc), lambda i, c: (i, 0, c)),
            pl.BlockSpec((nb, width - 1, tc), lambda i, c: (i, 0, c)),
        ],
        out_shape=[
            jax.ShapeDtypeStruct((m, d), _f32),
            jax.ShapeDtypeStruct((n, width - 1, dff), _f32),
            jax.ShapeDtypeStruct((n, width - 1, dff), _f32),
        ],
        scratch_shapes=[
            pltpu.VMEM((tm, d), _bf16),
            pltpu.VMEM((tm, d), _f32),
            pltpu.VMEM((tm, tc), _bf16),
            pltpu.VMEM((nb, 8 + t, tc), _f32),
            pltpu.VMEM((nb, 8 + t, tc), _f32),
        ],
        compiler_params=_params("arbitrary", "arbitrary"),
        name="ffn_sample",
    )(h, g_pre, w_up, ffn_k, ffn_k, ffn_b, ffn_b, state, state, w_down, g_post)
    return y, jnp.concatenate([tg, tv], axis=-1)


def _rel_bucket(dist):
    max_exact = N_BUCKETS // 2
    dd = jnp.maximum(dist, 1).astype(_f32)
    large = max_exact + (jnp.log(dd / max_exact) / math.log(MAX_DISTANCE / max_exact)
                         * (N_BUCKETS - max_exact)).astype(jnp.int32)
    large = jnp.minimum(large, N_BUCKETS - 1)
    return jnp.where(dist < max_exact, dist, large)


def _bias_table(rel_bias):
    n = WINDOW
    by_dist = rel_bias[_rel_bucket(jnp.arange(n))].astype(_f32).T
    g = jnp.concatenate([jnp.full_like(by_dist, MASKED), by_dist[:, ::-1],
                         jnp.full_like(by_dist, MASKED)], axis=1)
    heads, length = g.shape
    skew = jnp.broadcast_to(g[:, None, :], (heads, n, length)).reshape(heads, n * length)
    skew = skew[:, :n * (length - 1)].reshape(heads, n, length - 1)
    return skew[:, :, n - 1:3 * n - 1]


def _row(v):
    return v.reshape(1, -1)


def _layer(xp, xs, cache_k, cache_v, state_conv, layer, st_ffn, p):
    (g_pre, w_in, b_in, conv_k, conv_b, ln_g, ln_b, w_conv, sinks, rel_bias, w_attn, w_out, g_post,
     g_ffn_pre, w_up, ffn_k, ffn_b, w_down, g_ffn_post) = p
    nseq, seq, d = xp.shape
    t = xs.shape[1]
    n_heads = sinks.shape[0]
    keep, kv_heads, hd = cache_k.shape[1:]
    dq, dkv = n_heads * hd, kv_heads * hd
    d_conv = conv_k.shape[1]
    assert d_conv == d and dq == d and keep == WINDOW and seq % WINDOW == 0

    w_in_b = _column_tiles(w_in, 512)
    ffn_tile = min(FFN_CHUNK, w_down.shape[0])
    w_up_b = None
    g_pre, b_in, conv_b, ln_g, ln_b, g_post, g_ffn_pre, ffn_b, g_ffn_post = map(
        _row, (g_pre, b_in, conv_b, ln_g, ln_b, g_post, g_ffn_pre, ffn_b, g_ffn_post))

    bias = _bias_table(rel_bias)

    outs = []
    for x3, is_prompt in ((xp, True), (xs, False)):
        n = x3.shape[0]
        x = x3.reshape(-1, d)
        glu, qkv, gates, tiles = _inproj(x, g_pre, w_in_b, b_in, d_conv, dq + 2 * dkv, 2 * d,
                                         side=w_up if w_up_b is None else None, side_tile=ffn_tile)
        if w_up_b is None:
            w_up_b = tiles if tiles is not None else _column_tiles(w_up, ffn_tile, groups=2)
        if is_prompt:
            attn, (w_down_b, w_conv_b, w_attn_b, w_out_b) = _attn_prompt(
                qkv, bias, sinks.astype(_f32), nseq, n_heads, kv_heads, hd, sides=(w_down, w_conv, w_attn, w_out))
            conv_out = _conv_prompt(glu, conv_k, conv_b, ln_g, ln_b, w_conv_b, nseq)
            new_conv = glu.reshape(n, seq, d)[:, seq - (conv_k.shape[0] - 1):]
            last = qkv.reshape(n, seq, -1)[:, seq - keep:]
            new_k = last[:, :, dq:dq + dkv].reshape(n, keep, kv_heads, hd)
            new_v = last[:, :, dq + dkv:].reshape(n, keep, kv_heads, hd)
        else:
            conv_out, new_conv = _conv_sample(state_conv, layer, glu, conv_k, conv_b, ln_g, ln_b, w_conv_b)
            new_conv = new_conv[0]
            attn, new_k, new_v = _attn_sample(qkv, cache_k.reshape(n, keep, dkv), cache_v.reshape(n, keep, dkv), bias,
                                              sinks.astype(_f32), t, n_heads, kv_heads, hd)
            new_k, new_v = (a.reshape(n, keep, kv_heads, hd) for a in (new_k, new_v))
        h = _mix(attn, conv_out, gates, x, w_attn_b, w_out_b, g_post)
        if is_prompt:
            y, new_ffn = _ffn_prompt(h, g_ffn_pre, w_up_b, ffn_k, ffn_b, w_down_b, g_ffn_post, nseq)
        else:
            y, new_ffn = _ffn_sample(h, st_ffn, g_ffn_pre, w_up_b, ffn_k, ffn_b, w_down_b, g_ffn_post, t)
        outs.append((y.reshape(x3.shape), new_k, new_v, new_conv, new_ffn))
    return outs


def kernel(x_prompt, x_sample, cache_k, cache_v, state_conv, state_ffn_conv, norm_mix_pre, w_in, b_in, conv_dw_k, conv_dw_b, conv_ln_g, conv_ln_b, w_conv_proj, attn_sinks, rel_bias, w_attn_proj, w_out, norm_mix_post, norm_ffn_pre, w_up, ffn_dw_k, ffn_dw_b, w_down, norm_ffn_post):
    y_p, y_s = x_prompt, x_sample
    per_layer = []
    for l in range(w_in.shape[0]):
        p = (norm_mix_pre[l], w_in[l], b_in[l], conv_dw_k[l], conv_dw_b[l], conv_ln_g[l], conv_ln_b[l], w_conv_proj[l],
             attn_sinks[l], rel_bias, w_attn_proj[l], w_out[l], norm_mix_post[l],
             norm_ffn_pre[l], w_up[l], ffn_dw_k[l], ffn_dw_b[l], w_down[l], norm_ffn_post[l])
        (y_p, *rest_p), (y_s, *rest_s) = _layer(y_p, y_s, cache_k[l], cache_v[l], state_conv, l, state_ffn_conv[l], p)
        per_layer.append(rest_p + rest_s)
    stacked = [jnp.stack(leaves) for leaves in zip(*per_layer)]
    return (y_p, y_s, *stacked)
```

```python
import functools
import math

import jax
import jax.numpy as jnp
from jax import lax
from jax.experimental import pallas as pl
from jax.experimental.pallas import tpu as pltpu

EPS = 1e-6
WINDOW = 128
N_BUCKETS = 32
MAX_DISTANCE = 128
MASKED = -1e30
VMEM_LIMIT_BYTES = 60 * 1024 * 1024
HALO = 32
_bf16 = jnp.bfloat16
_f32 = jnp.float32


def _params(*sem):
    return pltpu.CompilerParams(dimension_semantics=sem, vmem_limit_bytes=VMEM_LIMIT_BYTES)


def _dot(a, b):
    return jnp.dot(a, b, preferred_element_type=_f32)


def _sigmoid(x):
    return 0.5 * (1.0 + jnp.tanh(0.5 * x))


def _rms(x, g):
    return x * lax.rsqrt(jnp.mean(x * x, axis=-1, keepdims=True) + EPS) * g


def _inproj_kernel(*refs, n_glu, n_qkv, side):
    if side:
        (x_ref, g_ref, wa_ref, wb_ref, ba_ref, bb_ref, side_ref,
         glu_ref, qkv_ref, gate_ref, side_out_ref, xn_ref) = refs
        side_out_ref[0] = side_ref[...].astype(_bf16)
    else:
        x_ref, g_ref, wa_ref, wb_ref, ba_ref, bb_ref, glu_ref, qkv_ref, gate_ref, xn_ref = refs
    j = pl.program_id(1)

    @pl.when(j == 0)
    def _():
        xn_ref[...] = _rms(x_ref[...], g_ref[...]).astype(_bf16)

    half = wb_ref.shape[2] // 2
    halves = [slice(0, half), slice(half, 2 * half)]

    def u(w_ref, b_ref, cs):
        return _dot(xn_ref[...], w_ref[0, :, cs]) + b_ref[:, cs]

    @pl.when(j < n_glu)
    def _():
        for cs in halves:
            glu_ref[:, cs] = u(wa_ref, ba_ref, cs) * _sigmoid(u(wb_ref, bb_ref, cs))

    @pl.when((j >= n_glu) & (j < n_glu + n_qkv))
    def _():
        for cs in halves:
            qkv_ref[:, cs] = u(wb_ref, bb_ref, cs)

    @pl.when(j >= n_glu + n_qkv)
    def _():
        for cs in halves:
            gate_ref[:, cs] = _sigmoid(u(wb_ref, bb_ref, cs))


def _cast_tile_kernel(w_ref, o_ref):
    o_ref[0] = w_ref[...].astype(_bf16)


def _column_tiles(w, tn, groups=1):
    d, n = w.shape
    nt = n // groups // tn
    return pl.pallas_call(
        _cast_tile_kernel,
        grid=(nt, groups),
        in_specs=[pl.BlockSpec((d, tn), lambda c, g: (0, g * nt + c))],
        out_specs=pl.BlockSpec((1, d, tn), lambda c, g: (c, 0, g)),
        out_shape=jax.ShapeDtypeStruct((nt, d, groups * tn), _bf16),
        compiler_params=_params("arbitrary", "arbitrary"),
        name="weight_tiles",
    )(w)


def _side_block_width(n_cols, tile, steps):
    bw = 128
    while bw < tile and n_cols // bw > steps:
        bw *= 2
    return bw if tile % bw == 0 and n_cols // bw <= steps else None


def _inproj(x, g, w, b, d_glu, d_qkv, d_gate, tm=1024, side=None, side_tile=None):
    m, d = x.shape
    tm = min(tm, m)
    tn = w.shape[2]
    n_glu, n_qkv, n_gate = d_glu // tn, d_qkv // tn, d_gate // tn
    n_col = n_glu + n_qkv + n_gate
    assert w.shape[0] == n_col + n_glu
    first_w = lambda i, j: (jnp.minimum(j, n_glu - 1), 0, 0)
    rest_w = lambda i, j: (j + n_glu, 0, 0)
    first = lambda i, j: (0, jnp.minimum(j, n_glu - 1))
    rest = lambda i, j: (0, j + n_glu)
    in_specs = [
        pl.BlockSpec((tm, d), lambda i, j: (i, 0)),
        pl.BlockSpec((1, d), lambda i, j: (0, 0)),
        pl.BlockSpec((1, d, tn), first_w),
        pl.BlockSpec((1, d, tn), rest_w),
        pl.BlockSpec((1, tn), first),
        pl.BlockSpec((1, tn), rest),
    ]
    out_specs = [
        pl.BlockSpec((tm, tn), lambda i, j: (i, jnp.minimum(j, n_glu - 1))),
        pl.BlockSpec((tm, tn), lambda i, j: (i, jnp.clip(j - n_glu, 0, n_qkv - 1))),
        pl.BlockSpec((tm, tn), lambda i, j: (i, jnp.maximum(j - n_glu - n_qkv, 0))),
    ]
    out_shape = [
        jax.ShapeDtypeStruct((m, d_glu), _f32),
        jax.ShapeDtypeStruct((m, d_qkv), _f32),
        jax.ShapeDtypeStruct((m, d_gate), _f32),
    ]
    operands = [x, g, w, w, b, b]
    bw = None if side is None else _side_block_width(side.shape[1], side_tile, (m // tm) * n_col)
    if bw is not None:
        d2, n2 = side.shape
        nblk, per_tile = n2 // bw, side_tile // bw
        blk = lambda i, j: jnp.minimum(i * n_col + j, nblk - 1)
        in_specs.append(pl.BlockSpec((d2, bw), lambda i, j: (0, blk(i, j))))
        out_specs.append(pl.BlockSpec(
            (1, d2, bw), lambda i, j: ((blk(i, j) % (nblk // 2)) // per_tile, 0,
                                       (blk(i, j) // (nblk // 2)) * per_tile + blk(i, j) % per_tile)))
        out_shape.append(jax.ShapeDtypeStruct((n2 // 2 // side_tile, d2, 2 * side_tile), _bf16))
        operands.append(side)
    outs = pl.pallas_call(
        functools.partial(_inproj_kernel, n_glu=n_glu, n_qkv=n_qkv, side=bw is not None),
        grid=(m // tm, n_col),
        in_specs=in_specs,
        out_specs=out_specs,
        out_shape=out_shape,
        scratch_shapes=[pltpu.VMEM((tm, d), _bf16)],
        compiler_params=_params("arbitrary", "arbitrary"),
        name="inproj",
    )(*operands)
    return outs if bw is not None else (*outs, None)


def _ln_silu_proj(conv_ref, act_ref, lg_ref, lb_ref, w_ref, o_ref, rows):
    rt = 64

    def body(r, carry):
        r0 = pl.multiple_of(r * rt, rt)
        if len(conv_ref.shape) == 2:
            c = conv_ref[pl.ds(r0, rt), :]
        else:
            per = rt // conv_ref.shape[1]
            c = conv_ref[pl.ds(pl.multiple_of(r * per, per), per)].reshape(rt, conv_ref.shape[2])
        mu = jnp.mean(c, axis=-1, keepdims=True)
        cc = c - mu
        var = jnp.mean(cc * cc, axis=-1, keepdims=True)
        y = cc * lax.rsqrt(var + EPS) * lg_ref[...] + lb_ref[...]
        act_ref[pl.ds(r0, rt), :] = (y * _sigmoid(y)).astype(_bf16)
        return carry

    lax.fori_loop(0, rows // rt, body, 0, unroll=4)
    o_ref[...] = _dot(act_ref[...], w_ref[...])


def _conv_prompt_kernel(glu_ref, k_ref, cb_ref, lg_ref, lb_ref, w_ref, o_ref, ext_ref, conv_ref, act_ref, *, tm, width):
    d = glu_ref.shape[1]
    i = pl.program_id(1)

    @pl.when(i == 0)
    def _():
        ext_ref[0:HALO, :] = jnp.zeros((HALO, d), _f32)

    @pl.when(i > 0)
    def _():
        ext_ref[0:HALO, :] = ext_ref[tm:tm + HALO, :]

    ext_ref[HALO:HALO + tm, :] = glu_ref[...]

    rt, ct, sub = 64, 128, 8
    first = HALO - (width - 1)

    def body(r, carry):
        r0 = pl.multiple_of(r * rt, rt)
        for c in range(d // ct):
            cs = slice(c * ct, (c + 1) * ct)
            blk = ext_ref[pl.ds(r0, rt + HALO), cs]
            acc = jnp.broadcast_to(cb_ref[:, cs], (rt, ct))
            for phase in range(sub):
                taps = [w for w in range(width) if (first + w) % sub == phase]
                rows = rt if phase == 0 else rt + sub
                part = None
                for w in taps:
                    lo = first + w - phase
                    term = k_ref[w:w + 1, cs] * blk[lo:lo + rows, :]
                    part = term if part is None else part + term
                acc = acc + part[phase:phase + rt, :]
            conv_ref[pl.ds(r0, rt), cs] = acc
        return carry

    lax.fori_loop(0, tm // rt, body, 0)
    _ln_silu_proj(conv_ref, act_ref, lg_ref, lb_ref, w_ref, o_ref, tm)


def _conv_prompt(glu, conv_k, conv_b, ln_g, ln_b, w, nseq, tm=256):
    m, d = glu.shape
    nt = m // nseq // tm
    width = conv_k.shape[0]
    const = lambda b, i: (0, 0)
    return pl.pallas_call(
        functools.partial(_conv_prompt_kernel, tm=tm, width=width),
        grid=(nseq, nt),
        in_specs=[
            pl.BlockSpec((tm, d), lambda b, i: (b * nt + i, 0)),
            pl.BlockSpec((width, d), const),
            pl.BlockSpec((1, d), const),
            pl.BlockSpec((1, d), const),
            pl.BlockSpec((1, d), const),
            pl.BlockSpec((d, d), const),
        ],
        out_specs=pl.BlockSpec((tm, d), lambda b, i: (b * nt + i, 0)),
        out_shape=jax.ShapeDtypeStruct((m, d), _f32),
        scratch_shapes=[pltpu.VMEM((tm + HALO, d), _f32), pltpu.VMEM((tm, d), _f32), pltpu.VMEM((tm, d), _bf16)],
        compiler_params=_params("arbitrary", "arbitrary"),
        name="conv_prompt",
    )(glu, conv_k, conv_b, ln_g, ln_b, w)


def _conv_sample_kernel(st_ref, glu_ref, k_ref, cb_ref, lg_ref, lb_ref, w_ref, o_ref, ns_ref,
                        ext_ref, conv_ref, act_ref, *, nb, t, width):
    d = glu_ref.shape[2]
    hist = width - 1
    ct, sub = 128, 8

    def history(n, carry):
        ext_ref[0:hist, :] = st_ref[0, n]
        ext_ref[hist:hist + t, :] = glu_ref[n]
        ns_ref[0, n] = ext_ref[t:t + hist, :]
        return carry

    lax.fori_loop(0, nb, history, 0, unroll=2)

    def group(gi, carry):
        seqs = pl.ds(pl.multiple_of(gi * sub, sub), sub)
        for c in range(d // ct):
            cs = slice(c * ct, (c + 1) * ct)
            x = [st_ref[0, seqs, j, cs] for j in range(hist)] + [glu_ref[seqs, j, cs] for j in range(t)]
            for tt in range(t):
                acc = jnp.broadcast_to(cb_ref[:, cs], (sub, ct))
                for w in range(width):
                    acc = acc + k_ref[w:w + 1, cs] * x[tt + w]
                conv_ref[seqs, tt, cs] = acc
        return carry

    lax.fori_loop(0, nb // sub, group, 0)
    _ln_silu_proj(conv_ref, act_ref, lg_ref, lb_ref, w_ref, o_ref, nb * t)


def _conv_sample(state, layer, glu, conv_k, conv_b, ln_g, ln_b, w, nb=32):
    _, n, hist, d = state.shape
    width = conv_k.shape[0]
    t = glu.shape[0] // n
    assert hist == width - 1 and t <= hist
    const = lambda i: (0, 0)
    return pl.pallas_call(
        functools.partial(_conv_sample_kernel, nb=nb, t=t, width=width),
        grid=(n // nb,),
        in_specs=[
            pl.BlockSpec((1, nb, hist, d), lambda i: (layer, i, 0, 0)),
            pl.BlockSpec((nb, t, d), lambda i: (i, 0, 0)),
            pl.BlockSpec((width, d), const),
            pl.BlockSpec((1, d), const),
            pl.BlockSpec((1, d), const),
            pl.BlockSpec((1, d), const),
            pl.BlockSpec((d, d), const),
        ],
        out_specs=[
            pl.BlockSpec((nb * t, d), lambda i: (i, 0)),
            pl.BlockSpec((1, nb, hist, d), lambda i: (0, i, 0, 0)),
        ],
        out_shape=[
            jax.ShapeDtypeStruct((n * t, d), _f32),
            jax.ShapeDtypeStruct((1, n, hist, d), _f32),
        ],
        scratch_shapes=[pltpu.VMEM((hist + t + (-(hist + t)) % 8, d), _f32), pltpu.VMEM((nb, t, d), _f32),
                        pltpu.VMEM((nb * t, d), _bf16)],
        compiler_params=_params("arbitrary"),
        name="conv_sample",
    )(state, glu.reshape(n, t, d), conv_k, conv_b, ln_g, ln_b, w)


def _attn_prompt_kernel(q_ref, kc_ref, kp_ref, vc_ref, vp_ref, bias_ref, sink_ref, *rest, kv_heads, hd, side):
    o_ref = rest[side]
    for side_ref, side_out_ref in zip(rest[:side], rest[side + 1:]):
        side_out_ref[...] = side_ref[...].astype(_bf16)
    blk = q_ref.shape[0]
    grp = 2 * hd
    pairs = q_ref.shape[1] // (kv_heads * grp)
    cols = pairs * blk
    scale = hd ** -0.5
    nt = (((1,), (1,)), ((), ()))
    low = lax.broadcasted_iota(jnp.int32, (2 * blk, grp), 1) < hd
    qi = lax.broadcasted_iota(jnp.int32, (blk, cols), 1) & (blk - 1)
    cur_visible = lax.broadcasted_iota(jnp.int32, (blk, cols), 0) <= qi

    def halves(p_ref, c_ref, h):
        g0 = (h // 2) * grp
        x = jnp.concatenate([p_ref[:, g0:g0 + grp], c_ref[:, g0:g0 + grp]], axis=0)
        swapped = pltpu.roll(x, hd, axis=1)
        in_low, in_high = (x, swapped) if h % 2 == 0 else (swapped, x)
        return jnp.where(low, in_low, 0.0), jnp.where(low, 0.0, in_high)

    for h in range(kv_heads):
        ks = [x.astype(_bf16) for x in halves(kp_ref, kc_ref, h)]
        vts = [x.T.astype(_bf16) for x in halves(vp_ref, vc_ref, h)]
        lanes = [slice((h * pairs + p) * grp, (h * pairs + p + 1) * grp) for p in range(pairs)]
        qs = (jnp.concatenate([q_ref[:, ls] for ls in lanes], axis=0) * scale).astype(_bf16)
        out_t = None
        for parity in range(2):
            s2 = lax.dot_general(ks[parity], qs, nt, preferred_element_type=_f32)
            s = jnp.where(cur_visible, s2[blk:, :], s2[:blk, :]) + bias_ref[0, h, parity]
            sink = sink_ref[h, parity]
            m = jnp.maximum(jnp.max(s, axis=0, keepdims=True), sink)
            p = jnp.exp(s - m)
            den = jnp.sum(p, axis=0, keepdims=True) + jnp.exp(sink - m)
            pn = p * (1.0 / den)
            p2 = jnp.concatenate([jnp.where(cur_visible, 0.0, pn), jnp.where(cur_visible, pn, 0.0)], axis=0)
            o_t = _dot(vts[parity], p2.astype(_bf16))
            out_t = o_t if out_t is None else out_t + o_t
        out = out_t.T
        for p, ls in enumerate(lanes):
            o_ref[:, ls] = out[p * blk:(p + 1) * blk, :]


def _attn_prompt(qkv, bias, sinks, nseq, n_heads, kv_heads, hd, sides=()):
    m = qkv.shape[0]
    dq, dkv = n_heads * hd, kv_heads * hd
    nb = m // nseq // WINDOW
    pairs = n_heads // kv_heads // 2
    rows = pairs * WINDOW
    assert WINDOW & (WINDOW - 1) == 0 and 2 * hd == 128 and kv_heads % 2 == 0
    qi = jnp.arange(WINDOW)[:, None]
    cur_visible = jnp.arange(WINDOW)[None, :] <= qi
    folded = jnp.stack([jnp.where(cur_visible, bias[:, :, WINDOW:], MASKED),
                        jnp.where(cur_visible, bias[:, :, WINDOW:], bias[:, :, :WINDOW])])
    folded = folded.reshape(2, kv_heads, pairs, 2, WINDOW, WINDOW).transpose(0, 1, 3, 5, 2, 4)
    folded = folded.reshape(2, kv_heads, 2, WINDOW, rows)
    sink_rows = jnp.repeat(sinks.reshape(kv_heads, pairs, 2).transpose(0, 2, 1), WINDOW, axis=-1)
    sink_rows = sink_rows.reshape(kv_heads, 2, 1, rows)
    kcol, vcol = dq // dkv, dq // dkv + 1
    cur = lambda b, i: b * nb + i
    prev = lambda b, i: b * nb + jnp.maximum(i - 1, 0)
    in_specs = [
        pl.BlockSpec((WINDOW, dq), lambda b, i: (cur(b, i), 0)),
        pl.BlockSpec((WINDOW, dkv), lambda b, i: (cur(b, i), kcol)),
        pl.BlockSpec((WINDOW, dkv), lambda b, i: (prev(b, i), kcol)),
        pl.BlockSpec((WINDOW, dkv), lambda b, i: (cur(b, i), vcol)),
        pl.BlockSpec((WINDOW, dkv), lambda b, i: (prev(b, i), vcol)),
        pl.BlockSpec((1, kv_heads, 2, WINDOW, rows), lambda b, i: (jnp.minimum(i, 1), 0, 0, 0, 0)),
        pl.BlockSpec((kv_heads, 2, 1, rows), lambda b, i: (0, 0, 0, 0)),
    ]
    out_specs = [pl.BlockSpec((WINDOW, dq), lambda b, i: (cur(b, i), 0))]
    out_shape = [jax.ShapeDtypeStruct((m, dq), _f32)]
    operands = [qkv, qkv, qkv, qkv, qkv, folded, sink_rows]
    steps = nseq * nb
    ride = [w for w in sides if w.shape[0] % (16 * steps) == 0]
    for w in ride:
        block = (w.shape[0] // steps, w.shape[1])
        in_specs.append(pl.BlockSpec(block, lambda b, i: (cur(b, i), 0)))
        out_specs.append(pl.BlockSpec(block, lambda b, i: (cur(b, i), 0)))
        out_shape.append(jax.ShapeDtypeStruct(w.shape, _bf16))
        operands.append(w)
    outs = pl.pallas_call(
        functools.partial(_attn_prompt_kernel, kv_heads=kv_heads, hd=hd, side=len(ride)),
        grid=(nseq, nb),
        in_specs=in_specs,
        out_specs=out_specs,
        out_shape=out_shape,
        compiler_params=_params("arbitrary", "arbitrary"),
        name="attn_prompt",
    )(*operands)
    cast = iter(outs[1:])
    return outs[0], [next(cast) if any(w is r for r in ride) else w.astype(_bf16) for w in sides]


def _attn_sample_kernel(qkv_ref, ck_ref, cv_ref, bias_ref, sink_ref, o_ref, ko_ref, vo_ref, *,
                        nb, t, n_heads, kv_heads, hd):
    dq, dkv = n_heads * hd, kv_heads * hd
    group = n_heads // kv_heads
    keep = ck_ref.shape[1]
    keys = bias_ref.shape[0]
    grp = 2 * hd
    scale = hd ** -0.5
    nt = (((1,), (1,)), ((), ()))
    low = lax.broadcasted_iota(jnp.int32, (t, grp), 1) < hd
    blank = jnp.zeros((t, grp), _f32)
    pad = jnp.zeros((keys - keep - t, dkv), _f32)

    def body(n, carry):
        r0 = pl.multiple_of(n * t, t)
        q = qkv_ref[pl.ds(r0, t), 0:dq]
        kn = qkv_ref[pl.ds(r0, t), dq:dq + dkv]
        vn = qkv_ref[pl.ds(r0, t), dq + dkv:dq + 2 * dkv]
        ck, cv = ck_ref[n], cv_ref[n]
        ko_ref[n, 0:keep - t, :] = ck[t:, :]
        ko_ref[n, keep - t:keep, :] = kn
        vo_ref[n, 0:keep - t, :] = cv[t:, :]
        vo_ref[n, keep - t:keep, :] = vn
        kp = jnp.concatenate([ck, kn, pad], axis=0).astype(_bf16)
        vp = jnp.concatenate([cv, vn, pad], axis=0).astype(_bf16)

        rows = []
        for h in range(kv_heads):
            for pair in range(group // 2):
                qv = q[:, (h * (group // 2) + pair) * grp:(h * (group // 2) + pair + 1) * grp]
                swapped = pltpu.roll(qv, hd, axis=1)
                for parity in range(2):
                    x = qv if parity == h % 2 else swapped
                    x = jnp.where(low, x, 0.0) if h % 2 == 0 else jnp.where(low, 0.0, x)
                    pieces = [blank] * (dkv // grp)
                    pieces[h // 2] = x
                    rows.append(jnp.concatenate(pieces, axis=1))
        wt = (jnp.concatenate(rows, axis=0) * scale).astype(_bf16)

        st = lax.dot_general(kp, wt, nt, preferred_element_type=_f32) + bias_ref[...]
        sink = sink_ref[...]
        m = jnp.maximum(jnp.max(st, axis=0, keepdims=True), sink)
        p = jnp.exp(st - m)
        den = jnp.sum(p, axis=0, keepdims=True) + jnp.exp(sink - m)
        pn = p * (1.0 / den)
        of = _dot(pn.T.astype(_bf16), vp)

        outs = []
        for h in range(kv_heads):
            ls = slice((h // 2) * grp, (h // 2 + 1) * grp)
            for pair in range(group // 2):
                r = (h * group + 2 * pair) * t
                even, odd = of[r:r + t, ls], of[r + t:r + 2 * t, ls]
                if h % 2 == 0:
                    outs.append(jnp.where(low, even, pltpu.roll(odd, hd, axis=1)))
                else:
                    outs.append(jnp.where(low, pltpu.roll(even, hd, axis=1), odd))
        o_ref[pl.ds(r0, t), :] = jnp.concatenate(outs, axis=1)
        return carry

    lax.fori_loop(0, nb, body, 0, unroll=2)


def _attn_sample(qkv, cache_k, cache_v, bias, sinks, t, n_heads, kv_heads, hd, nb=8):
    n, keep, dkv = cache_k.shape
    dq = n_heads * hd
    keys = bias.shape[2]
    assert keep == WINDOW and keep + t <= keys and 2 * hd == 128 and kv_heads % 2 == 0
    bias_t = bias[:, :t].transpose(2, 0, 1).reshape(keys, n_heads * t)
    sink_row = jnp.repeat(sinks, t).reshape(1, n_heads * t)
    const = lambda i: (0, 0)
    return pl.pallas_call(
        functools.partial(_attn_sample_kernel, nb=nb, t=t, n_heads=n_heads, kv_heads=kv_heads, hd=hd),
        grid=(n // nb,),
        in_specs=[
            pl.BlockSpec((nb * t, qkv.shape[1]), lambda i: (i, 0)),
            pl.BlockSpec((nb, keep, dkv), lambda i: (i, 0, 0)),
            pl.BlockSpec((nb, keep, dkv), lambda i: (i, 0, 0)),
            pl.BlockSpec((keys, n_heads * t), const),
            pl.BlockSpec((1, n_heads * t), const),
        ],
        out_specs=[
            pl.BlockSpec((nb * t, dq), lambda i: (i, 0)),
            pl.BlockSpec((nb, keep, dkv), lambda i: (i, 0, 0)),
            pl.BlockSpec((nb, keep, dkv), lambda i: (i, 0, 0)),
        ],
        out_shape=[
            jax.ShapeDtypeStruct((n * t, dq), _f32),
            jax.ShapeDtypeStruct((n, keep, dkv), _f32),
            jax.ShapeDtypeStruct((n, keep, dkv), _f32),
        ],
        compiler_params=_params("arbitrary"),
        name="attn_sample",
    )(qkv, cache_k, cache_v, bias_t, sink_row)


def _mix_kernel(attn_ref, conv_ref, gc_ref, ga_ref, x_ref, wa_ref, wo_ref, g_ref, h_ref):
    ao = _dot(attn_ref[...].astype(_bf16), wa_ref[...])
    mixed = gc_ref[...] * conv_ref[...] + ga_ref[...] * ao
    o = _dot(mixed.astype(_bf16), wo_ref[...])
    h_ref[...] = x_ref[...] + _rms(o, g_ref[...])


def _mix(attn, conv_out, gates, x, w_attn, w_out, g_post, tm=256):
    m, d = x.shape
    row = lambda i: (i, 0)
    const = lambda i: (0, 0)
    once = pl.Buffered(1)
    return pl.pallas_call(
        _mix_kernel,
        grid=(m // tm,),
        in_specs=[
            pl.BlockSpec((tm, d), row),
            pl.BlockSpec((tm, d), row),
            pl.BlockSpec((tm, d), lambda i: (i, 0)),
            pl.BlockSpec((tm, d), lambda i: (i, 1)),
            pl.BlockSpec((tm, d), row),
            pl.BlockSpec((d, d), const, pipeline_mode=once),
            pl.BlockSpec((d, d), const, pipeline_mode=once),
            pl.BlockSpec((1, d), const),
        ],
        out_specs=pl.BlockSpec((tm, d), row),
        out_shape=jax.ShapeDtypeStruct((m, d), _f32),
        compiler_params=_params("arbitrary"),
        name="mix",
    )(attn, conv_out, gates, gates, x, w_attn, w_out, g_post)


def _gelu_tanh(x):
    return 0.5 * x * (1.0 + jnp.tanh(math.sqrt(2.0 / math.pi) * (x + 0.044715 * (x * x * x))))


MXU_COLUMNS = 256
FFN_CHUNK = 1024


def _chunks(n, width=MXU_COLUMNS):
    return [slice(a, a + width) for a in range(0, n, width)]


def _ffn_chunks(tc, act_ref, wd_ref, acc_ref, gate_val):
    for cs in _chunks(tc):
        gate, val = gate_val(cs)
        act_ref[:, cs] = (_gelu_tanh(gate) * val).astype(_bf16)
    acc_ref[...] += _dot(act_ref[...], wd_ref[...])


def _pack_conv_params(ffn_k, ffn_b, nc, tc):
    rows = jnp.concatenate([ffn_k, ffn_b, jnp.zeros((8 - ffn_k.shape[0] - 1, ffn_k.shape[1]), _f32)], axis=0)
    return rows.reshape(8, 2, nc, tc).transpose(2, 0, 1, 3).reshape(nc, 8, 2 * tc)


def _conv_params(cp, tc):
    return cp[0:3, :tc], cp[0:3, tc:], cp[3:4, :tc], cp[3:4, tc:]


def _ffn_finish(c, nc, acc_ref, h_ref, gp_ref, y_ref):
    @pl.when(c == nc - 1)
    def _():
        y_ref[...] = h_ref[...] + _rms(acc_ref[...], gp_ref[...])


def _ffn_prompt_kernel(h_ref, g_ref, w_ref, cp_ref, wd_ref, gp_ref,
                       y_ref, tg_ref, tv_ref, hn_ref, acc_ref, act_ref, ug_ref, uv_ref, cg_ref, cv_ref, *, tm, nc):
    i, c = pl.program_id(1), pl.program_id(2)
    tc = w_ref.shape[2] // 2
    kg, kv, bg, bv = _conv_params(cp_ref[c], tc)

    @pl.when(c == 0)
    def _():
        hn_ref[...] = _rms(h_ref[...], g_ref[...]).astype(_bf16)
        acc_ref[...] = jnp.zeros_like(acc_ref)

    @pl.when(i == 0)
    def _():
        ug_ref[0:8, :] = jnp.zeros((8, tc), _f32)
        uv_ref[0:8, :] = jnp.zeros((8, tc), _f32)

    @pl.when(i > 0)
    def _():
        ug_ref[0:8, :] = cg_ref[c]
        uv_ref[0:8, :] = cv_ref[c]

    def conv(u_ref, col0, k_ref, b_ref, cs):
        u_ref[8:8 + tm, cs] = _dot(hn_ref[...], w_ref[0, :, col0 + cs.start:col0 + cs.stop])
        return (k_ref[2:3, cs] * u_ref[8:8 + tm, cs] + k_ref[1:2, cs] * u_ref[7:7 + tm, cs]
                + k_ref[0:1, cs] * u_ref[6:6 + tm, cs] + b_ref[:, cs])

    _ffn_chunks(tc, act_ref, wd_ref, acc_ref,
                lambda cs: (conv(ug_ref, 0, kg, bg, cs), conv(uv_ref, tc, kv, bv, cs)))
    cg_ref[c] = ug_ref[tm:tm + 8, :]
    cv_ref[c] = uv_ref[tm:tm + 8, :]
    tg_ref[0, 0] = ug_ref[tm + 6:tm + 8, :]
    tv_ref[0, 0] = uv_ref[tm + 6:tm + 8, :]
    _ffn_finish(c, nc, acc_ref, h_ref, gp_ref, y_ref)


def _ffn_prompt(h, g_pre, w_up, ffn_k, ffn_b, w_down, g_post, nseq, tm=512):
    m, d = h.shape
    dff = w_down.shape[0]
    nc, tc = w_up.shape[0], w_up.shape[2] // 2
    nt = m // nseq // tm
    width = ffn_k.shape[0]
    assert width == 3 and nc * tc == dff
    row = lambda b, i, c: (b * nt + i, 0)
    const = lambda b, i, c: (0, 0)
    y, tg, tv = pl.pallas_call(
        functools.partial(_ffn_prompt_kernel, tm=tm, nc=nc),
        grid=(nseq, nt, nc),
        in_specs=[
            pl.BlockSpec((tm, d), row),
            pl.BlockSpec((1, d), const),
            pl.BlockSpec((1, d, 2 * tc), lambda b, i, c: (c, 0, 0)),
            pl.BlockSpec((nc, 8, 2 * tc), lambda b, i, c: (0, 0, 0)),
            pl.BlockSpec((tc, d), lambda b, i, c: (c, 0)),
            pl.BlockSpec((1, d), const),
        ],
        out_specs=[
            pl.BlockSpec((tm, d), row, pipeline_mode=pl.Buffered(1)),
            pl.BlockSpec((1, 1, width - 1, tc), lambda b, i, c: (b, i, 0, c)),
            pl.BlockSpec((1, 1, width - 1, tc), lambda b, i, c: (b, i, 0, c)),
        ],
        out_shape=[
            jax.ShapeDtypeStruct((m, d), _f32),
            jax.ShapeDtypeStruct((nseq, nt, width - 1, dff), _f32),
            jax.ShapeDtypeStruct((nseq, nt, width - 1, dff), _f32),
        ],
        scratch_shapes=[
            pltpu.VMEM((tm, d), _bf16),
            pltpu.VMEM((tm, d), _f32),
            pltpu.VMEM((tm, tc), _bf16),
            pltpu.VMEM((tm + 8, tc), _f32),
            pltpu.VMEM((tm + 8, tc), _f32),
            pltpu.VMEM((nc, 8, tc), _f32),
            pltpu.VMEM((nc, 8, tc), _f32),
        ],
        compiler_params=_params("arbitrary", "arbitrary", "arbitrary"),
        name="ffn_prompt",
    )(h, g_pre, w_up, _pack_conv_params(ffn_k, ffn_b, nc, tc), w_down, g_post)
    return y, jnp.concatenate([tg[:, -1], tv[:, -1]], axis=-1)


def _ffn_sample_kernel(h_ref, g_ref, w_ref, cp_ref, sg_ref, sv_ref, wd_ref, gp_ref,
                       y_ref, tg_ref, tv_ref, hn_ref, acc_ref, act_ref, ug_ref, uv_ref, *, nb, t, nc):
    c = pl.program_id(1)
    tc = w_ref.shape[2] // 2
    kg, kv, bg, bv = _conv_params(cp_ref[c], tc)

    @pl.when(c == 0)
    def _():
        hn_ref[...] = _rms(h_ref[...], g_ref[...]).astype(_bf16)
        acc_ref[...] = jnp.zeros_like(acc_ref)

    ug_ref[:, 6:8, :] = sg_ref[...]
    uv_ref[:, 6:8, :] = sv_ref[...]

    def conv(u_ref, col0, k_ref, b_ref, cs):
        width = cs.stop - cs.start
        u_ref[:, 8:8 + t, cs] = _dot(hn_ref[...], w_ref[0, :, col0 + cs.start:col0 + cs.stop]).reshape(nb, t, width)
        cv = (k_ref[2:3, cs] * u_ref[:, 8:8 + t, cs] + k_ref[1:2, cs] * u_ref[:, 7:7 + t, cs]
              + k_ref[0:1, cs] * u_ref[:, 6:6 + t, cs] + b_ref[:, cs])
        return cv.reshape(nb * t, width)

    _ffn_chunks(tc, act_ref, wd_ref, acc_ref,
                lambda cs: (conv(ug_ref, 0, kg, bg, cs), conv(uv_ref, tc, kv, bv, cs)))
    tg_ref[...] = ug_ref[:, 6 + t:8 + t, :]
    tv_ref[...] = uv_ref[:, 6 + t:8 + t, :]
    _ffn_finish(c, nc, acc_ref, h_ref, gp_ref, y_ref)


def _ffn_sample(h, state, g_pre, w_up, ffn_k, ffn_b, w_down, g_post, t, nb=64):
    m, d = h.shape
    n = m // t
    dff = w_down.shape[0]
    nc, tc = w_up.shape[0], w_up.shape[2] // 2
    width = ffn_k.shape[0]
    assert width == 3 and t >= width - 1 and nc * tc == dff
    tm = nb * t
    row = lambda i, c: (i, 0)
    const = lambda i, c: (0, 0)
    y, tg, tv = pl.pallas_call(
        functools.partial(_ffn_sample_kernel, nb=nb, t=t, nc=nc),
        grid=(n // nb, nc),
        in_specs=[
            pl.BlockSpec((tm, d), row, pipeline_mode=pl.Buffered(1)),
            pl.BlockSpec((1, d), const),
            pl.BlockSpec((1, d, 2 * tc), lambda i, c: (c, 0, 0)),
            pl.BlockSpec((nc, 8, 2 * tc), lambda i, c: (0, 0, 0)),
            pl.BlockSpec((nb, width - 1, tc), lambda i, c: (i, 0, c)),
            pl.BlockSpec((nb, width - 1, tc), lambda i, c: (i, 0, c + nc)),
            pl.BlockSpec((tc, d), lambda i, c: (c, 0)),
            pl.BlockSpec((1, d), const),
        ],
        out_specs=[
            pl.BlockSpec((tm, d), row, pipeline_mode=pl.Buffered(1)),
            pl.BlockSpec((nb, width - 1, tc), lambda i, c: (i, 0, c)),
            pl.BlockSpec((nb, width - 1, tc), lambda i, c: (i, 0, c)),
        ],
        out_shape=[
            jax.ShapeDtypeStruct((m, d), _f32),
            jax.ShapeDtypeStruct((n, width - 1, dff), _f32),
            jax.ShapeDtypeStruct((n, width - 1, dff), _f32),
        ],
        scratch_shapes=[
            pltpu.VMEM((tm, d), _bf16),
            pltpu.VMEM((tm, d), _f32),
            pltpu.VMEM((tm, tc), _bf16),
            pltpu.VMEM((nb, 8 + t, tc), _f32),
            pltpu.VMEM((nb, 8 + t, tc), _f32),
        ],
        compiler_params=_params("arbitrary", "arbitrary"),
        name="ffn_sample",
    )(h, g_pre, w_up, _pack_conv_params(ffn_k, ffn_b, nc, tc), state, state, w_down, g_post)
    return y, jnp.concatenate([tg, tv], axis=-1)


def _rel_bucket(dist):
    max_exact = N_BUCKETS // 2
    dd = jnp.maximum(dist, 1).astype(_f32)
    large = max_exact + (jnp.log(dd / max_exact) / math.log(MAX_DISTANCE / max_exact)
                         * (N_BUCKETS - max_exact)).astype(jnp.int32)
    large = jnp.minimum(large, N_BUCKETS - 1)
    return jnp.where(dist < max_exact, dist, large)


def _bias_table(rel_bias):
    n = WINDOW
    by_dist = rel_bias[_rel_bucket(jnp.arange(n))].astype(_f32).T
    g = jnp.concatenate([jnp.full_like(by_dist, MASKED), by_dist[:, ::-1],
                         jnp.full_like(by_dist, MASKED)], axis=1)
    heads, length = g.shape
    skew = jnp.broadcast_to(g[:, None, :], (heads, n, length)).reshape(heads, n * length)
    skew = skew[:, :n * (length - 1)].reshape(heads, n, length - 1)
    return skew[:, :, n - 1:3 * n - 1]


def _row(v):
    return v.reshape(1, -1)


def _layer(xp, xs, cache_k, cache_v, state_conv, layer, st_ffn, p):
    (g_pre, w_in, b_in, conv_k, conv_b, ln_g, ln_b, w_conv, sinks, rel_bias, w_attn, w_out, g_post,
     g_ffn_pre, w_up, ffn_k, ffn_b, w_down, g_ffn_post) = p
    nseq, seq, d = xp.shape
    t = xs.shape[1]
    n_heads = sinks.shape[0]
    keep, kv_heads, hd = cache_k.shape[1:]
    dq, dkv = n_heads * hd, kv_heads * hd
    d_conv = conv_k.shape[1]
    assert d_conv == d and dq == d and keep == WINDOW and seq % WINDOW == 0

    w_in_b = _column_tiles(w_in, 512)
    ffn_tile = min(FFN_CHUNK, w_down.shape[0])
    w_up_b = None
    g_pre, b_in, conv_b, ln_g, ln_b, g_post, g_ffn_pre, ffn_b, g_ffn_post = map(
        _row, (g_pre, b_in, conv_b, ln_g, ln_b, g_post, g_ffn_pre, ffn_b, g_ffn_post))

    bias = _bias_table(rel_bias)

    outs = []
    for x3, is_prompt in ((xp, True), (xs, False)):
        n = x3.shape[0]
        x = x3.reshape(-1, d)
        glu, qkv, gates, tiles = _inproj(x, g_pre, w_in_b, b_in, d_conv, dq + 2 * dkv, 2 * d,
                                         side=w_up if w_up_b is None else None, side_tile=ffn_tile)
        if w_up_b is None:
            w_up_b = tiles if tiles is not None else _column_tiles(w_up, ffn_tile, groups=2)
        if is_prompt:
            attn, (w_down_b, w_conv_b, w_attn_b, w_out_b) = _attn_prompt(
                qkv, bias, sinks.astype(_f32), nseq, n_heads, kv_heads, hd, sides=(w_down, w_conv, w_attn, w_out))
            conv_out = _conv_prompt(glu, conv_k, conv_b, ln_g, ln_b, w_conv_b, nseq)
            new_conv = glu.reshape(n, seq, d)[:, seq - (conv_k.shape[0] - 1):]
            last = qkv.reshape(n, seq, -1)[:, seq - keep:]
            new_k = last[:, :, dq:dq + dkv].reshape(n, keep, kv_heads, hd)
            new_v = last[:, :, dq + dkv:].reshape(n, keep, kv_heads, hd)
        else:
            conv_out, new_conv = _conv_sample(state_conv, layer, glu, conv_k, conv_b, ln_g, ln_b, w_conv_b)
            new_conv = new_conv[0]
            attn, new_k, new_v = _attn_sample(qkv, cache_k.reshape(n, keep, dkv), cache_v.reshape(n, keep, dkv), bias,
                                              sinks.astype(_f32), t, n_heads, kv_heads, hd)
            new_k, new_v = (a.reshape(n, keep, kv_heads, hd) for a in (new_k, new_v))
        h = _mix(attn, conv_out, gates, x, w_attn_b, w_out_b, g_post)
        if is_prompt:
            y, new_ffn = _ffn_prompt(h, g_ffn_pre, w_up_b, ffn_k, ffn_b, w_down_b, g_ffn_post, nseq)
        else:
            y, new_ffn = _ffn_sample(h, st_ffn, g_ffn_pre, w_up_b, ffn_k, ffn_b, w_down_b, g_ffn_post, t)
        outs.append((y.reshape(x3.shape), new_k, new_v, new_conv, new_ffn))
    return outs


def kernel(x_prompt, x_sample, cache_k, cache_v, state_conv, state_ffn_conv, norm_mix_pre, w_in, b_in, conv_dw_k, conv_dw_b, conv_ln_g, conv_ln_b, w_conv_proj, attn_sinks, rel_bias, w_attn_proj, w_out, norm_mix_post, norm_ffn_pre, w_up, ffn_dw_k, ffn_dw_b, w_down, norm_ffn_post):
    y_p, y_s = x_prompt, x_sample
    per_layer = []
    for l in range(w_in.shape[0]):
        p = (norm_mix_pre[l], w_in[l], b_in[l], conv_dw_k[l], conv_dw_b[l], conv_ln_g[l], conv_ln_b[l], w_conv_proj[l],
             attn_sinks[l], rel_bias, w_attn_proj[l], w_out[l], norm_mix_post[l],
             norm_ffn_pre[l], w_up[l], ffn_dw_k[l], ffn_dw_b[l], w_down[l], norm_ffn_post[l])
        (y_p, *rest_p), (y_s, *rest_s) = _layer(y_p, y_s, cache_k[l], cache_v[l], state_conv, l, state_ffn_conv[l], p)
        per_layer.append(rest_p + rest_s)
    stacked = [jnp.stack(leaves) for leaves in zip(*per_layer)]
    return (y_p, y_s, *stacked)
```
